```python
import math, functools
import jax, jax.numpy as jnp
from jax import lax
import numpy as np

D_MODEL = 2048
BATCH = 4
SEQ = 4096
DEPTH = 4

GRID_W = 64
CTX_LEN = 256
Q_BLOCK = 128
ROPE_DIM = 64
ROPE_BASE = 10000.0

DA_HEADS = 8
DA_HEAD_DIM = 64
DA_V_DIM = 2 * DA_HEAD_DIM
S5_CHANNELS = 512
S5_GROUP = 16
S5_GROUPS = S5_CHANNELS // S5_GROUP
S5_STATE = 64
MLA_HEADS = 8
MLA_Q_RANK = 512
MLA_KV_RANK = 256
MLA_NOPE = 128
MLA_ROPE = ROPE_DIM
MLA_V = 128
GDN_HEADS = 8
GDN_DK = 128
GDN_DV = 128
GDN_CONV = 3
GDN_CHUNK = 64
N_EXPERTS = 16
N_GROUPS = 4
EXPERTS_PER_GROUP = N_EXPERTS // N_GROUPS
TOP_K = 2
D_FF_EXPERT = 1024

N_EVEN = (DEPTH + 1) // 2
N_ODD = DEPTH // 2
DEEPNORM_ALPHA = (2 * DEPTH) ** 0.25
DEEPNORM_BETA = (8 * DEPTH) ** -0.25

EVEN_QK = 2 * DA_HEADS * DA_HEAD_DIM
EVEN_IN = 2 * EVEN_QK + DA_HEADS * DA_V_DIM + S5_CHANNELS
EVEN_MIX = DA_HEADS * DA_V_DIM + S5_CHANNELS
GDN_QKV = 2 * GDN_HEADS * GDN_DK + GDN_HEADS * GDN_DV
ODD_SPLIT = (MLA_Q_RANK, MLA_KV_RANK, MLA_ROPE, GDN_QKV, GDN_HEADS * GDN_DV, GDN_HEADS, GDN_HEADS, GDN_HEADS, GDN_HEADS)
ODD_IN = sum(ODD_SPLIT)
ODD_MIX = MLA_HEADS * MLA_V + GDN_HEADS * GDN_DV

kernel_name = 'hybrid_diffattn_s5_mla_gdn_moe_dit'

f32 = jnp.float32


def layer_norm(x, g, b, eps=1e-5):
    xf = x.astype(f32)
    mu = jnp.mean(xf, axis=-1, keepdims=True)
    var = jnp.mean(jnp.square(xf - mu), axis=-1, keepdims=True)
    return ((xf - mu) * lax.rsqrt(var + eps) * g.astype(f32) + b.astype(f32)).astype(x.dtype)


def rms_norm(x, w, eps=1e-6):
    xf = x.astype(f32)
    y = xf * lax.rsqrt(jnp.mean(jnp.square(xf), axis=-1, keepdims=True) + eps)
    return (y * w.astype(f32)).astype(x.dtype)


def l2norm(x, eps=1e-6):
    return x * lax.rsqrt(jnp.sum(jnp.square(x), axis=-1, keepdims=True) + eps)


def axial_rope_tables(rows):
    t_row = jnp.repeat(jnp.arange(rows), GRID_W).astype(f32)
    t_col = jnp.tile(jnp.arange(GRID_W), rows).astype(f32)
    n_freq = ROPE_DIM // 4
    inv_freq = ROPE_BASE ** (-jnp.arange(n_freq, dtype=f32) / n_freq)
    ang = jnp.concatenate([t_row[:, None] * inv_freq, t_col[:, None] * inv_freq], axis=-1)
    return jnp.cos(ang), jnp.sin(ang)


def apply_rope(x, cos, sin):
    bshape = (cos.shape[0],) + (1,) * (x.ndim - 3) + (cos.shape[1],)
    c, s = cos.reshape(bshape), sin.reshape(bshape)
    xp = x.astype(f32).reshape(x.shape[:-1] + (-1, 2))
    x1, x2 = xp[..., 0], xp[..., 1]
    out = jnp.stack([x1 * c - x2 * s, x1 * s + x2 * c], axis=-1)
    return out.reshape(x.shape).astype(x.dtype)


def softmax_mix_attention(q, k, v, map_w, scale):
    B, Sq, M, H, dk = q.shape
    nb = Sq // Q_BLOCK
    qb = jnp.swapaxes(q.reshape(B, nb, Q_BLOCK, M, H, dk), 0, 1)
    mw = map_w.astype(f32)

    def block(qi):
        s = jnp.einsum('bqmhd,bkmhd->bmhqk', qi, k).astype(f32) * scale
        p = jnp.einsum('m,bmhqk->bhqk', mw, jax.nn.softmax(s, axis=-1))
        return jnp.einsum('bhqk,bkhd->bqhd', p.astype(v.dtype), v)

    out = lax.map(block, qb)
    return jnp.swapaxes(out, 0, 1).reshape(B, Sq, H, v.shape[-1])


def _linear_recurrence(e_i, e_j):
    a_i, b_i = e_i
    a_j, b_j = e_j
    return a_j * a_i, a_j * b_i + b_j


def _s5_discretise(lam_re, lam_im, b_re, b_im, log_dt):
    lam = lax.complex(lam_re.astype(f32), lam_im.astype(f32))
    lam_dt = lam * jnp.exp(log_dt.astype(f32))[:, None]
    lam_bar = jnp.exp(lam_dt)
    b_bar = ((lam_bar - 1.0) / lam)[..., None] * lax.complex(b_re.astype(f32), b_im.astype(f32))
    return lam_dt, lam_bar, b_bar


def _s5_scan(u, lam_dt, lam_bar, b_bar, h0, reverse):
    B, L, _ = u.shape
    ug = u.astype(f32).reshape(B, L, S5_GROUPS, S5_GROUP).astype(jnp.complex64)
    bu = jnp.einsum('blgi,gpi->blgp', ug, b_bar)
    a = jnp.broadcast_to(lam_bar, bu.shape)
    _, hs = lax.associative_scan(_linear_recurrence, (a, bu), axis=1, reverse=reverse)
    if h0 is not None:
        steps = jnp.arange(L, 0, -1) if reverse else jnp.arange(1, L + 1)
        hs = hs + jnp.exp(lam_dt[None] * steps.astype(f32)[:, None, None])[None] * h0[:, None]
    final = hs[:, 0] if reverse else hs[:, -1]
    return hs, final


def _s5_readout(hs, c_mat):
    B, L = hs.shape[:2]
    return jnp.einsum('blgp,gip->blgi', hs, c_mat).real.reshape(B, L, S5_CHANNELS)


def s5_bidirectional(u_ctx, u_lat, lam_re, lam_im, b_re, b_im, c_re, c_im, log_dt, d_skip, glu_w, glu_b, need_ctx):
    y_ctx, y_lat = [], []
    for d in range(2):
        rev = d == 1
        lam_dt, lam_bar, b_bar = _s5_discretise(lam_re[d], lam_im[d], b_re[d], b_im[d], log_dt[d])
        c_mat = lax.complex(c_re[d].astype(f32), c_im[d].astype(f32))
        h_ctx, s_ctx = _s5_scan(u_ctx, lam_dt, lam_bar, b_bar, None, rev)
        h_lat, _ = _s5_scan(u_lat, lam_dt, lam_bar, b_bar, s_ctx, rev)
        y_lat.append(_s5_readout(h_lat, c_mat))
        if need_ctx:
            y_ctx.append(_s5_readout(h_ctx, c_mat))

    def finish(u, ys):
        y = ys[0] + ys[1] + d_skip.astype(f32) * u.astype(f32)
        y = jax.nn.gelu(y)
        y = y * jax.nn.sigmoid(y @ glu_w.astype(f32) + glu_b.astype(f32))
        return y.astype(u.dtype)

    out_ctx = finish(u_ctx, y_ctx) if need_ctx else None
    return out_ctx, finish(u_lat, y_lat)


def depthwise_conv_centred(x, w):
    K = w.shape[0]
    return lax.conv_general_dilated(x, w[:, None, :].astype(x.dtype), window_strides=(1,),
                                    padding=(((K - 1) // 2, K // 2),),
                                    dimension_numbers=('NWC', 'WIO', 'NWC'),
                                    feature_group_count=x.shape[-1])


def _gated_delta_chunked(q, k, v, g, beta, s0):
    B, L, H, _ = q.shape
    C = GDN_CHUNK
    N = L // C

    def chunks(t):
        return t.reshape(B, N, C, H, -1).transpose(1, 0, 3, 2, 4)

    qc, kc, vc = chunks(q), chunks(k), chunks(v)
    gc = g.reshape(B, N, C, H).transpose(1, 0, 3, 2)
    bc = beta.reshape(B, N, C, H).transpose(1, 0, 3, 2)
    gcum = jnp.cumsum(gc, axis=-1)
    idx = jnp.arange(C)
    incl = idx[:, None] >= idx[None, :]
    strict = idx[:, None] > idx[None, :]
    decay = jnp.exp(jnp.where(incl, gcum[..., :, None] - gcum[..., None, :], -jnp.inf))
    kb = kc * bc[..., None]
    m = jnp.where(strict, jnp.einsum('nbhid,nbhjd->nbhij', kb, kc) * decay, 0.0)
    tri = m + jnp.eye(C, dtype=f32)
    solve = functools.partial(lax.linalg.triangular_solve, left_side=True, lower=True, unit_diagonal=True)
    u = solve(tri, vc * bc[..., None])
    w = solve(tri, kb * jnp.exp(gcum)[..., None])
    a_intra = jnp.where(incl, jnp.einsum('nbhid,nbhjd->nbhij', qc, kc) * decay, 0.0)
    qg = qc * jnp.exp(gcum)[..., None]
    g_last = gcum[..., -1]
    kd = kc * jnp.exp(g_last[..., None] - gcum)[..., None]

    def step(S, xs):
        qg_i, kd_i, u_i, w_i, a_i, gl_i = xs
        v_new = u_i - jnp.einsum('bhck,bhkv->bhcv', w_i, S)
        o = jnp.einsum('bhck,bhkv->bhcv', qg_i, S) + jnp.einsum('bhcj,bhjv->bhcv', a_i, v_new)
        S = S * jnp.exp(gl_i)[..., None, None] + jnp.einsum('bhck,bhcv->bhkv', kd_i, v_new)
        return S, o

    s_final, o = lax.scan(step, s0, (qg, kd, u, w, a_intra, g_last))
    return o.transpose(1, 0, 3, 2, 4).reshape(B, L, H, v.shape[-1]), s_final


def _gdn_direction(q, k, v, g, beta, s0, reverse):
    if reverse:
        flip = lambda t: jnp.flip(t, axis=1)
        o, s = _gated_delta_chunked(flip(q), flip(k), flip(v), flip(g), flip(beta), s0)
        return flip(o), s
    return _gated_delta_chunked(q, k, v, g, beta, s0)


def gated_deltanet_bidirectional(pc, pl, conv_w, a_log, dt_bias, norm_w, need_ctx):
    def prep(qkv_raw, a_f, a_b, b_f, b_b):
        B, L, _ = qkv_raw.shape
        qkv = jax.nn.silu(depthwise_conv_centred(qkv_raw, conv_w)).astype(f32)
        nk = GDN_HEADS * GDN_DK
        q = l2norm(qkv[..., :nk].reshape(B, L, GDN_HEADS, GDN_DK)) * (GDN_DK ** -0.5)
        k = l2norm(qkv[..., nk:2 * nk].reshape(B, L, GDN_HEADS, GDN_DK))
        v = qkv[..., 2 * nk:].reshape(B, L, GDN_HEADS, GDN_DV)
        gates = []
        for d, (a_raw, b_raw) in enumerate(((a_f, b_f), (a_b, b_b))):
            g = -jnp.exp(a_log[d].astype(f32)) * jax.nn.softplus(a_raw.astype(f32) + dt_bias[d].astype(f32))
            gates.append((g, jax.nn.sigmoid(b_raw.astype(f32))))
        return q, k, v, gates

    qc, kc, vc, gtc = prep(pc[0], pc[2], pc[3], pc[4], pc[5])
    ql, kl, vl, gtl = prep(pl[0], pl[2], pl[3], pl[4], pl[5])
    s_zero = jnp.zeros((qc.shape[0], GDN_HEADS, GDN_DK, GDN_DV), f32)
    oc_f, sc_f = _gdn_direction(qc, kc, vc, gtc[0][0], gtc[0][1], s_zero, False)
    oc_b, sc_b = _gdn_direction(qc, kc, vc, gtc[1][0], gtc[1][1], s_zero, True)
    ol_f, _ = _gdn_direction(ql, kl, vl, gtl[0][0], gtl[0][1], sc_f, False)
    ol_b, _ = _gdn_direction(ql, kl, vl, gtl[1][0], gtl[1][1], sc_b, True)

    def finish(o, z):
        B, L = z.shape[:2]
        o = rms_norm(o, norm_w) * jax.nn.silu(z.astype(f32).reshape(B, L, GDN_HEADS, GDN_DV))
        return o.reshape(B, L, GDN_HEADS * GDN_DV).astype(z.dtype)

    y_ctx = finish(oc_f + oc_b, pc[1]) if need_ctx else None
    return y_ctx, finish(ol_f + ol_b, pl[1])


def _split_even(p):
    B, L, _ = p.shape
    q = p[..., :EVEN_QK].reshape(B, L, 2, DA_HEADS, DA_HEAD_DIM)
    k = p[..., EVEN_QK:2 * EVEN_QK].reshape(B, L, 2, DA_HEADS, DA_HEAD_DIM)
    v = p[..., 2 * EVEN_QK:2 * EVEN_QK + DA_HEADS * DA_V_DIM].reshape(B, L, DA_HEADS, DA_V_DIM)
    s = p[..., 2 * EVEN_QK + DA_HEADS * DA_V_DIM:]
    return q, k, v, s


def even_mixer(u_ctx, u_lat, w_in, w_out, da_lam, da_subln, lam_init,
               lam_re, lam_im, b_re, b_im, c_re, c_im, log_dt, d_skip, glu_w, glu_b,
               cos, sin, need_ctx):
    qc, kc, vc, sc = _split_even(u_ctx @ w_in)
    ql, kl, vl, sl = _split_even(u_lat @ w_in)
    ql, kl = apply_rope(ql, cos, sin), apply_rope(kl, cos, sin)
    lv = da_lam.astype(f32)
    lam = jnp.exp(jnp.sum(lv[0] * lv[1])) - jnp.exp(jnp.sum(lv[2] * lv[3])) + lam_init
    map_w = jnp.stack([jnp.ones((), f32), -lam])
    scale = DA_HEAD_DIM ** -0.5
    a_lat = softmax_mix_attention(ql, jnp.concatenate([kc, kl], axis=1), jnp.concatenate([vc, vl], axis=1), map_w, scale)
    y_ctx, y_lat = s5_bidirectional(sc, sl, lam_re, lam_im, b_re, b_im, c_re, c_im, log_dt, d_skip, glu_w, glu_b, need_ctx)

    def merge(a, y):
        B, L = a.shape[:2]
        a = (rms_norm(a, da_subln) * (1.0 - lam_init)).reshape(B, L, DA_HEADS * DA_V_DIM)
        return jnp.concatenate([a, y], axis=-1) @ w_out

    out_ctx = merge(softmax_mix_attention(qc, kc, vc, map_w, scale), y_ctx) if need_ctx else None
    return out_ctx, merge(a_lat, y_lat)


def _mla_qkv(cq, ckv, k_rope, q_norm, kv_norm, w_uq, w_ukv, cos, sin):
    B, L, _ = cq.shape
    q = (rms_norm(cq, q_norm) @ w_uq).reshape(B, L, MLA_HEADS, MLA_NOPE + MLA_ROPE)
    kv = (rms_norm(ckv, kv_norm) @ w_ukv).reshape(B, L, MLA_HEADS, MLA_NOPE + MLA_V)
    q_nope, q_rope = q[..., :MLA_NOPE], q[..., MLA_NOPE:]
    k_nope, v = kv[..., :MLA_NOPE], kv[..., MLA_NOPE:]
    if cos is not None:
        q_rope, k_rope = apply_rope(q_rope, cos, sin), apply_rope(k_rope, cos, sin)
    k_rope = jnp.broadcast_to(k_rope[:, :, None, :], (B, L, MLA_HEADS, MLA_ROPE))
    q = jnp.concatenate([q_nope, q_rope], axis=-1)[:, :, None]
    k = jnp.concatenate([k_nope, k_rope], axis=-1)[:, :, None]
    return q, k, v


def odd_mixer(u_ctx, u_lat, w_in, w_out, q_norm, kv_norm, w_uq, w_ukv,
              conv_w, a_log, dt_bias, gdn_norm, cos, sin, need_ctx):
    points = [int(p) for p in np.cumsum(ODD_SPLIT)[:-1]]
    pc = jnp.split(u_ctx @ w_in, points, axis=-1)
    pl = jnp.split(u_lat @ w_in, points, axis=-1)
    qc, kc, vc = _mla_qkv(pc[0], pc[1], pc[2], q_norm, kv_norm, w_uq, w_ukv, None, None)
    ql, kl, vl = _mla_qkv(pl[0], pl[1], pl[2], q_norm, kv_norm, w_uq, w_ukv, cos, sin)
    one = jnp.ones((1,), f32)
    scale = (MLA_NOPE + MLA_ROPE) ** -0.5
    m_lat = softmax_mix_attention(ql, jnp.concatenate([kc, kl], axis=1), jnp.concatenate([vc, vl], axis=1), one, scale)
    g_ctx, g_lat = gated_deltanet_bidirectional(pc[3:], pl[3:], conv_w, a_log, dt_bias, gdn_norm, need_ctx)

    def merge(a, y):
        B, L = a.shape[:2]
        return jnp.concatenate([a.reshape(B, L, MLA_HEADS * MLA_V), y], axis=-1) @ w_out

    out_ctx = merge(softmax_mix_attention(qc, kc, vc, one, scale), g_ctx) if need_ctx else None
    return out_ctx, merge(m_lat, g_lat)


def moe_ffn(u, router_w, router_bias, w_gate, w_up, w_down):
    uf = u.astype(f32)
    scores = jax.nn.sigmoid(uf @ router_w.astype(f32))
    biased = scores + router_bias.astype(f32)
    grp = biased.reshape(biased.shape[:-1] + (N_GROUPS, EXPERTS_PER_GROUP))
    grp_score = jnp.sum(lax.top_k(grp, 2)[0], axis=-1)
    best = jnp.argmax(grp_score, axis=-1)
    in_grp = jnp.repeat(jnp.arange(N_GROUPS) == best[..., None], EXPERTS_PER_GROUP, axis=-1)
    _, idx = lax.top_k(jnp.where(in_grp, biased, -jnp.inf), TOP_K)
    wsel = jnp.take_along_axis(scores, idx, axis=-1)
    wsel = wsel / jnp.sum(wsel, axis=-1, keepdims=True)
    gates = jnp.sum(jax.nn.one_hot(idx, N_EXPERTS, dtype=f32) * wsel[..., None], axis=-2)
    out = jnp.zeros(u.shape, f32)
    for e in range(N_EXPERTS):
        hid = jax.nn.silu(u @ w_gate[e]) * (u @ w_up[e])
        out = out + gates[..., e:e + 1] * (hid @ w_down[e]).astype(f32)
    return out.astype(u.dtype)


def setup_inputs(seed: int = 0) -> dict:
    key = jax.random.key(seed)
    ks = iter(jax.random.split(key, 48))
    nrm = lambda shape, scale: jax.random.normal(next(ks), shape, f32) * scale
    unif = lambda shape, lo, hi: jax.random.uniform(next(ks), shape, f32, lo, hi)
    D, NE, NO = D_MODEL, N_EVEN, N_ODD
    G, P = S5_GROUPS, S5_STATE
    dt = jnp.exp(unif((NO, 2, GDN_HEADS), math.log(1e-3), math.log(1e-1)))
    return {
        'x': nrm((BATCH, SEQ, D), 1.0),
        'c': nrm((BATCH, D), 1.0),
        'ctx': nrm((BATCH, CTX_LEN, D), 1.0),
        'c_ctx': nrm((D,), 1.0),
        'mod_w': nrm((DEPTH, D, 6 * D), 0.5 * D ** -0.5),
        'mod_b': nrm((DEPTH, 6 * D), 0.02),
        'ln_g': 1.0 + nrm((DEPTH, 2, D), 0.02),
        'ln_b': nrm((DEPTH, 2, D), 0.02),
        'e_w_in': nrm((NE, D, EVEN_IN), D ** -0.5),
        'e_w_out': nrm((NE, EVEN_MIX, D), DEEPNORM_BETA * EVEN_MIX ** -0.5),
        'da_lam': nrm((NE, 4, DA_HEAD_DIM), 0.1),
        'da_subln': 1.0 + nrm((NE, DA_V_DIM), 0.02),
        's5_lam_re': -0.5 + nrm((NE, 2, G, P), 0.01),
        's5_lam_im': math.pi * jnp.arange(P, dtype=f32) + nrm((NE, 2, G, P), 0.01),
        's5_b_re': nrm((NE, 2, G, P, S5_GROUP), (2 * S5_GROUP) ** -0.5),
        's5_b_im': nrm((NE, 2, G, P, S5_GROUP), (2 * S5_GROUP) ** -0.5),
        's5_c_re': nrm((NE, 2, G, S5_GROUP, P), P ** -0.5),
        's5_c_im': nrm((NE, 2, G, S5_GROUP, P), P ** -0.5),
        's5_log_dt': unif((NE, 2, G), math.log(1e-3), math.log(1e-1)),
        's5_d': nrm((NE, S5_CHANNELS), 1.0),
        's5_glu_w': nrm((NE, S5_CHANNELS, S5_CHANNELS), S5_CHANNELS ** -0.5),
        's5_glu_b': nrm((NE, S5_CHANNELS), 0.02),
        'o_w_in': nrm((NO, D, ODD_IN), D ** -0.5),
        'o_w_out': nrm((NO, ODD_MIX, D), DEEPNORM_BETA * ODD_MIX ** -0.5),
        'mla_q_norm': 1.0 + nrm((NO, MLA_Q_RANK), 0.02),
        'mla_kv_norm': 1.0 + nrm((NO, MLA_KV_RANK), 0.02),
        'mla_w_uq': nrm((NO, MLA_Q_RANK, MLA_HEADS * (MLA_NOPE + MLA_ROPE)), MLA_Q_RANK ** -0.5),
        'mla_w_ukv': nrm((NO, MLA_KV_RANK, MLA_HEADS * (MLA_NOPE + MLA_V)), MLA_KV_RANK ** -0.5),
        'gdn_conv': nrm((NO, GDN_CONV, GDN_QKV), GDN_CONV ** -0.5),
        'gdn_a_log': jnp.log(unif((NO, 2, GDN_HEADS), 1.0, 16.0)),
        'gdn_dt_bias': dt + jnp.log(-jnp.expm1(-dt)),
        'gdn_norm': 1.0 + nrm((NO, GDN_DV), 0.02),
        'router_w': nrm((D, N_EXPERTS), D ** -0.5),
        'router_bias': nrm((N_EXPERTS,), 0.01),
        'moe_w_gate': nrm((DEPTH, N_EXPERTS, D, D_FF_EXPERT), D ** -0.5),
        'moe_w_up': nrm((DEPTH, N_EXPERTS, D, D_FF_EXPERT), D ** -0.5),
        'moe_w_down': nrm((DEPTH, N_EXPERTS, D_FF_EXPERT, D), DEEPNORM_BETA * D_FF_EXPERT ** -0.5),
    }


def reference(x, c, ctx, c_ctx, mod_w, mod_b, ln_g, ln_b,
              e_w_in, e_w_out, da_lam, da_subln,
              s5_lam_re, s5_lam_im, s5_b_re, s5_b_im, s5_c_re, s5_c_im, s5_log_dt, s5_d, s5_glu_w, s5_glu_b,
              o_w_in, o_w_out, mla_q_norm, mla_kv_norm, mla_w_uq, mla_w_ukv,
              gdn_conv, gdn_a_log, gdn_dt_bias, gdn_norm,
              router_w, router_bias, moe_w_gate, moe_w_up, moe_w_down):
    n_lat = x.shape[1]
    rows = n_lat // GRID_W
    cos, sin = axial_rope_tables(rows)
    h = ctx
    c_act = jax.nn.silu(c)
    cc_act = jax.nn.silu(c_ctx)
    for l in range(DEPTH):
        need_ctx = l < DEPTH - 1
        m_lat = jnp.split((c_act @ mod_w[l] + mod_b[l])[:, None, :], 6, axis=-1)
        m_ctx = jnp.split(cc_act @ mod_w[l] + mod_b[l], 6, axis=-1)
        u_lat = x * (1 + m_lat[1]) + m_lat[0]
        u_ctx = h * (1 + m_ctx[1]) + m_ctx[0]
        i = l // 2
        if l % 2 == 0:
            o_ctx, o_lat = even_mixer(u_ctx, u_lat, e_w_in[i], e_w_out[i], da_lam[i], da_subln[i],
                                      0.8 - 0.6 * math.exp(-0.3 * l),
                                      s5_lam_re[i], s5_lam_im[i], s5_b_re[i], s5_b_im[i], s5_c_re[i], s5_c_im[i],
                                      s5_log_dt[i], s5_d[i], s5_glu_w[i], s5_glu_b[i], cos, sin, need_ctx)
        else:
            o_ctx, o_lat = odd_mixer(u_ctx, u_lat, o_w_in[i], o_w_out[i], mla_q_norm[i], mla_kv_norm[i],
                                     mla_w_uq[i], mla_w_ukv[i], gdn_conv[i], gdn_a_log[i], gdn_dt_bias[i],
                                     gdn_norm[i], cos, sin, need_ctx)
        x = layer_norm(DEEPNORM_ALPHA * x + m_lat[2] * o_lat, ln_g[l, 0], ln_b[l, 0])
        u_lat = x * (1 + m_lat[4]) + m_lat[3]
        f_lat = moe_ffn(u_lat, router_w, router_bias, moe_w_gate[l], moe_w_up[l], moe_w_down[l])
        x = layer_norm(DEEPNORM_ALPHA * x + m_lat[5] * f_lat, ln_g[l, 1], ln_b[l, 1])
        if need_ctx:
            h = layer_norm(DEEPNORM_ALPHA * h + m_ctx[2] * o_ctx, ln_g[l, 0], ln_b[l, 0])
            u_ctx = h * (1 + m_ctx[4]) + m_ctx[3]
            f_ctx = moe_ffn(u_ctx, router_w, router_bias, moe_w_gate[l], moe_w_up[l], moe_w_down[l])
            h = layer_norm(DEEPNORM_ALPHA * h + m_ctx[5] * f_ctx, ln_g[l, 1], ln_b[l, 1])
    return x
```

```python
import functools
import math

import jax
import jax.numpy as jnp
import numpy as np
from jax import lax
from jax.experimental import pallas as pl
from jax.experimental.pallas import tpu as pltpu

f32 = jnp.float32
bf16 = jnp.bfloat16

GRID_W = 64
ROPE_DIM = 64
ROPE_BASE = 10000.0
DA_HEADS = 8
DA_HEAD_DIM = 64
DA_V_DIM = 128
S5_CHANNELS = 512
S5_GROUP = 16
S5_GROUPS = 32
S5_STATE = 64
S5_BLOCK = 16
MLA_HEADS = 8
MLA_Q_RANK = 512
MLA_KV_RANK = 256
MLA_NOPE = 128
MLA_ROPE = 64
MLA_V = 128
GDN_HEADS = 8
GDN_DK = 128
GDN_DV = 128
GDN_QKV = 3072
GDN_CHUNK = 64
N_EXPERTS = 16
N_GROUPS = 4
EXPERTS_PER_GROUP = 4
DEPTH = 4
DEEPNORM_ALPHA = (2 * DEPTH) ** 0.25
LANE = 128
MOE_TM = 256
VMEM_LIMIT = 56 * 1024 * 1024

O_QKV, O_Z, O_CQ, O_CKV, O_KR, O_GATE, O_IN = 0, 3072, 4096, 4608, 4864, 4992, 5120


def _cp(*sem):
    return pltpu.CompilerParams(dimension_semantics=sem, vmem_limit_bytes=VMEM_LIMIT)


def _tile(n, cap, mult=16):
    best = None
    for t in range(mult, min(n, cap) + 1, mult):
        if n % t == 0:
            best = t
    assert best is not None, (n, cap, mult)
    return best


def _sigmoid(x):
    return 1.0 / (1.0 + jnp.exp(-x))


def _mod_kernel(a_ref, w_ref, b_ref, o_ref):
    a = a_ref[...]
    act = (a * _sigmoid(a)).astype(bf16)
    o_ref[...] = jnp.dot(act, w_ref[...].astype(bf16), preferred_element_type=f32) + b_ref[...]


def _modulation(cc, mod_w, mod_b):
    depth, D, N = mod_w.shape
    R = cc.shape[0]
    tn = _tile(N, 1024, LANE)
    return pl.pallas_call(
        _mod_kernel,
        grid=(depth, N // tn),
        in_specs=[pl.BlockSpec((R, D), lambda l, j: (0, 0)),
                  pl.BlockSpec((None, D, tn), lambda l, j: (l, 0, j)),
                  pl.BlockSpec((None, 1, tn), lambda l, j: (l, 0, j))],
        out_specs=pl.BlockSpec((None, R, tn), lambda l, j: (l, 0, j)),
        out_shape=jax.ShapeDtypeStruct((depth, R, N), f32),
        compiler_params=_cp("parallel", "parallel"),
        name="modulation",
    )(cc, mod_w, mod_b.reshape(depth, 1, N))


def _select_mod(ml, mc, row0, tm, C, D, chunks):
    is_ctx = (row0 + lax.broadcasted_iota(jnp.int32, (tm, 1), 0)) < C
    return [jnp.where(is_ctx, mc[:, k * D:(k + 1) * D], ml[:, k * D:(k + 1) * D]) for k in chunks]


def _rope(seg, cos, sin):
    nxt = pltpu.roll(seg, LANE - 1, axis=1)
    prv = pltpu.roll(seg, 1, axis=1)
    even = (lax.broadcasted_iota(jnp.int32, seg.shape, 1) % 2) == 0
    return seg * cos + jnp.where(even, nxt, prv) * sin


def _inproj_kernel(x_ref, ml_ref, mc_ref, w_ref, cos_ref, sin_ref, *rest, C, tm, tn, D, n_rope, n_a, has_b):
    if has_b:
        oa_ref, ob_ref, u_scr = rest
    else:
        oa_ref, u_scr = rest
        ob_ref = None
    i = pl.program_id(1)
    j = pl.program_id(2)

    @pl.when(j == 0)
    def _():
        shift, scale = _select_mod(ml_ref[...], mc_ref[...], i * tm, tm, C, D, (0, 1))
        u_scr[...] = (x_ref[...] * (1.0 + scale) + shift).astype(bf16)

    def acc():
        return jnp.dot(u_scr[...], w_ref[...], preferred_element_type=f32)

    if n_rope > 0:
        @pl.when(j < n_rope)
        def _():
            a = acc()
            cos = cos_ref[...]
            sin = sin_ref[...]
            for c in range(tn // LANE):
                oa_ref[:, c * LANE:(c + 1) * LANE] = _rope(a[:, c * LANE:(c + 1) * LANE], cos, sin).astype(oa_ref.dtype)

    @pl.when(jnp.logical_and(j >= n_rope, j < n_a))
    def _():
        oa_ref[...] = acc().astype(oa_ref.dtype)

    if has_b:
        @pl.when(j >= n_a)
        def _():
            ob_ref[...] = acc().astype(ob_ref.dtype)


def _inproj(x, mods, w, cos, sin, *, C, n_rope_cols, n_a_cols, a_dtype, b_dtype):
    B, Lc, D = x.shape
    N = w.shape[1]
    tm = _tile(Lc, 1088)
    tn = 512
    assert N % tn == 0 and n_rope_cols % tn == 0 and n_a_cols % tn == 0
    n_a = n_a_cols // tn
    has_b = n_a_cols < N
    kern = functools.partial(_inproj_kernel, C=C, tm=tm, tn=tn, D=D, n_rope=n_rope_cols // tn, n_a=n_a, has_b=has_b)
    out_shape = [jax.ShapeDtypeStruct((B, Lc, n_a_cols), a_dtype)]
    out_specs = [pl.BlockSpec((None, tm, tn), lambda b, i, j: (b, i, jnp.minimum(j, n_a - 1)))]
    if has_b:
        out_shape.append(jax.ShapeDtypeStruct((B, Lc, N - n_a_cols), b_dtype))
        out_specs.append(pl.BlockSpec((None, tm, tn), lambda b, i, j: (b, i, jnp.maximum(j - n_a, 0))))
    nb = mods.shape[0] - 1
    return pl.pallas_call(
        kern,
        grid=(B, Lc // tm, N // tn),
        in_specs=[pl.BlockSpec((None, tm, D), lambda b, i, j: (b, i, 0)),
                  pl.BlockSpec((None, 1, 6 * D), lambda b, i, j: (b, 0, 0)),
                  pl.BlockSpec((None, 1, 6 * D), lambda b, i, j: (nb, 0, 0)),
                  pl.BlockSpec((D, tn), lambda b, i, j: (0, j)),
                  pl.BlockSpec((tm, LANE), lambda b, i, j: (i, 0)),
                  pl.BlockSpec((tm, LANE), lambda b, i, j: (i, 0))],
        out_specs=out_specs,
        out_shape=out_shape,
        scratch_shapes=[pltpu.VMEM((tm, D), bf16)],
        compiler_params=_cp("parallel", "parallel", "arbitrary"),
        name="inproj",
    )(x, mods, mods, w, cos, sin)


def _softmax_pv(s, v):
    m = jnp.max(s, axis=1, keepdims=True)
    p = jnp.exp(s - m)
    l = jnp.sum(p, axis=1, keepdims=True)
    return jnp.dot(p.astype(bf16), v, preferred_element_type=f32) / l


def _qk(q, k):
    return lax.dot_general(q, k, (((1,), (1,)), ((), ())), preferred_element_type=f32)


def _diff_attn_kernel(q_ref, k_ref, v_ref, lam_ref, sub_ref, o_ref, *, C, tq, lam_init):
    qi = pl.program_id(2)
    lv = lam_ref[...]
    lam = (jnp.exp(jnp.sum(lv[0:1] * lv[1:2], axis=1, keepdims=True))
           - jnp.exp(jnp.sum(lv[2:3] * lv[3:4], axis=1, keepdims=True)) + lam_init)

    def run(nkeys):
        q = q_ref[...]
        k = k_ref[0:nkeys, :]
        v = v_ref[0:nkeys, :]
        hd = DA_HEAD_DIM
        o1 = _softmax_pv(_qk(q[:, 0:hd], k[:, 0:hd]), v)
        o2 = _softmax_pv(_qk(q[:, hd:2 * hd], k[:, hd:2 * hd]), v)
        o = o1 - lam * o2
        o = o * lax.rsqrt(jnp.mean(o * o, axis=1, keepdims=True) + 1e-6) * sub_ref[...]
        o_ref[...] = (o * (1.0 - lam_init)).astype(o_ref.dtype)

    @pl.when(qi * tq < C)
    def _():
        run(C)

    @pl.when(qi * tq >= C)
    def _():
        run(k_ref.shape[0])


def _diff_attention(qkv, da_lam, da_subln, *, C, lam_init):
    B, Lc, _ = qkv.shape
    H = DA_HEADS
    tq = _tile(C, 256)
    kern = functools.partial(_diff_attn_kernel, C=C, tq=tq, lam_init=lam_init)
    return pl.pallas_call(
        kern,
        grid=(B, H, Lc // tq),
        in_specs=[pl.BlockSpec((None, tq, LANE), lambda b, h, i: (b, i, h)),
                  pl.BlockSpec((None, Lc, LANE), lambda b, h, i: (b, 0, H + h)),
                  pl.BlockSpec((None, Lc, LANE), lambda b, h, i: (b, 0, 2 * H + h)),
                  pl.BlockSpec((4, DA_HEAD_DIM), lambda b, h, i: (0, 0)),
                  pl.BlockSpec((1, DA_V_DIM), lambda b, h, i: (0, 0))],
        out_specs=pl.BlockSpec((None, tq, LANE), lambda b, h, i: (b, i, h)),
        out_shape=jax.ShapeDtypeStruct((B, Lc, H * DA_V_DIM), bf16),
        compiler_params=_cp("parallel", "parallel", "arbitrary"),
        name="diff_attention",
    )(qkv, qkv, qkv, da_lam, da_subln.reshape(1, DA_V_DIM))


def _mla_attn_kernel(qn_ref, qr_ref, kn_ref, kr_ref, v_ref, o_ref, *, C, tq, scale):
    qi = pl.program_id(2)

    def run(nkeys):
        q = jnp.concatenate([qn_ref[...], qr_ref[...]], axis=1)
        k = jnp.concatenate([kn_ref[0:nkeys, :], kr_ref[0:nkeys, :]], axis=1)
        o_ref[...] = _softmax_pv(_qk(q, k) * scale, v_ref[0:nkeys, :]).astype(o_ref.dtype)

    @pl.when(qi * tq < C)
    def _():
        run(C)

    @pl.when(qi * tq >= C)
    def _():
        run(kn_ref.shape[0])


def _mla_attention(q, kv, kr, *, C):
    B, Lc, _ = q.shape
    H = MLA_HEADS
    tq = _tile(C, 256)
    kern = functools.partial(_mla_attn_kernel, C=C, tq=tq, scale=(MLA_NOPE + MLA_ROPE) ** -0.5)
    return pl.pallas_call(
        kern,
        grid=(B, H, Lc // tq),
        in_specs=[pl.BlockSpec((None, tq, LANE), lambda b, h, i: (b, i, h)),
                  pl.BlockSpec((None, tq, LANE), lambda b, h, i: (b, i, H + h)),
                  pl.BlockSpec((None, Lc, LANE), lambda b, h, i: (b, 0, h)),
                  pl.BlockSpec((None, Lc, LANE), lambda b, h, i: (b, 0, 0)),
                  pl.BlockSpec((None, Lc, LANE), lambda b, h, i: (b, 0, H + h))],
        out_specs=pl.BlockSpec((None, tq, LANE), lambda b, h, i: (b, i, h)),
        out_shape=jax.ShapeDtypeStruct((B, Lc, H * MLA_V), bf16),
        compiler_params=_cp("parallel", "parallel", "arbitrary"),
        name="mla_attention",
    )(q, q, kv, kr, kv)


def _rms(x, w, eps=1e-6):
    return x * lax.rsqrt(jnp.mean(x * x, axis=1, keepdims=True) + eps) * w


def _mla_up_kernel(cq_ref, ckv_ref, kr_ref, qn_ref, kvn_ref, wq_ref, wkv_ref, cos_ref, sin_ref,
                   q_ref, kv_ref, kro_ref):
    nq = MLA_HEADS * MLA_NOPE
    cos = cos_ref[...]
    sin = sin_ref[...]
    q = jnp.dot(_rms(cq_ref[...], qn_ref[...]).astype(bf16), wq_ref[...], preferred_element_type=f32)
    q_ref[:, 0:nq] = q[:, 0:nq].astype(q_ref.dtype)
    for h in range(MLA_HEADS):
        c0 = nq + h * LANE
        q_ref[:, c0:c0 + LANE] = _rope(q[:, c0:c0 + LANE], cos, sin).astype(q_ref.dtype)
    kv = jnp.dot(_rms(ckv_ref[...], kvn_ref[...]).astype(bf16), wkv_ref[...], preferred_element_type=f32)
    kv_ref[...] = kv.astype(kv_ref.dtype)
    kro_ref[...] = _rope(kr_ref[...], cos, sin).astype(kro_ref.dtype)


def _mla_up(p, q_norm, kv_norm, wq, wkv, cos, sin):
    B, Lc, _ = p.shape
    tm = _tile(Lc, 544)
    nq = wq.shape[1]
    nkv = wkv.shape[1]
    return pl.pallas_call(
        _mla_up_kernel,
        grid=(B, Lc // tm),
        in_specs=[pl.BlockSpec((None, tm, MLA_Q_RANK), lambda b, i: (b, i, O_CQ // MLA_Q_RANK)),
                  pl.BlockSpec((None, tm, MLA_KV_RANK), lambda b, i: (b, i, O_CKV // MLA_KV_RANK)),
                  pl.BlockSpec((None, tm, LANE), lambda b, i: (b, i, O_KR // LANE)),
                  pl.BlockSpec((1, MLA_Q_RANK), lambda b, i: (0, 0)),
                  pl.BlockSpec((1, MLA_KV_RANK), lambda b, i: (0, 0)),
                  pl.BlockSpec((MLA_Q_RANK, nq), lambda b, i: (0, 0)),
                  pl.BlockSpec((MLA_KV_RANK, nkv), lambda b, i: (0, 0)),
                  pl.BlockSpec((tm, LANE), lambda b, i: (i, 0)),
                  pl.BlockSpec((tm, LANE), lambda b, i: (i, 0))],
        out_specs=[pl.BlockSpec((None, tm, nq), lambda b, i: (b, i, 0)),
                   pl.BlockSpec((None, tm, nkv), lambda b, i: (b, i, 0)),
                   pl.BlockSpec((None, tm, LANE), lambda b, i: (b, i, 0))],
        out_shape=[jax.ShapeDtypeStruct((B, Lc, nq), bf16),
                   jax.ShapeDtypeStruct((B, Lc, nkv), bf16),
                   jax.ShapeDtypeStruct((B, Lc, LANE), bf16)],
        compiler_params=_cp("parallel", "parallel"),
        name="mla_up",
    )(p, p, p, q_norm.reshape(1, -1), kv_norm.reshape(1, -1), wq, wkv, cos, sin)


def _s5_finish_kernel(yf_ref, yb_ref, u_ref, d_ref, w_ref, b_ref, o_ref):
    y = yf_ref[...] + yb_ref[...] + d_ref[...] * u_ref[...]
    y = 0.5 * y * (1.0 + jnp.tanh(math.sqrt(2.0 / math.pi) * (y + 0.044715 * (y * y * y))))
    z = jnp.dot(y.astype(bf16), w_ref[...], preferred_element_type=f32) + b_ref[...]
    o_ref[...] = (y * _sigmoid(z)).astype(o_ref.dtype)


def _s5_finish(yf, yb, u, d_skip, glu_w, glu_b):
    B, Lc, N = u.shape
    tm = _tile(Lc, 1088)
    row = lambda b, i: (b, i, 0)
    const = lambda b, i: (0, 0)
    return pl.pallas_call(
        _s5_finish_kernel,
        grid=(B, Lc // tm),
        in_specs=[pl.BlockSpec((None, tm, N), row), pl.BlockSpec((None, tm, N), row), pl.BlockSpec((None, tm, N), row),
                  pl.BlockSpec((1, N), const), pl.BlockSpec((N, N), const), pl.BlockSpec((1, N), const)],
        out_specs=pl.BlockSpec((None, tm, N), row),
        out_shape=jax.ShapeDtypeStruct((B, Lc, N), bf16),
        compiler_params=_cp("parallel", "parallel"),
        name="s5_finish",
    )(yf, yb, u, d_skip.reshape(1, N), glu_w.astype(bf16), glu_b.reshape(1, N))


def _gdn_finish_kernel(of_ref, ob_ref, z_ref, w_ref, o_ref):
    o = of_ref[...] + ob_ref[...]
    z = z_ref[...]
    w = w_ref[...]
    for h in range(GDN_HEADS):
        sl = slice(h * GDN_DV, (h + 1) * GDN_DV)
        zz = z[:, sl]
        o_ref[:, sl] = (_rms(o[:, sl], w) * (zz * _sigmoid(zz))).astype(o_ref.dtype)


def _gdn_finish(of, ob, p, norm_w):
    B, Lc, N = of.shape
    tm = _tile(Lc, 1088)
    row = lambda b, i: (b, i, 0)
    return pl.pallas_call(
        _gdn_finish_kernel,
        grid=(B, Lc // tm),
        in_specs=[pl.BlockSpec((None, tm, N), row), pl.BlockSpec((None, tm, N), row),
                  pl.BlockSpec((None, tm, N), lambda b, i: (b, i, O_Z // N)),
                  pl.BlockSpec((1, GDN_DV), lambda b, i: (0, 0))],
        out_specs=pl.BlockSpec((None, tm, N), row),
        out_shape=jax.ShapeDtypeStruct((B, Lc, N), bf16),
        compiler_params=_cp("parallel", "parallel"),
        name="gdn_finish",
    )(of, ob, p, norm_w.reshape(1, GDN_DV))


def _layer_norm(y, g, b, eps=1e-5):
    mu = jnp.mean(y, axis=1, keepdims=True)
    d = y - mu
    var = jnp.mean(d * d, axis=1, keepdims=True)
    return d * lax.rsqrt(var + eps) * g + b


def _merge_kernel(x_ref, a1_ref, a2_ref, w1_ref, w2_ref, ml_ref, mc_ref, g_ref, b_ref, rh_ref, rl_ref,
                  xo_ref, u_ref, lg_ref, *, C, tm, D):
    i = pl.program_id(1)
    o = (jnp.dot(a1_ref[...], w1_ref[...], preferred_element_type=f32)
         + jnp.dot(a2_ref[...], w2_ref[...], preferred_element_type=f32))
    gate, shift, scale = _select_mod(ml_ref[...], mc_ref[...], i * tm, tm, C, D, (2, 3, 4))
    xn = _layer_norm(DEEPNORM_ALPHA * x_ref[...] + gate * o, g_ref[...], b_ref[...])
    xo_ref[...] = xn
    u = xn * (1.0 + scale) + shift
    uh = u.astype(bf16)
    u_ref[...] = uh
    ul = (u - uh.astype(f32)).astype(bf16)
    rh = rh_ref[...]
    lg_ref[...] = (jnp.dot(uh, rh, preferred_element_type=f32) + jnp.dot(ul, rh, preferred_element_type=f32)
                   + jnp.dot(uh, rl_ref[...], preferred_element_type=f32))


def _merge(x, a1, a2, w1, w2, mods, ln_g, ln_b, r_hi, r_lo, *, C):
    B, Lc, D = x.shape
    K1, K2 = a1.shape[2], a2.shape[2]
    tm = _tile(Lc, 272)
    nb = mods.shape[0] - 1
    row = lambda b, i: (b, i, 0)
    const = lambda b, i: (0, 0)
    kern = functools.partial(_merge_kernel, C=C, tm=tm, D=D)
    return pl.pallas_call(
        kern,
        grid=(B, Lc // tm),
        in_specs=[pl.BlockSpec((None, tm, D), row),
                  pl.BlockSpec((None, tm, K1), row),
                  pl.BlockSpec((None, tm, K2), row),
                  pl.BlockSpec((K1, D), const),
                  pl.BlockSpec((K2, D), const),
                  pl.BlockSpec((None, 1, 6 * D), lambda b, i: (b, 0, 0)),
                  pl.BlockSpec((None, 1, 6 * D), lambda b, i: (nb, 0, 0)),
                  pl.BlockSpec((1, D), const),
                  pl.BlockSpec((1, D), const),
                  pl.BlockSpec((D, LANE), const),
                  pl.BlockSpec((D, LANE), const)],
        out_specs=[pl.BlockSpec((None, tm, D), row),
                   pl.BlockSpec((None, tm, D), row),
                   pl.BlockSpec((None, tm, LANE), row)],
        out_shape=[jax.ShapeDtypeStruct((B, Lc, D), f32),
                   jax.ShapeDtypeStruct((B, Lc, D), bf16),
                   jax.ShapeDtypeStruct((B, Lc, LANE), f32)],
        compiler_params=_cp("parallel", "parallel"),
        name="merge",
    )(x, a1, a2, w1, w2, mods, mods, ln_g.reshape(1, D), ln_b.reshape(1, D), r_hi, r_lo)


def _router_kernel(lg_ref, bias_ref, o_ref):
    lg = lg_ref[...]
    shape = lg.shape
    lane = lax.broadcasted_iota(jnp.int32, shape, 1)
    valid = lane < N_EXPERTS
    neg = -jnp.inf
    scores = _sigmoid(lg)
    biased = jnp.where(valid, scores + bias_ref[...], neg)

    def first_argmax(v):
        m = jnp.max(v, axis=1, keepdims=True)
        idx = jnp.min(jnp.where(v == m, lane, LANE), axis=1, keepdims=True)
        return m, idx

    best_score = None
    best_group = None
    for g in range(N_GROUPS):
        in_g = jnp.logical_and(lane >= g * EXPERTS_PER_GROUP, lane < (g + 1) * EXPERTS_PER_GROUP)
        vals = jnp.where(in_g, biased, neg)
        m1, i1 = first_argmax(vals)
        m2, _ = first_argmax(jnp.where(lane == i1, neg, vals))
        gs = m1 + m2
        if g == 0:
            best_score, best_group = gs, jnp.zeros_like(i1)
        else:
            better = gs > best_score
            best_score = jnp.where(better, gs, best_score)
            best_group = jnp.where(better, g, best_group)
    in_best = jnp.logical_and(lane >= best_group * EXPERTS_PER_GROUP, lane < (best_group + 1) * EXPERTS_PER_GROUP)
    vals = jnp.where(in_best, biased, neg)
    _, e0 = first_argmax(vals)
    _, e1 = first_argmax(jnp.where(lane == e0, neg, vals))
    w0 = jnp.sum(jnp.where(lane == e0, scores, 0.0), axis=1, keepdims=True)
    w1 = jnp.sum(jnp.where(lane == e1, scores, 0.0), axis=1, keepdims=True)
    tot = w0 + w1
    out = jnp.where(lane == 0, e0.astype(f32), jnp.where(lane == 1, e1.astype(f32),
                    jnp.where(lane == 2, w0 / tot, jnp.where(lane == 3, w1 / tot, 0.0))))
    o_ref[...] = out


def _router(logits, bias_row):
    T = logits.shape[0]
    tm = _tile(T, 1088, 8)
    return pl.pallas_call(
        _router_kernel,
        grid=(T // tm,),
        in_specs=[pl.BlockSpec((tm, LANE), lambda i: (i, 0)), pl.BlockSpec((1, LANE), lambda i: (0, 0))],
        out_specs=pl.BlockSpec((tm, LANE), lambda i: (i, 0)),
        out_shape=jax.ShapeDtypeStruct((T, LANE), f32),
        compiler_params=_cp("parallel"),
        name="router",
    )(logits, bias_row)


def _moe_kernel(te_ref, na_ref, x_ref, wg_ref, wu_ref, wd_ref, y_ref):
    i = pl.program_id(0)

    @pl.when(i < na_ref[0])
    def _():
        x = x_ref[...]
        g = jnp.dot(x, wg_ref[...], preferred_element_type=f32)
        u = jnp.dot(x, wu_ref[...], preferred_element_type=f32)
        h = (g * _sigmoid(g) * u).astype(bf16)
        y_ref[...] = jnp.dot(h, wd_ref[...], preferred_element_type=f32)

    @pl.when(i >= na_ref[0])
    def _():
        y_ref[...] = jnp.zeros_like(y_ref)


def _moe_experts(xs, tile_expert, n_active, wg, wu, wd):
    S, D = xs.shape
    F = wg.shape[2]
    tm = MOE_TM
    grid_spec = pltpu.PrefetchScalarGridSpec(
        num_scalar_prefetch=2,
        grid=(S // tm,),
        in_specs=[pl.BlockSpec((tm, D), lambda i, te, na: (i, 0)),
                  pl.BlockSpec((None, D, F), lambda i, te, na: (te[i], 0, 0)),
                  pl.BlockSpec((None, D, F), lambda i, te, na: (te[i], 0, 0)),
                  pl.BlockSpec((None, F, D), lambda i, te, na: (te[i], 0, 0))],
        out_specs=pl.BlockSpec((tm, D), lambda i, te, na: (i, 0)),
    )
    return pl.pallas_call(
        _moe_kernel,
        grid_spec=grid_spec,
        out_shape=jax.ShapeDtypeStruct((S, D), f32),
        compiler_params=_cp("arbitrary"),
        name="moe_experts",
    )(tile_expert, n_active, xs, wg, wu, wd)


def _final_ln_kernel(x_ref, f_ref, ml_ref, mc_ref, g_ref, b_ref, o_ref, *, C, tm, D):
    i = pl.program_id(1)
    (gate,) = _select_mod(ml_ref[...], mc_ref[...], i * tm, tm, C, D, (5,))
    o_ref[...] = _layer_norm(DEEPNORM_ALPHA * x_ref[...] + gate * f_ref[...], g_ref[...], b_ref[...])


def _final_ln(x, f, mods, ln_g, ln_b, *, C):
    B, Lc, D = x.shape
    tm = _tile(Lc, 544)
    nb = mods.shape[0] - 1
    row = lambda b, i: (b, i, 0)
    kern = functools.partial(_final_ln_kernel, C=C, tm=tm, D=D)
    return pl.pallas_call(
        kern,
        grid=(B, Lc // tm),
        in_specs=[pl.BlockSpec((None, tm, D), row), pl.BlockSpec((None, tm, D), row),
                  pl.BlockSpec((None, 1, 6 * D), lambda b, i: (b, 0, 0)),
                  pl.BlockSpec((None, 1, 6 * D), lambda b, i: (nb, 0, 0)),
                  pl.BlockSpec((1, D), lambda b, i: (0, 0)), pl.BlockSpec((1, D), lambda b, i: (0, 0))],
        out_specs=pl.BlockSpec((None, tm, D), row),
        out_shape=jax.ShapeDtypeStruct((B, Lc, D), f32),
        compiler_params=_cp("parallel", "parallel"),
        name="final_ln",
    )(x, f, mods, mods, ln_g.reshape(1, D), ln_b.reshape(1, D))


def _seq_reverse(t, C):
    return jnp.concatenate([jnp.flip(t[:, :C], axis=1), jnp.flip(t[:, C:], axis=1)], axis=1)


def _s5_tables(lam_re, lam_im, b_re, b_im, c_re, c_im, log_dt):
    T1 = S5_BLOCK
    hp = lax.Precision.HIGHEST
    lam = lax.complex(lam_re.astype(f32), lam_im.astype(f32))
    lam_dt = lam * jnp.exp(log_dt.astype(f32))[:, None]
    lam_bar = jnp.exp(lam_dt)
    b_bar = ((lam_bar - 1.0) / lam)[..., None] * lax.complex(b_re.astype(f32), b_im.astype(f32))
    c = lax.complex(c_re.astype(f32), c_im.astype(f32))
    pw = jnp.exp(lam_dt[None] * jnp.arange(T1 + 1, dtype=f32)[:, None, None])
    kern = jnp.einsum('gip,tgp,gpj->tgij', c, pw[:T1], b_bar, precision=hp).real
    t_idx = jnp.arange(T1)
    lag = t_idx[None, :] - t_idx[:, None]
    toep = jnp.where((lag >= 0)[None, :, None, :, None],
                     jnp.transpose(kern[jnp.clip(lag, 0, T1 - 1)], (2, 0, 4, 1, 3)), 0.0)
    G = lam.shape[0]
    toep = toep.reshape(G, T1 * S5_GROUP, T1 * S5_GROUP)
    bc = pw[T1 - 1 - t_idx][:, :, :, None] * b_bar[None]
    bc = jnp.transpose(bc, (1, 0, 3, 2)).reshape(G, T1 * S5_GROUP, S5_STATE)
    cp = c[None] * pw[1:T1 + 1][:, :, None, :]
    cc = jnp.transpose(cp, (1, 3, 0, 2)).reshape(G, S5_STATE, T1 * S5_GROUP)
    a = pw[T1]
    return toep, bc.real, bc.imag, cc.real, -cc.imag, a.real, a.imag


def _s5_direction(u, tables):
    toep, bcr, bci, ccr, cci, ar, ai = tables
    B, Lc, _ = u.shape
    T1, G = S5_BLOCK, S5_GROUPS
    nb = Lc // T1
    ug = u.reshape(B, nb, T1, G, S5_GROUP).transpose(3, 0, 1, 2, 4).reshape(G, B * nb, T1 * S5_GROUP)
    y_intra = jnp.einsum('grk,gko->gro', ug, toep)
    gr = jnp.einsum('grk,gkp->grp', ug, bcr).reshape(G, B, nb, S5_STATE)
    gi = jnp.einsum('grk,gkp->grp', ug, bci).reshape(G, B, nb, S5_STATE)

    def step(h, xs):
        hr, hi = h
        xr, xi = xs
        nr = ar[:, None] * hr - ai[:, None] * hi + xr
        ni = ar[:, None] * hi + ai[:, None] * hr + xi
        return (nr, ni), (hr, hi)

    z = jnp.zeros((G, B, S5_STATE), f32)
    _, (hpr, hpi) = lax.scan(step, (z, z), (jnp.moveaxis(gr, 2, 0), jnp.moveaxis(gi, 2, 0)))
    hpr = jnp.moveaxis(hpr, 0, 2).reshape(G, B * nb, S5_STATE)
    hpi = jnp.moveaxis(hpi, 0, 2).reshape(G, B * nb, S5_STATE)
    y = y_intra + jnp.einsum('grp,gpo->gro', hpr, ccr) + jnp.einsum('grp,gpo->gro', hpi, cci)
    return y.reshape(G, B, nb, T1, S5_GROUP).transpose(1, 2, 3, 0, 4).reshape(B, Lc, S5_CHANNELS)


def _s5_mixer(s, C, lam_re, lam_im, b_re, b_im, c_re, c_im, log_dt):
    yf = _s5_direction(s, _s5_tables(lam_re[0], lam_im[0], b_re[0], b_im[0], c_re[0], c_im[0], log_dt[0]))
    yb = _s5_direction(_seq_reverse(s, C), _s5_tables(lam_re[1], lam_im[1], b_re[1], b_im[1], c_re[1], c_im[1], log_dt[1]))
    return yf, _seq_reverse(yb, C)


def _gdn_scan(q, k, v, g, beta):
    B, L, H, _ = q.shape
    Cn = GDN_CHUNK
    N = L // Cn
    chunks = lambda t: t.reshape(B, N, Cn, H, -1).transpose(1, 0, 3, 2, 4)
    qc, kc, vc = chunks(q), chunks(k), chunks(v)
    gc = g.reshape(B, N, Cn, H).transpose(1, 0, 3, 2)
    bc = beta.reshape(B, N, Cn, H).transpose(1, 0, 3, 2)
    gcum = jnp.cumsum(gc, axis=-1)
    idx = jnp.arange(Cn)
    incl = idx[:, None] >= idx[None, :]
    strict = idx[:, None] > idx[None, :]
    decay = jnp.exp(jnp.where(incl, gcum[..., :, None] - gcum[..., None, :], -jnp.inf))
    kb = kc * bc[..., None]
    m = jnp.where(strict, jnp.einsum('nbhid,nbhjd->nbhij', kb, kc) * decay, 0.0)
    tri = m + jnp.eye(Cn, dtype=f32)
    solve = functools.partial(lax.linalg.triangular_solve, left_side=True, lower=True, unit_diagonal=True)
    u = solve(tri, vc * bc[..., None])
    w = solve(tri, kb * jnp.exp(gcum)[..., None])
    a_intra = jnp.where(incl, jnp.einsum('nbhid,nbhjd->nbhij', qc, kc) * decay, 0.0)
    qg = qc * jnp.exp(gcum)[..., None]
    g_last = gcum[..., -1]
    kd = kc * jnp.exp(g_last[..., None] - gcum)[..., None]

    def step(S, xs):
        qg_i, kd_i, u_i, w_i, a_i, gl_i = xs
        v_new = u_i - jnp.einsum('bhck,bhkv->bhcv', w_i, S)
        o = jnp.einsum('bhck,bhkv->bhcv', qg_i, S) + jnp.einsum('bhcj,bhjv->bhcv', a_i, v_new)
        S = S * jnp.exp(gl_i)[..., None, None] + jnp.einsum('bhck,bhcv->bhkv', kd_i, v_new)
        return S, o

    s0 = jnp.zeros((B, H, q.shape[-1], v.shape[-1]), f32)
    _, o = lax.scan(step, s0, (qg, kd, u, w, a_intra, g_last))
    return o.transpose(1, 0, 3, 2, 4).reshape(B, L, H * v.shape[-1])


def _gdn_mixer(p, C, conv_w, a_log, dt_bias):
    B, Lc, _ = p.shape
    raw = p[..., O_QKV:O_QKV + GDN_QKV]
    row = jnp.arange(Lc)[None, :, None]
    prev = jnp.where((row == 0) | (row == C), 0.0, jnp.roll(raw, 1, axis=1))
    nxt = jnp.where((row == C - 1) | (row == Lc - 1), 0.0, jnp.roll(raw, -1, axis=1))
    qkv = prev * conv_w[0] + raw * conv_w[1] + nxt * conv_w[2]
    qkv = qkv * _sigmoid(qkv)
    nk = GDN_HEADS * GDN_DK
    l2 = lambda x: x * lax.rsqrt(jnp.sum(jnp.square(x), axis=-1, keepdims=True) + 1e-6)
    q = l2(qkv[..., :nk].reshape(B, Lc, GDN_HEADS, GDN_DK)) * (GDN_DK ** -0.5)
    k = l2(qkv[..., nk:2 * nk].reshape(B, Lc, GDN_HEADS, GDN_DK))
    v = qkv[..., 2 * nk:].reshape(B, Lc, GDN_HEADS, GDN_DV)
    gates = p[..., O_GATE:O_GATE + 4 * GDN_HEADS].reshape(B, Lc, 4, GDN_HEADS)
    outs = []
    for d in range(2):
        g = -jnp.exp(a_log[d].astype(f32)) * jax.nn.softplus(gates[:, :, d] + dt_bias[d].astype(f32))
        beta = _sigmoid(gates[:, :, 2 + d])
        if d == 0:
            outs.append(_gdn_scan(q, k, v, g, beta))
        else:
            r = lambda t: _seq_reverse(t, C)
            outs.append(r(_gdn_scan(r(q), r(k), r(v), r(g), r(beta))))
    return outs


def _dispatch_plan(e_idx, n_tiles):
    T = e_idx.shape[0]
    tm = MOE_TM
    flat = e_idx.reshape(-1)
    onehot = (flat[:, None] == jnp.arange(N_EXPERTS)[None, :]).astype(jnp.int32)
    rank = jnp.sum((jnp.cumsum(onehot, axis=0) - onehot) * onehot, axis=1)
    counts = jnp.sum(onehot, axis=0)
    padded = ((counts + tm - 1) // tm) * tm
    ends = jnp.cumsum(padded)
    offs = ends - padded
    pos = offs[flat] + rank
    row_token = jnp.zeros((n_tiles * tm,), jnp.int32).at[pos].set(jnp.arange(2 * T, dtype=jnp.int32) // 2)
    tile_start = jnp.arange(n_tiles, dtype=jnp.int32) * tm
    tile_expert = jnp.minimum(jnp.sum((tile_start[:, None] >= ends[None, :]).astype(jnp.int32), axis=1), N_EXPERTS - 1)
    n_active = (ends[-1] // tm).astype(jnp.int32).reshape(1)
    last_e = tile_expert[jnp.maximum(n_active[0] - 1, 0)]
    tile_expert = jnp.where(tile_start < ends[-1], tile_expert, last_e).astype(jnp.int32)
    return row_token, pos.reshape(T, 2), tile_expert, n_active


def _moe(u, logits, bias_row, wg, wu, wd):
    B, Lc, D = u.shape
    T = B * Lc
    r = _router(logits.reshape(T, LANE), bias_row)
    e_idx = r[:, 0:2].astype(jnp.int32)
    wsel = r[:, 2:4]
    n_tiles = (2 * T + MOE_TM - 1) // MOE_TM + N_EXPERTS
    row_token, pos, tile_expert, n_active = _dispatch_plan(e_idx, n_tiles)
    xs = u.reshape(T, D)[row_token]
    ys = _moe_experts(xs, tile_expert, n_active, wg, wu, wd)
    f = wsel[:, 0:1] * ys[pos[:, 0]] + wsel[:, 1:2] * ys[pos[:, 1]]
    return f.reshape(B, Lc, D)


def _rope_tables(L, C):
    rows = L // GRID_W
    t_row = jnp.repeat(jnp.arange(rows), GRID_W).astype(f32)
    t_col = jnp.tile(jnp.arange(GRID_W), rows).astype(f32)
    n_freq = ROPE_DIM // 4
    inv_freq = ROPE_BASE ** (-jnp.arange(n_freq, dtype=f32) / n_freq)
    ang = jnp.concatenate([t_row[:, None] * inv_freq, t_col[:, None] * inv_freq], axis=-1)
    cos = jnp.repeat(jnp.cos(ang), 2, axis=1)
    sin = jnp.repeat(jnp.sin(ang), 2, axis=1) * jnp.tile(jnp.array([-1.0, 1.0], f32), ROPE_DIM // 2)
    one = jnp.ones((L, ROPE_DIM), f32)
    zero = jnp.zeros((L, ROPE_DIM), f32)
    ctx = lambda t, fill: jnp.concatenate([jnp.full((C, LANE), fill, f32), t], axis=0)
    return (ctx(jnp.concatenate([cos, cos], 1), 1.0), ctx(jnp.concatenate([sin, sin], 1), 0.0),
            ctx(jnp.concatenate([cos, one], 1), 1.0), ctx(jnp.concatenate([sin, zero], 1), 0.0))


def _prep_even_w_in(w):
    D = w.shape[0]
    nqk = 2 * DA_HEADS * DA_HEAD_DIM
    heads_first = lambda t: t.reshape(D, 2, DA_HEADS, DA_HEAD_DIM).transpose(0, 2, 1, 3).reshape(D, nqk)
    q = heads_first(w[:, :nqk]) * (DA_HEAD_DIM ** -0.5)
    k = heads_first(w[:, nqk:2 * nqk])
    return jnp.concatenate([q, k, w[:, 2 * nqk:]], axis=1).astype(bf16)


def _prep_odd_w_in(w):
    D = w.shape[0]
    z = lambda n: jnp.zeros((D, n), w.dtype)
    c0 = MLA_Q_RANK
    c1 = c0 + MLA_KV_RANK
    c2 = c1 + MLA_ROPE
    c3 = c2 + GDN_QKV
    c4 = c3 + GDN_HEADS * GDN_DV
    parts = [w[:, c2:c3], w[:, c3:c4], w[:, :c0], w[:, c0:c1], w[:, c1:c2], z(LANE - MLA_ROPE),
             w[:, c4:], z(LANE - 4 * GDN_HEADS)]
    out = jnp.concatenate(parts, axis=1)
    assert out.shape[1] == O_IN
    return out.astype(bf16)


def _prep_mla_w(w_uq, w_ukv):
    H = MLA_HEADS
    rq = w_uq.shape[0]
    wq = w_uq.reshape(rq, H, MLA_NOPE + MLA_ROPE)
    nope = wq[:, :, :MLA_NOPE].reshape(rq, H * MLA_NOPE)
    rope = jnp.concatenate([wq[:, :, MLA_NOPE:], jnp.zeros((rq, H, LANE - MLA_ROPE), w_uq.dtype)], axis=2)
    wq_p = jnp.concatenate([nope, rope.reshape(rq, H * LANE)], axis=1).astype(bf16)
    rkv = w_ukv.shape[0]
    wkv = w_ukv.reshape(rkv, H, MLA_NOPE + MLA_V)
    wkv_p = jnp.concatenate([wkv[:, :, :MLA_NOPE].reshape(rkv, H * MLA_NOPE),
                             wkv[:, :, MLA_NOPE:].reshape(rkv, H * MLA_V)], axis=1).astype(bf16)
    return wq_p, wkv_p


def kernel(x, c, ctx, c_ctx, mod_w, mod_b, ln_g, ln_b, e_w_in, e_w_out, da_lam, da_subln, s5_lam_re, s5_lam_im, s5_b_re, s5_b_im, s5_c_re, s5_c_im, s5_log_dt, s5_d, s5_glu_w, s5_glu_b, o_w_in, o_w_out, mla_q_norm, mla_kv_norm, mla_w_uq, mla_w_ukv, gdn_conv, gdn_a_log, gdn_dt_bias, gdn_norm, router_w, router_bias, moe_w_gate, moe_w_up, moe_w_down):
    B, L, D = x.shape
    C = ctx.shape[1]
    depth = mod_w.shape[0]
    assert depth == DEPTH

    R = -(-(B + 1) // 8) * 8
    cc = jnp.concatenate([c, c_ctx[None, :], jnp.zeros((R - B - 1, D), f32)], axis=0)
    mods_all = _modulation(cc, mod_w, mod_b)[:, :B + 1].reshape(depth, B + 1, 1, 6 * D)

    cos_e, sin_e, cos_o, sin_o = _rope_tables(L, C)
    r_pad = jnp.concatenate([router_w.astype(f32), jnp.zeros((D, LANE - N_EXPERTS), f32)], axis=1)
    r_hi = r_pad.astype(bf16)
    r_lo = (r_pad - r_hi.astype(f32)).astype(bf16)
    bias_row = jnp.concatenate([router_bias.astype(f32), jnp.zeros((LANE - N_EXPERTS,), f32)]).reshape(1, LANE)

    xs = jnp.concatenate([ctx, x], axis=1)
    n_att = DA_HEADS * DA_V_DIM
    for l in range(depth):
        mods = mods_all[l]
        i = l // 2
        if l % 2 == 0:
            qkv, s = _inproj(xs, mods, _prep_even_w_in(e_w_in[i]), cos_e, sin_e, C=C,
                             n_rope_cols=4 * DA_HEADS * DA_HEAD_DIM, n_a_cols=3072, a_dtype=bf16, b_dtype=f32)
            a1 = _diff_attention(qkv, da_lam[i].astype(f32), da_subln[i].astype(f32), C=C,
                                 lam_init=0.8 - 0.6 * math.exp(-0.3 * l))
            yf, yb = _s5_mixer(s, C, s5_lam_re[i], s5_lam_im[i], s5_b_re[i], s5_b_im[i], s5_c_re[i], s5_c_im[i],
                               s5_log_dt[i])
            a2 = _s5_finish(yf, yb, s, s5_d[i], s5_glu_w[i], s5_glu_b[i])
            w_out = e_w_out[i].astype(bf16)
        else:
            (p,) = _inproj(xs, mods, _prep_odd_w_in(o_w_in[i]), cos_e, sin_e, C=C,
                           n_rope_cols=0, n_a_cols=O_IN, a_dtype=f32, b_dtype=f32)
            wq_p, wkv_p = _prep_mla_w(mla_w_uq[i], mla_w_ukv[i])
            q, kv, kr = _mla_up(p, mla_q_norm[i], mla_kv_norm[i], wq_p, wkv_p, cos_o, sin_o)
            a1 = _mla_attention(q, kv, kr, C=C)
            of, ob = _gdn_mixer(p, C, gdn_conv[i].astype(f32), gdn_a_log[i], gdn_dt_bias[i])
            a2 = _gdn_finish(of, ob, p, gdn_norm[i])
            w_out = o_w_out[i].astype(bf16)
        xs, u, logits = _merge(xs, a1, a2, w_out[:n_att], w_out[n_att:], mods, ln_g[l, 0], ln_b[l, 0],
                               r_hi, r_lo, C=C)
        f = _moe(u, logits, bias_row, moe_w_gate[l].astype(bf16), moe_w_up[l].astype(bf16),
                 moe_w_down[l].astype(bf16))
        xs = _final_ln(xs, f, mods, ln_g[l, 1], ln_b[l, 1], C=C)
    return xs[:, C:, :]
```

```python
import functools
import math

import jax
import jax.numpy as jnp
import numpy as np
from jax import lax
from jax.experimental import pallas as pl
from jax.experimental.pallas import tpu as pltpu

f32 = jnp.float32
bf16 = jnp.bfloat16

GRID_W = 64
ROPE_DIM = 64
ROPE_BASE = 10000.0
DA_HEADS = 8
DA_HEAD_DIM = 64
DA_V_DIM = 128
S5_CHANNELS = 512
S5_GROUP = 16
S5_GROUPS = 32
S5_STATE = 64
S5_BLOCK = 16
MLA_HEADS = 8
MLA_Q_RANK = 512
MLA_KV_RANK = 256
MLA_NOPE = 128
MLA_ROPE = 64
MLA_V = 128
GDN_HEADS = 8
GDN_DK = 128
GDN_DV = 128
GDN_QKV = 3072
GDN_CHUNK = 64
N_EXPERTS = 16
N_GROUPS = 4
EXPERTS_PER_GROUP = 4
DEPTH = 4
DEEPNORM_ALPHA = (2 * DEPTH) ** 0.25
LANE = 128
MOE_TM = 256
VMEM_LIMIT = 56 * 1024 * 1024

O_QKV, O_Z, O_CQ, O_CKV, O_KR, O_GATE, O_IN = 0, 3072, 4096, 4608, 4864, 4992, 5120


def _cp(*sem):
    return pltpu.CompilerParams(dimension_semantics=sem, vmem_limit_bytes=VMEM_LIMIT)


def _tile(n, cap, mult=16):
    best = None
    for t in range(mult, min(n, cap) + 1, mult):
        if n % t == 0:
            best = t
    assert best is not None, (n, cap, mult)
    return best


def _sigmoid(x):
    return 1.0 / (1.0 + jnp.exp(-x))


def _mod_kernel(a_ref, w_ref, b_ref, o_ref):
    a = a_ref[...]
    act = (a * _sigmoid(a)).astype(bf16)
    o_ref[...] = jnp.dot(act, w_ref[...].astype(bf16), preferred_element_type=f32) + b_ref[...]


def _modulation(cc, mod_w, mod_b):
    depth, D, N = mod_w.shape
    R = cc.shape[0]
    tn = _tile(N, 1024, LANE)
    return pl.pallas_call(
        _mod_kernel,
        grid=(depth, N // tn),
        in_specs=[pl.BlockSpec((R, D), lambda l, j: (0, 0)),
                  pl.BlockSpec((None, D, tn), lambda l, j: (l, 0, j)),
                  pl.BlockSpec((None, 1, tn), lambda l, j: (l, 0, j))],
        out_specs=pl.BlockSpec((None, R, tn), lambda l, j: (l, 0, j)),
        out_shape=jax.ShapeDtypeStruct((depth, R, N), f32),
        compiler_params=_cp("parallel", "parallel"),
        name="modulation",
    )(cc, mod_w, mod_b.reshape(depth, 1, N))


def _select_mod(ml, mc, row0, tm, C, D, chunks):
    is_ctx = (row0 + lax.broadcasted_iota(jnp.int32, (tm, 1), 0)) < C
    return [jnp.where(is_ctx, mc[:, k * D:(k + 1) * D], ml[:, k * D:(k + 1) * D]) for k in chunks]


def _rope(seg, cos, sin):
    nxt = pltpu.roll(seg, LANE - 1, axis=1)
    prv = pltpu.roll(seg, 1, axis=1)
    even = (lax.broadcasted_iota(jnp.int32, seg.shape, 1) % 2) == 0
    return seg * cos + jnp.where(even, nxt, prv) * sin


def _inproj_kernel(x_ref, ml_ref, mc_ref, w_ref, cos_ref, sin_ref, *rest, C, tm, tn, D, n_rope, n_a, has_b):
    if has_b:
        oa_ref, ob_ref, u_scr = rest
    else:
        oa_ref, u_scr = rest
        ob_ref = None
    i = pl.program_id(1)
    j = pl.program_id(2)

    @pl.when(j == 0)
    def _():
        shift, scale = _select_mod(ml_ref[...], mc_ref[...], i * tm, tm, C, D, (0, 1))
        u_scr[...] = (x_ref[...] * (1.0 + scale) + shift).astype(bf16)

    def acc():
        return jnp.dot(u_scr[...], w_ref[...], preferred_element_type=f32)

    if n_rope > 0:
        @pl.when(j < n_rope)
        def _():
            a = acc()
            cos = cos_ref[...]
            sin = sin_ref[...]
            for c in range(tn // LANE):
                oa_ref[:, c * LANE:(c + 1) * LANE] = _rope(a[:, c * LANE:(c + 1) * LANE], cos, sin).astype(oa_ref.dtype)

    @pl.when(jnp.logical_and(j >= n_rope, j < n_a))
    def _():
        oa_ref[...] = acc().astype(oa_ref.dtype)

    if has_b:
        @pl.when(j >= n_a)
        def _():
            ob_ref[...] = acc().astype(ob_ref.dtype)


def _inproj(x, mods, w, cos, sin, *, C, n_rope_cols, n_a_cols, a_dtype, b_dtype):
    B, Lc, D = x.shape
    N = w.shape[1]
    tm = _tile(Lc, 1088)
    tn = 512
    assert N % tn == 0 and n_rope_cols % tn == 0 and n_a_cols % tn == 0
    n_a = n_a_cols // tn
    has_b = n_a_cols < N
    kern = functools.partial(_inproj_kernel, C=C, tm=tm, tn=tn, D=D, n_rope=n_rope_cols // tn, n_a=n_a, has_b=has_b)
    out_shape = [jax.ShapeDtypeStruct((B, Lc, n_a_cols), a_dtype)]
    out_specs = [pl.BlockSpec((None, tm, tn), lambda b, i, j: (b, i, jnp.minimum(j, n_a - 1)))]
    if has_b:
        out_shape.append(jax.ShapeDtypeStruct((B, Lc, N - n_a_cols), b_dtype))
        out_specs.append(pl.BlockSpec((None, tm, tn), lambda b, i, j: (b, i, jnp.maximum(j - n_a, 0))))
    nb = mods.shape[0] - 1
    return pl.pallas_call(
        kern,
        grid=(B, Lc // tm, N // tn),
        in_specs=[pl.BlockSpec((None, tm, D), lambda b, i, j: (b, i, 0)),
                  pl.BlockSpec((None, 1, 6 * D), lambda b, i, j: (b, 0, 0)),
                  pl.BlockSpec((None, 1, 6 * D), lambda b, i, j: (nb, 0, 0)),
                  pl.BlockSpec((D, tn), lambda b, i, j: (0, j)),
                  pl.BlockSpec((tm, LANE), lambda b, i, j: (i, 0)),
                  pl.BlockSpec((tm, LANE), lambda b, i, j: (i, 0))],
        out_specs=out_specs,
        out_shape=out_shape,
        scratch_shapes=[pltpu.VMEM((tm, D), bf16)],
        compiler_params=_cp("parallel", "parallel", "arbitrary"),
        name="inproj",
    )(x, mods, mods, w, cos, sin)


def _softmax_pv(s, v):
    m = jnp.max(s, axis=1, keepdims=True)
    p = jnp.exp(s - m)
    l = jnp.sum(p, axis=1, keepdims=True)
    return jnp.dot(p.astype(bf16), v, preferred_element_type=f32) / l


def _qk(q, k):
    return lax.dot_general(q, k, (((1,), (1,)), ((), ())), preferred_element_type=f32)


def _diff_attn_kernel(q_ref, k_ref, v_ref, lam_ref, sub_ref, o_ref, *, C, tq, lam_init):
    qi = pl.program_id(2)
    lv = lam_ref[...]
    lam = (jnp.exp(jnp.sum(lv[0:1] * lv[1:2], axis=1, keepdims=True))
           - jnp.exp(jnp.sum(lv[2:3] * lv[3:4], axis=1, keepdims=True)) + lam_init)

    def run(nkeys):
        q = q_ref[...]
        k = k_ref[0:nkeys, :]
        v = v_ref[0:nkeys, :]
        hd = DA_HEAD_DIM
        o1 = _softmax_pv(_qk(q[:, 0:hd], k[:, 0:hd]), v)
        o2 = _softmax_pv(_qk(q[:, hd:2 * hd], k[:, hd:2 * hd]), v)
        o = o1 - lam * o2
        o = o * lax.rsqrt(jnp.mean(o * o, axis=1, keepdims=True) + 1e-6) * sub_ref[...]
        o_ref[...] = (o * (1.0 - lam_init)).astype(o_ref.dtype)

    @pl.when(qi * tq < C)
    def _():
        run(C)

    @pl.when(qi * tq >= C)
    def _():
        run(k_ref.shape[0])


def _diff_attention(qkv, da_lam, da_subln, *, C, lam_init):
    B, Lc, _ = qkv.shape
    H = DA_HEADS
    tq = _tile(C, 256)
    kern = functools.partial(_diff_attn_kernel, C=C, tq=tq, lam_init=lam_init)
    return pl.pallas_call(
        kern,
        grid=(B, H, Lc // tq),
        in_specs=[pl.BlockSpec((None, tq, LANE), lambda b, h, i: (b, i, h)),
                  pl.BlockSpec((None, Lc, LANE), lambda b, h, i: (b, 0, H + h)),
                  pl.BlockSpec((None, Lc, LANE), lambda b, h, i: (b, 0, 2 * H + h)),
                  pl.BlockSpec((4, DA_HEAD_DIM), lambda b, h, i: (0, 0)),
                  pl.BlockSpec((1, DA_V_DIM), lambda b, h, i: (0, 0))],
        out_specs=pl.BlockSpec((None, tq, LANE), lambda b, h, i: (b, i, h)),
        out_shape=jax.ShapeDtypeStruct((B, Lc, H * DA_V_DIM), bf16),
        compiler_params=_cp("parallel", "parallel", "arbitrary"),
        name="diff_attention",
    )(qkv, qkv, qkv, da_lam, da_subln.reshape(1, DA_V_DIM))


def _mla_attn_kernel(qn_ref, qr_ref, kn_ref, kr_ref, v_ref, o_ref, *, C, tq, scale):
    qi = pl.program_id(2)

    def run(nkeys):
        q = jnp.concatenate([qn_ref[...], qr_ref[...]], axis=1)
        k = jnp.concatenate([kn_ref[0:nkeys, :], kr_ref[0:nkeys, :]], axis=1)
        o_ref[...] = _softmax_pv(_qk(q, k) * scale, v_ref[0:nkeys, :]).astype(o_ref.dtype)

    @pl.when(qi * tq < C)
    def _():
        run(C)

    @pl.when(qi * tq >= C)
    def _():
        run(kn_ref.shape[0])


def _mla_attention(q, kv, kr, *, C):
    B, Lc, _ = q.shape
    H = MLA_HEADS
    tq = _tile(C, 256)
    kern = functools.partial(_mla_attn_kernel, C=C, tq=tq, scale=(MLA_NOPE + MLA_ROPE) ** -0.5)
    return pl.pallas_call(
        kern,
        grid=(B, H, Lc // tq),
        in_specs=[pl.BlockSpec((None, tq, LANE), lambda b, h, i: (b, i, h)),
                  pl.BlockSpec((None, tq, LANE), lambda b, h, i: (b, i, H + h)),
                  pl.BlockSpec((None, Lc, LANE), lambda b, h, i: (b, 0, h)),
                  pl.BlockSpec((None, Lc, LANE), lambda b, h, i: (b, 0, 0)),
                  pl.BlockSpec((None, Lc, LANE), lambda b, h, i: (b, 0, H + h))],
        out_specs=pl.BlockSpec((None, tq, LANE), lambda b, h, i: (b, i, h)),
        out_shape=jax.ShapeDtypeStruct((B, Lc, H * MLA_V), bf16),
        compiler_params=_cp("parallel", "parallel", "arbitrary"),
        name="mla_attention",
    )(q, q, kv, kr, kv)


def _rms(x, w, eps=1e-6):
    return x * lax.rsqrt(jnp.mean(x * x, axis=1, keepdims=True) + eps) * w


def _mla_up_kernel(cq_ref, ckv_ref, kr_ref, qn_ref, kvn_ref, wq_ref, wkv_ref, cos_ref, sin_ref,
                   q_ref, kv_ref, kro_ref):
    nq = MLA_HEADS * MLA_NOPE
    cos = cos_ref[...]
    sin = sin_ref[...]
    q = jnp.dot(_rms(cq_ref[...], qn_ref[...]).astype(bf16), wq_ref[...], preferred_element_type=f32)
    q_ref[:, 0:nq] = q[:, 0:nq].astype(q_ref.dtype)
    for h in range(MLA_HEADS):
        c0 = nq + h * LANE
        q_ref[:, c0:c0 + LANE] = _rope(q[:, c0:c0 + LANE], cos, sin).astype(q_ref.dtype)
    kv = jnp.dot(_rms(ckv_ref[...], kvn_ref[...]).astype(bf16), wkv_ref[...], preferred_element_type=f32)
    kv_ref[...] = kv.astype(kv_ref.dtype)
    kro_ref[...] = _rope(kr_ref[...], cos, sin).astype(kro_ref.dtype)


def _mla_up(p, q_norm, kv_norm, wq, wkv, cos, sin):
    B, Lc, _ = p.shape
    tm = _tile(Lc, 544)
    nq = wq.shape[1]
    nkv = wkv.shape[1]
    return pl.pallas_call(
        _mla_up_kernel,
        grid=(B, Lc // tm),
        in_specs=[pl.BlockSpec((None, tm, MLA_Q_RANK), lambda b, i: (b, i, O_CQ // MLA_Q_RANK)),
                  pl.BlockSpec((None, tm, MLA_KV_RANK), lambda b, i: (b, i, O_CKV // MLA_KV_RANK)),
                  pl.BlockSpec((None, tm, LANE), lambda b, i: (b, i, O_KR // LANE)),
                  pl.BlockSpec((1, MLA_Q_RANK), lambda b, i: (0, 0)),
                  pl.BlockSpec((1, MLA_KV_RANK), lambda b, i: (0, 0)),
                  pl.BlockSpec((MLA_Q_RANK, nq), lambda b, i: (0, 0)),
                  pl.BlockSpec((MLA_KV_RANK, nkv), lambda b, i: (0, 0)),
                  pl.BlockSpec((tm, LANE), lambda b, i: (i, 0)),
                  pl.BlockSpec((tm, LANE), lambda b, i: (i, 0))],
        out_specs=[pl.BlockSpec((None, tm, nq), lambda b, i: (b, i, 0)),
                   pl.BlockSpec((None, tm, nkv), lambda b, i: (b, i, 0)),
                   pl.BlockSpec((None, tm, LANE), lambda b, i: (b, i, 0))],
        out_shape=[jax.ShapeDtypeStruct((B, Lc, nq), bf16),
                   jax.ShapeDtypeStruct((B, Lc, nkv), bf16),
                   jax.ShapeDtypeStruct((B, Lc, LANE), bf16)],
        compiler_params=_cp("parallel", "parallel"),
        name="mla_up",
    )(p, p, p, q_norm.reshape(1, -1), kv_norm.reshape(1, -1), wq, wkv, cos, sin)


def _s5_finish_kernel(y_ref, u_ref, d_ref, w_ref, b_ref, o_ref):
    y = y_ref[...] + d_ref[...] * u_ref[...]
    y = 0.5 * y * (1.0 + jnp.tanh(math.sqrt(2.0 / math.pi) * (y + 0.044715 * (y * y * y))))
    z = jnp.dot(y.astype(bf16), w_ref[...], preferred_element_type=f32) + b_ref[...]
    o_ref[...] = (y * _sigmoid(z)).astype(o_ref.dtype)


def _s5_finish(y, u, d_skip, glu_w, glu_b):
    B, Lc, N = u.shape
    tm = _tile(Lc, 1088)
    row = lambda b, i: (b, i, 0)
    const = lambda b, i: (0, 0)
    return pl.pallas_call(
        _s5_finish_kernel,
        grid=(B, Lc // tm),
        in_specs=[pl.BlockSpec((None, tm, N), row), pl.BlockSpec((None, tm, N), row),
                  pl.BlockSpec((1, N), const), pl.BlockSpec((N, N), const), pl.BlockSpec((1, N), const)],
        out_specs=pl.BlockSpec((None, tm, N), row),
        out_shape=jax.ShapeDtypeStruct((B, Lc, N), bf16),
        compiler_params=_cp("parallel", "parallel"),
        name="s5_finish",
    )(y, u, d_skip.reshape(1, N), glu_w.astype(bf16), glu_b.reshape(1, N))


def _gdn_finish_kernel(of_ref, ob_ref, z_ref, w_ref, o_ref):
    o = of_ref[...] + ob_ref[...]
    z = z_ref[...]
    w = w_ref[...]
    for h in range(GDN_HEADS):
        sl = slice(h * GDN_DV, (h + 1) * GDN_DV)
        zz = z[:, sl]
        o_ref[:, sl] = (_rms(o[:, sl], w) * (zz * _sigmoid(zz))).astype(o_ref.dtype)


def _gdn_finish(of, ob, p, norm_w):
    B, Lc, N = of.shape
    tm = _tile(Lc, 1088)
    row = lambda b, i: (b, i, 0)
    return pl.pallas_call(
        _gdn_finish_kernel,
        grid=(B, Lc // tm),
        in_specs=[pl.BlockSpec((None, tm, N), row), pl.BlockSpec((None, tm, N), row),
                  pl.BlockSpec((None, tm, N), lambda b, i: (b, i, O_Z // N)),
                  pl.BlockSpec((1, GDN_DV), lambda b, i: (0, 0))],
        out_specs=pl.BlockSpec((None, tm, N), row),
        out_shape=jax.ShapeDtypeStruct((B, Lc, N), bf16),
        compiler_params=_cp("parallel", "parallel"),
        name="gdn_finish",
    )(of, ob, p, norm_w.reshape(1, GDN_DV))


def _layer_norm(y, g, b, eps=1e-5):
    mu = jnp.mean(y, axis=1, keepdims=True)
    d = y - mu
    var = jnp.mean(d * d, axis=1, keepdims=True)
    return d * lax.rsqrt(var + eps) * g + b


def _merge_kernel(x_ref, a1_ref, a2_ref, w1_ref, w2_ref, ml_ref, mc_ref, g_ref, b_ref, rh_ref, rl_ref,
                  xo_ref, u_ref, lg_ref, *, C, tm, D):
    i = pl.program_id(1)
    o = (jnp.dot(a1_ref[...], w1_ref[...], preferred_element_type=f32)
         + jnp.dot(a2_ref[...], w2_ref[...], preferred_element_type=f32))
    gate, shift, scale = _select_mod(ml_ref[...], mc_ref[...], i * tm, tm, C, D, (2, 3, 4))
    xn = _layer_norm(DEEPNORM_ALPHA * x_ref[...] + gate * o, g_ref[...], b_ref[...])
    xo_ref[...] = xn
    u = xn * (1.0 + scale) + shift
    uh = u.astype(bf16)
    u_ref[...] = uh
    ul = (u - uh.astype(f32)).astype(bf16)
    rh = rh_ref[...]
    lg_ref[...] = (jnp.dot(uh, rh, preferred_element_type=f32) + jnp.dot(ul, rh, preferred_element_type=f32)
                   + jnp.dot(uh, rl_ref[...], preferred_element_type=f32))


def _merge(x, a1, a2, w1, w2, mods, ln_g, ln_b, r_hi, r_lo, *, C):
    B, Lc, D = x.shape
    K1, K2 = a1.shape[2], a2.shape[2]
    tm = _tile(Lc, 272)
    nb = mods.shape[0] - 1
    row = lambda b, i: (b, i, 0)
    const = lambda b, i: (0, 0)
    kern = functools.partial(_merge_kernel, C=C, tm=tm, D=D)
    return pl.pallas_call(
        kern,
        grid=(B, Lc // tm),
        in_specs=[pl.BlockSpec((None, tm, D), row),
                  pl.BlockSpec((None, tm, K1), row),
                  pl.BlockSpec((None, tm, K2), row),
                  pl.BlockSpec((K1, D), const),
                  pl.BlockSpec((K2, D), const),
                  pl.BlockSpec((None, 1, 6 * D), lambda b, i: (b, 0, 0)),
                  pl.BlockSpec((None, 1, 6 * D), lambda b, i: (nb, 0, 0)),
                  pl.BlockSpec((1, D), const),
                  pl.BlockSpec((1, D), const),
                  pl.BlockSpec((D, LANE), const),
                  pl.BlockSpec((D, LANE), const)],
        out_specs=[pl.BlockSpec((None, tm, D), row),
                   pl.BlockSpec((None, tm, D), row),
                   pl.BlockSpec((None, tm, LANE), row)],
        out_shape=[jax.ShapeDtypeStruct((B, Lc, D), f32),
                   jax.ShapeDtypeStruct((B, Lc, D), bf16),
                   jax.ShapeDtypeStruct((B, Lc, LANE), f32)],
        compiler_params=_cp("parallel", "parallel"),
        name="merge",
    )(x, a1, a2, w1, w2, mods, mods, ln_g.reshape(1, D), ln_b.reshape(1, D), r_hi, r_lo)


def _router_kernel(lg_ref, bias_ref, o_ref):
    lg = lg_ref[...]
    shape = lg.shape
    lane = lax.broadcasted_iota(jnp.int32, shape, 1)
    valid = lane < N_EXPERTS
    neg = -jnp.inf
    scores = _sigmoid(lg)
    biased = jnp.where(valid, scores + bias_ref[...], neg)

    def first_argmax(v):
        m = jnp.max(v, axis=1, keepdims=True)
        idx = jnp.min(jnp.where(v == m, lane, LANE), axis=1, keepdims=True)
        return m, idx

    best_score = None
    best_group = None
    for g in range(N_GROUPS):
        in_g = jnp.logical_and(lane >= g * EXPERTS_PER_GROUP, lane < (g + 1) * EXPERTS_PER_GROUP)
        vals = jnp.where(in_g, biased, neg)
        m1, i1 = first_argmax(vals)
        m2, _ = first_argmax(jnp.where(lane == i1, neg, vals))
        gs = m1 + m2
        if g == 0:
            best_score, best_group = gs, jnp.zeros_like(i1)
        else:
            better = gs > best_score
            best_score = jnp.where(better, gs, best_score)
            best_group = jnp.where(better, g, best_group)
    in_best = jnp.logical_and(lane >= best_group * EXPERTS_PER_GROUP, lane < (best_group + 1) * EXPERTS_PER_GROUP)
    vals = jnp.where(in_best, biased, neg)
    _, e0 = first_argmax(vals)
    _, e1 = first_argmax(jnp.where(lane == e0, neg, vals))
    w0 = jnp.sum(jnp.where(lane == e0, scores, 0.0), axis=1, keepdims=True)
    w1 = jnp.sum(jnp.where(lane == e1, scores, 0.0), axis=1, keepdims=True)
    tot = w0 + w1
    out = jnp.where(lane == 0, e0.astype(f32), jnp.where(lane == 1, e1.astype(f32),
                    jnp.where(lane == 2, w0 / tot, jnp.where(lane == 3, w1 / tot, 0.0))))
    o_ref[...] = out


def _router(logits, bias_row):
    T = logits.shape[0]
    tm = _tile(T, 1088, 8)
    return pl.pallas_call(
        _router_kernel,
        grid=(T // tm,),
        in_specs=[pl.BlockSpec((tm, LANE), lambda i: (i, 0)), pl.BlockSpec((1, LANE), lambda i: (0, 0))],
        out_specs=pl.BlockSpec((tm, LANE), lambda i: (i, 0)),
        out_shape=jax.ShapeDtypeStruct((T, LANE), f32),
        compiler_params=_cp("parallel"),
        name="router",
    )(logits, bias_row)


def _moe_kernel(te_ref, na_ref, x_ref, wg_ref, wu_ref, wd_ref, y_ref):
    i = pl.program_id(0)

    @pl.when(i < na_ref[0])
    def _():
        x = x_ref[...]
        g = jnp.dot(x, wg_ref[...], preferred_element_type=f32)
        u = jnp.dot(x, wu_ref[...], preferred_element_type=f32)
        h = (g * _sigmoid(g) * u).astype(bf16)
        y_ref[...] = jnp.dot(h, wd_ref[...], preferred_element_type=f32)

    @pl.when(i >= na_ref[0])
    def _():
        y_ref[...] = jnp.zeros_like(y_ref)


def _moe_experts(xs, tile_expert, n_active, wg, wu, wd):
    S, D = xs.shape
    F = wg.shape[2]
    tm = MOE_TM
    grid_spec = pltpu.PrefetchScalarGridSpec(
        num_scalar_prefetch=2,
        grid=(S // tm,),
        in_specs=[pl.BlockSpec((tm, D), lambda i, te, na: (i, 0)),
                  pl.BlockSpec((None, D, F), lambda i, te, na: (te[i], 0, 0)),
                  pl.BlockSpec((None, D, F), lambda i, te, na: (te[i], 0, 0)),
                  pl.BlockSpec((None, F, D), lambda i, te, na: (te[i], 0, 0))],
        out_specs=pl.BlockSpec((tm, D), lambda i, te, na: (i, 0)),
    )
    return pl.pallas_call(
        _moe_kernel,
        grid_spec=grid_spec,
        out_shape=jax.ShapeDtypeStruct((S, D), f32),
        compiler_params=_cp("arbitrary"),
        name="moe_experts",
    )(tile_expert, n_active, xs, wg, wu, wd)


def _final_ln_kernel(x_ref, f_ref, ml_ref, mc_ref, g_ref, b_ref, o_ref, *, C, tm, D):
    i = pl.program_id(1)
    (gate,) = _select_mod(ml_ref[...], mc_ref[...], i * tm, tm, C, D, (5,))
    o_ref[...] = _layer_norm(DEEPNORM_ALPHA * x_ref[...] + gate * f_ref[...], g_ref[...], b_ref[...])


def _final_ln(x, f, mods, ln_g, ln_b, *, C):
    B, Lc, D = x.shape
    tm = _tile(Lc, 544)
    nb = mods.shape[0] - 1
    row = lambda b, i: (b, i, 0)
    kern = functools.partial(_final_ln_kernel, C=C, tm=tm, D=D)
    return pl.pallas_call(
        kern,
        grid=(B, Lc // tm),
        in_specs=[pl.BlockSpec((None, tm, D), row), pl.BlockSpec((None, tm, D), row),
                  pl.BlockSpec((None, 1, 6 * D), lambda b, i: (b, 0, 0)),
                  pl.BlockSpec((None, 1, 6 * D), lambda b, i: (nb, 0, 0)),
                  pl.BlockSpec((1, D), lambda b, i: (0, 0)), pl.BlockSpec((1, D), lambda b, i: (0, 0))],
        out_specs=pl.BlockSpec((None, tm, D), row),
        out_shape=jax.ShapeDtypeStruct((B, Lc, D), f32),
        compiler_params=_cp("parallel", "parallel"),
        name="final_ln",
    )(x, f, mods, mods, ln_g.reshape(1, D), ln_b.reshape(1, D))


def _seq_reverse(t, C):
    return jnp.concatenate([jnp.flip(t[:, :C], axis=1), jnp.flip(t[:, C:], axis=1)], axis=1)


S5_PAIRS = S5_GROUPS // 2
S5_ROWS_PER_BLOCK = 8


def _s5_pair_tables(lam_re, lam_im, b_re, b_im, c_re, c_im, log_dt):
    T1, G, P, J = S5_BLOCK, S5_GROUPS, S5_STATE, S5_GROUP
    n = T1 * J
    t_idx = jnp.arange(T1)
    lag = t_idx[None, :] - t_idx[:, None]
    toep = 0.0
    bc_secs, cc_secs, a_secs = [], [], []
    for d in range(2):
        tb = _s5_tables_dir(lam_re[d], lam_im[d], b_re[d], b_im[d], c_re[d], c_im[d], log_dt[d])
        kern, pw, b_bar, c = tb
        use = (lag >= 0) if d == 0 else (lag <= 0)
        kk = jnp.transpose(kern[jnp.clip(jnp.abs(lag), 0, T1 - 1)], (2, 0, 4, 1, 3))
        toep = toep + jnp.where(use[None, :, None, :, None], kk, 0.0).reshape(G, n, n)
        p_in = pw[T1 - 1 - t_idx] if d == 0 else pw[t_idx]
        bc = jnp.transpose(p_in[:, :, :, None] * b_bar[None], (1, 0, 3, 2)).reshape(G, n, P)
        p_out = pw[t_idx + 1] if d == 0 else pw[T1 - t_idx]
        cc = jnp.transpose(c[None] * p_out[:, :, None, :], (1, 3, 0, 2)).reshape(G, P, n)
        bc_secs += [bc.real, bc.imag]
        cc_secs += [cc.real, -cc.imag]
        a_secs += [pw[T1].real, pw[T1].imag]
    eye2 = jnp.eye(2, dtype=f32)
    bc = jnp.stack(bc_secs, axis=2).reshape(S5_PAIRS, 2, n, 4, P)
    bc = (bc[:, :, :, :, None, :] * eye2[None, :, None, None, :, None]).reshape(S5_PAIRS, 2 * n, 4 * 2 * P)
    cc = jnp.stack(cc_secs, axis=1).reshape(S5_PAIRS, 2, 4, P, n)
    cc = jnp.transpose(cc, (0, 2, 1, 3, 4))
    cc = (cc[:, :, :, :, None, :] * eye2[None, None, :, None, :, None]).reshape(S5_PAIRS, 4 * 2 * P, 2 * n)
    a = jnp.transpose(jnp.stack(a_secs, axis=1).reshape(S5_PAIRS, 2, 4, P), (0, 2, 1, 3)).reshape(S5_PAIRS, 1, 4 * 2 * P)
    return toep.reshape(S5_PAIRS, 2, n, n).astype(bf16), bc.astype(bf16), cc.astype(bf16), a


def _s5_tables_dir(lam_re, lam_im, b_re, b_im, c_re, c_im, log_dt):
    T1 = S5_BLOCK
    lam = lax.complex(lam_re.astype(f32), lam_im.astype(f32))
    lam_dt = lam * jnp.exp(log_dt.astype(f32))[:, None]
    lam_bar = jnp.exp(lam_dt)
    b_bar = ((lam_bar - 1.0) / lam)[..., None] * lax.complex(b_re.astype(f32), b_im.astype(f32))
    c = lax.complex(c_re.astype(f32), c_im.astype(f32))
    pw = jnp.exp(lam_dt[None] * jnp.arange(T1 + 1, dtype=f32)[:, None, None])
    kern = jnp.einsum('gip,tgp,gpj->tgij', c, pw[:T1], b_bar, precision=lax.Precision.HIGHEST).real
    return kern, pw, b_bar, c


def _s5_kernel(u_ref, toep_ref, bc_ref, cc_ref, a_ref, y_ref, gin_scr, hp_scr, *, nb, nb_ctx):
    RB = S5_ROWS_PER_BLOCK
    u = u_ref[...]
    gin_scr[...] = jnp.dot(u, bc_ref[...], preferred_element_type=f32)
    a = a_ref[...]
    afr, afi, abr, abi = (a[:, k * LANE:(k + 1) * LANE] for k in range(4))

    def body(c, carry):
        hfr, hfi, hbr, hbi = carry
        cb = jnp.where(c < nb_ctx, nb_ctx - 1 - c, nb - 1 - (c - nb_ctx))
        rf = pl.multiple_of(c * RB, RB)
        rb = pl.multiple_of(cb * RB, RB)
        hp_scr[pl.ds(rf, RB), 0:LANE] = hfr
        hp_scr[pl.ds(rf, RB), LANE:2 * LANE] = hfi
        hp_scr[pl.ds(rb, RB), 2 * LANE:3 * LANE] = hbr
        hp_scr[pl.ds(rb, RB), 3 * LANE:4 * LANE] = hbi
        gf = gin_scr[pl.ds(rf, RB), 0:2 * LANE]
        gb = gin_scr[pl.ds(rb, RB), 2 * LANE:4 * LANE]
        return (afr * hfr - afi * hfi + gf[:, :LANE], afr * hfi + afi * hfr + gf[:, LANE:],
                abr * hbr - abi * hbi + gb[:, :LANE], abr * hbi + abi * hbr + gb[:, LANE:])

    z = jnp.zeros((RB, LANE), f32)
    lax.fori_loop(0, nb, body, (z, z, z, z))
    n = S5_BLOCK * S5_GROUP
    y = jnp.dot(hp_scr[...].astype(bf16), cc_ref[...], preferred_element_type=f32)
    y_ref[:, 0:n] = y[:, 0:n] + jnp.dot(u[:, 0:n], toep_ref[0], preferred_element_type=f32)
    y_ref[:, n:2 * n] = y[:, n:2 * n] + jnp.dot(u[:, n:2 * n], toep_ref[1], preferred_element_type=f32)


def _s5_mixer(s, C, lam_re, lam_im, b_re, b_im, c_re, c_im, log_dt):
    B, Lc, _ = s.shape
    T1, J, RB = S5_BLOCK, S5_GROUP, S5_ROWS_PER_BLOCK
    assert B <= RB and C % T1 == 0 and Lc % T1 == 0
    nb = Lc // T1
    n = T1 * J
    toep, bc, cc, a = _s5_pair_tables(lam_re, lam_im, b_re, b_im, c_re, c_im, log_dt)
    u = s.astype(bf16).reshape(B, nb, T1, S5_PAIRS, 2, J).transpose(3, 1, 0, 4, 2, 5)
    u = jnp.pad(u, ((0, 0), (0, 0), (0, RB - B), (0, 0), (0, 0), (0, 0))).reshape(S5_PAIRS, nb * RB, 2 * n)
    R = nb * RB
    kern = functools.partial(_s5_kernel, nb=nb, nb_ctx=C // T1)
    y = pl.pallas_call(
        kern,
        grid=(S5_PAIRS,),
        in_specs=[pl.BlockSpec((None, R, 2 * n), lambda q: (q, 0, 0)),
                  pl.BlockSpec((None, 2, n, n), lambda q: (q, 0, 0, 0)),
                  pl.BlockSpec((None, 2 * n, 4 * LANE), lambda q: (q, 0, 0)),
                  pl.BlockSpec((None, 4 * LANE, 2 * n), lambda q: (q, 0, 0)),
                  pl.BlockSpec((None, 1, 4 * LANE), lambda q: (q, 0, 0))],
        out_specs=pl.BlockSpec((None, R, 2 * n), lambda q: (q, 0, 0)),
        out_shape=jax.ShapeDtypeStruct((S5_PAIRS, R, 2 * n), f32),
        scratch_shapes=[pltpu.VMEM((R, 4 * LANE), f32), pltpu.VMEM((R, 4 * LANE), f32)],
        compiler_params=_cp("parallel"),
        name="s5_scan",
    )(u, toep, bc, cc, a)
    y = y.reshape(S5_PAIRS, nb, RB, 2, T1, J)[:, :, :B].transpose(2, 1, 4, 0, 3, 5)
    return y.reshape(B, Lc, S5_CHANNELS)


def _dotT(a, b):
    return lax.dot_general(a.astype(bf16), b.astype(bf16), (((1,), (1,)), ((), ())), preferred_element_type=f32)


def _dot(a, b):
    return jnp.dot(a.astype(bf16), b.astype(bf16), preferred_element_type=f32)


def _split(a):
    hi = a.astype(bf16)
    return hi, (a - hi.astype(f32)).astype(bf16)


def _dot_split(a, b):
    d = lambda x, y: jnp.dot(x, y, preferred_element_type=f32)
    return d(a[0], b[0]) + d(a[1], b[0]) + d(a[0], b[1])


def _gdn_chunk_index(n, n_ctx, n_all, reverse):
    if not reverse:
        return n
    return jnp.where(n < n_ctx, n_ctx - 1 - n, n_all - 1 - (n - n_ctx))


def _gdn_kernel(x_ref, xp_ref, xn_ref, gt_ref, cw_ref, alog_ref, dtb_ref, o_ref, s_scr, *, C, Lc, Cn, reverse):
    n = pl.program_id(1)
    n_ctx, n_all = C // Cn, Lc // Cn
    ci = _gdn_chunk_index(n, n_ctx, n_all, reverse)
    H, dk, dv = GDN_HEADS, GDN_DK, GDN_DV
    d = 1 if reverse else 0

    @pl.when(n == 0)
    def _():
        s_scr[...] = jnp.zeros_like(s_scr)

    x = x_ref[...]
    loc = lax.broadcasted_iota(jnp.int32, (Cn, 1), 0)
    row = ci * Cn + loc
    prev = jnp.where(loc == 0, xp_ref[7:8, :], pltpu.roll(x, 1, axis=0))
    prev = jnp.where(jnp.logical_or(row == 0, row == C), 0.0, prev)
    nxt = jnp.where(loc == Cn - 1, xn_ref[0:1, :], pltpu.roll(x, Cn - 1, axis=0))
    nxt = jnp.where(jnp.logical_or(row == C - 1, row == Lc - 1), 0.0, nxt)
    y = prev * cw_ref[0:1, :] + x * cw_ref[1:2, :] + nxt * cw_ref[2:3, :]
    y = y * _sigmoid(y)

    graw = gt_ref[...]
    z = graw + dtb_ref[...]
    g_all = -jnp.exp(alog_ref[...]) * (jnp.maximum(z, 0.0) + jnp.log(1.0 + jnp.exp(-jnp.abs(z))))
    beta_all = _sigmoid(graw)

    ii = lax.broadcasted_iota(jnp.int32, (Cn, Cn), 0)
    jj = lax.broadcasted_iota(jnp.int32, (Cn, Cn), 1)
    incl = (ii <= jj) if reverse else (ii >= jj)
    strict = (ii < jj) if reverse else (ii > jj)
    eye = (ii == jj).astype(f32)
    tri = incl.astype(bf16)
    g_hi = g_all.astype(bf16)
    g_lo = (g_all - g_hi.astype(f32)).astype(bf16)
    gc_all = jnp.dot(tri, g_hi, preferred_element_type=f32) + jnp.dot(tri, g_lo, preferred_element_type=f32)
    gc_all_t = gc_all.T
    last = 0 if reverse else Cn - 1

    for h in range(H):
        lane = 8 * d + h
        gc = gc_all[:, lane:lane + 1]
        gc_row = gc_all_t[lane:lane + 1, :]
        gl = gc_all[last:last + 1, lane:lane + 1]
        beta = beta_all[:, 16 + lane:17 + lane]
        decay = jnp.where(incl, jnp.exp(jnp.where(incl, gc - gc_row, 0.0)), 0.0)
        q = y[:, h * dk:(h + 1) * dk]
        k = y[:, H * dk + h * dk:H * dk + (h + 1) * dk]
        v = y[:, 2 * H * dk + h * dv:2 * H * dk + (h + 1) * dv]
        q = q * lax.rsqrt(jnp.sum(q * q, axis=1, keepdims=True) + 1e-6) * (dk ** -0.5)
        k = k * lax.rsqrt(jnp.sum(k * k, axis=1, keepdims=True) + 1e-6)
        kb = k * beta
        m = jnp.where(strict, _dotT(kb, k) * decay, 0.0)
        a = jnp.where(incl, _dotT(q, k) * decay, 0.0)
        t_inv = eye - m
        pw = _split(m)
        for _ in range(int(math.log2(Cn)) - 1):
            pw = _split(_dot_split(pw, pw))
            t_inv = t_inv + _dot_split(_split(t_inv), pw)
        e_gc = jnp.exp(gc)
        uw = _dot(t_inv, jnp.concatenate([v * beta, kb * e_gc], axis=1))
        s_h = s_scr[h]
        r = _dot(jnp.concatenate([uw[:, dv:], q * e_gc], axis=0), s_h)
        v_new = uw[:, :dv] - r[:Cn]
        o_ref[:, h * dv:(h + 1) * dv] = r[Cn:] + _dot(a, v_new)
        kd = k * jnp.exp(gl - gc)
        s_scr[h] = s_h * jnp.exp(gl) + lax.dot_general(kd.astype(bf16), v_new.astype(bf16), (((0,), (0,)), ((), ())),
                                                       preferred_element_type=f32)


def _gdn_direction(p, conv_w, alog_row, dtb_row, *, C, reverse):
    B, Lc, _ = p.shape
    Cn = LANE
    n_ctx, n_all = C // Cn, Lc // Cn
    nq = GDN_QKV
    ci = lambda n: _gdn_chunk_index(n, n_ctx, n_all, reverse)
    sub = Cn // 8
    kern = functools.partial(_gdn_kernel, C=C, Lc=Lc, Cn=Cn, reverse=reverse)
    return pl.pallas_call(
        kern,
        grid=(B, n_all),
        in_specs=[pl.BlockSpec((None, Cn, nq), lambda b, n: (b, ci(n), 0)),
                  pl.BlockSpec((None, 8, nq), lambda b, n: (b, jnp.maximum(ci(n) * sub - 1, 0), 0)),
                  pl.BlockSpec((None, 8, nq), lambda b, n: (b, jnp.minimum((ci(n) + 1) * sub, Lc // 8 - 1), 0)),
                  pl.BlockSpec((None, Cn, LANE), lambda b, n: (b, ci(n), O_GATE // LANE)),
                  pl.BlockSpec((3, nq), lambda b, n: (0, 0)),
                  pl.BlockSpec((1, LANE), lambda b, n: (0, 0)),
                  pl.BlockSpec((1, LANE), lambda b, n: (0, 0))],
        out_specs=pl.BlockSpec((None, Cn, GDN_HEADS * GDN_DV), lambda b, n: (b, ci(n), 0)),
        out_shape=jax.ShapeDtypeStruct((B, Lc, GDN_HEADS * GDN_DV), f32),
        scratch_shapes=[pltpu.VMEM((GDN_HEADS, GDN_DK, GDN_DV), f32)],
        compiler_params=_cp("parallel", "arbitrary"),
        name="gdn_rev" if reverse else "gdn_fwd",
    )(p, p, p, p, conv_w, alog_row, dtb_row)


def _gdn_pallas(p, C, conv_w, a_log, dt_bias):
    outs = []
    for d in range(2):
        pad = lambda t: jnp.zeros((1, LANE), f32).at[0, 8 * d:8 * d + GDN_HEADS].set(t[d].astype(f32))
        outs.append(_gdn_direction(p, conv_w, pad(a_log), pad(dt_bias), C=C, reverse=(d == 1)))
    return outs


def _gdn_scan(q, k, v, g, beta):
    B, L, H, _ = q.shape
    Cn = GDN_CHUNK
    N = L // Cn
    chunks = lambda t: t.reshape(B, N, Cn, H, -1).transpose(1, 0, 3, 2, 4)
    qc, kc, vc = chunks(q), chunks(k), chunks(v)
    gc = g.reshape(B, N, Cn, H).transpose(1, 0, 3, 2)
    bc = beta.reshape(B, N, Cn, H).transpose(1, 0, 3, 2)
    gcum = jnp.cumsum(gc, axis=-1)
    idx = jnp.arange(Cn)
    incl = idx[:, None] >= idx[None, :]
    strict = idx[:, None] > idx[None, :]
    decay = jnp.exp(jnp.where(incl, gcum[..., :, None] - gcum[..., None, :], -jnp.inf))
    kb = kc * bc[..., None]
    m = jnp.where(strict, jnp.einsum('nbhid,nbhjd->nbhij', kb, kc) * decay, 0.0)
    tri = m + jnp.eye(Cn, dtype=f32)
    solve = functools.partial(lax.linalg.triangular_solve, left_side=True, lower=True, unit_diagonal=True)
    u = solve(tri, vc * bc[..., None])
    w = solve(tri, kb * jnp.exp(gcum)[..., None])
    a_intra = jnp.where(incl, jnp.einsum('nbhid,nbhjd->nbhij', qc, kc) * decay, 0.0)
    qg = qc * jnp.exp(gcum)[..., None]
    g_last = gcum[..., -1]
    kd = kc * jnp.exp(g_last[..., None] - gcum)[..., None]

    def step(S, xs):
        qg_i, kd_i, u_i, w_i, a_i, gl_i = xs
        v_new = u_i - jnp.einsum('bhck,bhkv->bhcv', w_i, S)
        o = jnp.einsum('bhck,bhkv->bhcv', qg_i, S) + jnp.einsum('bhcj,bhjv->bhcv', a_i, v_new)
        S = S * jnp.exp(gl_i)[..., None, None] + jnp.einsum('bhck,bhcv->bhkv', kd_i, v_new)
        return S, o

    s0 = jnp.zeros((B, H, q.shape[-1], v.shape[-1]), f32)
    _, o = lax.scan(step, s0, (qg, kd, u, w, a_intra, g_last))
    return o.transpose(1, 0, 3, 2, 4).reshape(B, L, H * v.shape[-1])


def _gdn_mixer(p, C, conv_w, a_log, dt_bias):
    B, Lc, _ = p.shape
    raw = p[..., O_QKV:O_QKV + GDN_QKV]
    row = jnp.arange(Lc)[None, :, None]
    prev = jnp.where((row == 0) | (row == C), 0.0, jnp.roll(raw, 1, axis=1))
    nxt = jnp.where((row == C - 1) | (row == Lc - 1), 0.0, jnp.roll(raw, -1, axis=1))
    qkv = prev * conv_w[0] + raw * conv_w[1] + nxt * conv_w[2]
    qkv = qkv * _sigmoid(qkv)
    nk = GDN_HEADS * GDN_DK
    l2 = lambda x: x * lax.rsqrt(jnp.sum(jnp.square(x), axis=-1, keepdims=True) + 1e-6)
    q = l2(qkv[..., :nk].reshape(B, Lc, GDN_HEADS, GDN_DK)) * (GDN_DK ** -0.5)
    k = l2(qkv[..., nk:2 * nk].reshape(B, Lc, GDN_HEADS, GDN_DK))
    v = qkv[..., 2 * nk:].reshape(B, Lc, GDN_HEADS, GDN_DV)
    gates = p[..., O_GATE:O_GATE + 4 * GDN_HEADS].reshape(B, Lc, 4, GDN_HEADS)
    outs = []
    for d in range(2):
        g = -jnp.exp(a_log[d].astype(f32)) * jax.nn.softplus(gates[:, :, d] + dt_bias[d].astype(f32))
        beta = _sigmoid(gates[:, :, 2 + d])
        if d == 0:
            outs.append(_gdn_scan(q, k, v, g, beta))
        else:
            r = lambda t: _seq_reverse(t, C)
            outs.append(r(_gdn_scan(r(q), r(k), r(v), r(g), r(beta))))
    return outs


def _dispatch_plan(e_idx, n_tiles):
    T = e_idx.shape[0]
    tm = MOE_TM
    flat = e_idx.reshape(-1)
    onehot = (flat[:, None] == jnp.arange(N_EXPERTS)[None, :]).astype(jnp.int32)
    rank = jnp.sum((jnp.cumsum(onehot, axis=0) - onehot) * onehot, axis=1)
    counts = jnp.sum(onehot, axis=0)
    padded = ((counts + tm - 1) // tm) * tm
    ends = jnp.cumsum(padded)
    offs = ends - padded
    pos = offs[flat] + rank
    row_token = jnp.zeros((n_tiles * tm,), jnp.int32).at[pos].set(jnp.arange(2 * T, dtype=jnp.int32) // 2)
    tile_start = jnp.arange(n_tiles, dtype=jnp.int32) * tm
    tile_expert = jnp.minimum(jnp.sum((tile_start[:, None] >= ends[None, :]).astype(jnp.int32), axis=1), N_EXPERTS - 1)
    n_active = (ends[-1] // tm).astype(jnp.int32).reshape(1)
    last_e = tile_expert[jnp.maximum(n_active[0] - 1, 0)]
    tile_expert = jnp.where(tile_start < ends[-1], tile_expert, last_e).astype(jnp.int32)
    return row_token, pos.reshape(T, 2), tile_expert, n_active


def _moe(u, logits, bias_row, wg, wu, wd):
    B, Lc, D = u.shape
    T = B * Lc
    r = _router(logits.reshape(T, LANE), bias_row)
    e_idx = r[:, 0:2].astype(jnp.int32)
    wsel = r[:, 2:4]
    n_tiles = (2 * T + MOE_TM - 1) // MOE_TM + N_EXPERTS
    row_token, pos, tile_expert, n_active = _dispatch_plan(e_idx, n_tiles)
    xs = u.reshape(T, D)[row_token]
    ys = _moe_experts(xs, tile_expert, n_active, wg, wu, wd)
    f = wsel[:, 0:1] * ys[pos[:, 0]] + wsel[:, 1:2] * ys[pos[:, 1]]
    return f.reshape(B, Lc, D)


def _rope_tables(L, C):
    rows = L // GRID_W
    t_row = jnp.repeat(jnp.arange(rows), GRID_W).astype(f32)
    t_col = jnp.tile(jnp.arange(GRID_W), rows).astype(f32)
    n_freq = ROPE_DIM // 4
    inv_freq = ROPE_BASE ** (-jnp.arange(n_freq, dtype=f32) / n_freq)
    ang = jnp.concatenate([t_row[:, None] * inv_freq, t_col[:, None] * inv_freq], axis=-1)
    cos = jnp.repeat(jnp.cos(ang), 2, axis=1)
    sin = jnp.repeat(jnp.sin(ang), 2, axis=1) * jnp.tile(jnp.array([-1.0, 1.0], f32), ROPE_DIM // 2)
    one = jnp.ones((L, ROPE_DIM), f32)
    zero = jnp.zeros((L, ROPE_DIM), f32)
    ctx = lambda t, fill: jnp.concatenate([jnp.full((C, LANE), fill, f32), t], axis=0)
    return (ctx(jnp.concatenate([cos, cos], 1), 1.0), ctx(jnp.concatenate([sin, sin], 1), 0.0),
            ctx(jnp.concatenate([cos, one], 1), 1.0), ctx(jnp.concatenate([sin, zero], 1), 0.0))


def _prep_even_w_in(w):
    D = w.shape[0]
    nqk = 2 * DA_HEADS * DA_HEAD_DIM
    heads_first = lambda t: t.reshape(D, 2, DA_HEADS, DA_HEAD_DIM).transpose(0, 2, 1, 3).reshape(D, nqk)
    q = heads_first(w[:, :nqk]) * (DA_HEAD_DIM ** -0.5)
    k = heads_first(w[:, nqk:2 * nqk])
    return jnp.concatenate([q, k, w[:, 2 * nqk:]], axis=1).astype(bf16)


def _prep_odd_w_in(w):
    D = w.shape[0]
    z = lambda n: jnp.zeros((D, n), w.dtype)
    c0 = MLA_Q_RANK
    c1 = c0 + MLA_KV_RANK
    c2 = c1 + MLA_ROPE
    c3 = c2 + GDN_QKV
    c4 = c3 + GDN_HEADS * GDN_DV
    parts = [w[:, c2:c3], w[:, c3:c4], w[:, :c0], w[:, c0:c1], w[:, c1:c2], z(LANE - MLA_ROPE),
             w[:, c4:], z(LANE - 4 * GDN_HEADS)]
    out = jnp.concatenate(parts, axis=1)
    assert out.shape[1] == O_IN
    return out.astype(bf16)


def _prep_mla_w(w_uq, w_ukv):
    H = MLA_HEADS
    rq = w_uq.shape[0]
    wq = w_uq.reshape(rq, H, MLA_NOPE + MLA_ROPE)
    nope = wq[:, :, :MLA_NOPE].reshape(rq, H * MLA_NOPE)
    rope = jnp.concatenate([wq[:, :, MLA_NOPE:], jnp.zeros((rq, H, LANE - MLA_ROPE), w_uq.dtype)], axis=2)
    wq_p = jnp.concatenate([nope, rope.reshape(rq, H * LANE)], axis=1).astype(bf16)
    rkv = w_ukv.shape[0]
    wkv = w_ukv.reshape(rkv, H, MLA_NOPE + MLA_V)
    wkv_p = jnp.concatenate([wkv[:, :, :MLA_NOPE].reshape(rkv, H * MLA_NOPE),
                             wkv[:, :, MLA_NOPE:].reshape(rkv, H * MLA_V)], axis=1).astype(bf16)
    return wq_p, wkv_p


def kernel(x, c, ctx, c_ctx, mod_w, mod_b, ln_g, ln_b, e_w_in, e_w_out, da_lam, da_subln, s5_lam_re, s5_lam_im, s5_b_re, s5_b_im, s5_c_re, s5_c_im, s5_log_dt, s5_d, s5_glu_w, s5_glu_b, o_w_in, o_w_out, mla_q_norm, mla_kv_norm, mla_w_uq, mla_w_ukv, gdn_conv, gdn_a_log, gdn_dt_bias, gdn_norm, router_w, router_bias, moe_w_gate, moe_w_up, moe_w_down):
    B, L, D = x.shape
    C = ctx.shape[1]
    depth = mod_w.shape[0]
    assert depth == DEPTH

    R = -(-(B + 1) // 8) * 8
    cc = jnp.concatenate([c, c_ctx[None, :], jnp.zeros((R - B - 1, D), f32)], axis=0)
    mods_all = _modulation(cc, mod_w, mod_b)[:, :B + 1].reshape(depth, B + 1, 1, 6 * D)

    cos_e, sin_e, cos_o, sin_o = _rope_tables(L, C)
    r_pad = jnp.concatenate([router_w.astype(f32), jnp.zeros((D, LANE - N_EXPERTS), f32)], axis=1)
    r_hi = r_pad.astype(bf16)
    r_lo = (r_pad - r_hi.astype(f32)).astype(bf16)
    bias_row = jnp.concatenate([router_bias.astype(f32), jnp.zeros((LANE - N_EXPERTS,), f32)]).reshape(1, LANE)

    xs = jnp.concatenate([ctx, x], axis=1)
    n_att = DA_HEADS * DA_V_DIM
    for l in range(depth):
        mods = mods_all[l]
        i = l // 2
        if l % 2 == 0:
            qkv, s = _inproj(xs, mods, _prep_even_w_in(e_w_in[i]), cos_e, sin_e, C=C,
                             n_rope_cols=4 * DA_HEADS * DA_HEAD_DIM, n_a_cols=3072, a_dtype=bf16, b_dtype=f32)
            a1 = _diff_attention(qkv, da_lam[i].astype(f32), da_subln[i].astype(f32), C=C,
                                 lam_init=0.8 - 0.6 * math.exp(-0.3 * l))
            y = _s5_mixer(s, C, s5_lam_re[i], s5_lam_im[i], s5_b_re[i], s5_b_im[i], s5_c_re[i], s5_c_im[i],
                          s5_log_dt[i])
            a2 = _s5_finish(y, s, s5_d[i], s5_glu_w[i], s5_glu_b[i])
            w_out = e_w_out[i].astype(bf16)
        else:
            (p,) = _inproj(xs, mods, _prep_odd_w_in(o_w_in[i]), cos_e, sin_e, C=C,
                           n_rope_cols=0, n_a_cols=O_IN, a_dtype=f32, b_dtype=f32)
            wq_p, wkv_p = _prep_mla_w(mla_w_uq[i], mla_w_ukv[i])
            q, kv, kr = _mla_up(p, mla_q_norm[i], mla_kv_norm[i], wq_p, wkv_p, cos_o, sin_o)
            a1 = _mla_attention(q, kv, kr, C=C)
            of, ob = _gdn_pallas(p, C, gdn_conv[i].astype(f32), gdn_a_log[i], gdn_dt_bias[i])
            a2 = _gdn_finish(of, ob, p, gdn_norm[i])
            w_out = o_w_out[i].astype(bf16)
        xs, u, logits = _merge(xs, a1, a2, w_out[:n_att], w_out[n_att:], mods, ln_g[l, 0], ln_b[l, 0],
                               r_hi, r_lo, C=C)
        f = _moe(u, logits, bias_row, moe_w_gate[l].astype(bf16), moe_w_up[l].astype(bf16),
                 moe_w_down[l].astype(bf16))
        xs = _final_ln(xs, f, mods, ln_g[l, 1], ln_b[l, 1], C=C)
    return xs[:, C:, :]
```

```python
import functools
import math

import jax
import jax.numpy as jnp
import numpy as np
from jax import lax
from jax.experimental import pallas as pl
from jax.experimental.pallas import tpu as pltpu

f32 = jnp.float32
bf16 = jnp.bfloat16

GRID_W = 64
ROPE_DIM = 64
ROPE_BASE = 10000.0
DA_HEADS = 8
DA_HEAD_DIM = 64
DA_V_DIM = 128
S5_CHANNELS = 512
S5_GROUP = 16
S5_GROUPS = 32
S5_STATE = 64
S5_BLOCK = 16
MLA_HEADS = 8
MLA_Q_RANK = 512
MLA_KV_RANK = 256
MLA_NOPE = 128
MLA_ROPE = 64
MLA_V = 128
GDN_HEADS = 8
GDN_DK = 128
GDN_DV = 128
GDN_QKV = 3072
GDN_CHUNK = 64
N_EXPERTS = 16
N_GROUPS = 4
EXPERTS_PER_GROUP = 4
DEPTH = 4
DEEPNORM_ALPHA = (2 * DEPTH) ** 0.25
LANE = 128
MOE_TM = 256
VMEM_LIMIT = 56 * 1024 * 1024

O_QKV, O_Z, O_CQ, O_CKV, O_KR, O_GATE, O_IN = 0, 3072, 4096, 4608, 4864, 4992, 5120


def _cp(*sem):
    return pltpu.CompilerParams(dimension_semantics=sem, vmem_limit_bytes=VMEM_LIMIT)


def _tile(n, cap, mult=16):
    best = None
    for t in range(mult, min(n, cap) + 1, mult):
        if n % t == 0:
            best = t
    assert best is not None, (n, cap, mult)
    return best


def _sigmoid(x):
    return 1.0 / (1.0 + jnp.exp(-x))


def _mod_kernel(a_ref, w_ref, b_ref, o_ref):
    a = a_ref[...]
    act = (a * _sigmoid(a)).astype(bf16)
    o_ref[...] = jnp.dot(act, w_ref[...].astype(bf16), preferred_element_type=f32) + b_ref[...]


def _modulation(cc, mod_w, mod_b):
    depth, D, N = mod_w.shape
    R = cc.shape[0]
    tn = _tile(N, 1024, LANE)
    return pl.pallas_call(
        _mod_kernel,
        grid=(depth, N // tn),
        in_specs=[pl.BlockSpec((R, D), lambda l, j: (0, 0)),
                  pl.BlockSpec((None, D, tn), lambda l, j: (l, 0, j)),
                  pl.BlockSpec((None, 1, tn), lambda l, j: (l, 0, j))],
        out_specs=pl.BlockSpec((None, R, tn), lambda l, j: (l, 0, j)),
        out_shape=jax.ShapeDtypeStruct((depth, R, N), f32),
        compiler_params=_cp("parallel", "parallel"),
        name="modulation",
    )(cc, mod_w, mod_b.reshape(depth, 1, N))


def _select_mod(ml, mc, row0, tm, C, D, chunks):
    is_ctx = (row0 + lax.broadcasted_iota(jnp.int32, (tm, 1), 0)) < C
    return [jnp.where(is_ctx, mc[:, k * D:(k + 1) * D], ml[:, k * D:(k + 1) * D]) for k in chunks]


def _rope(seg, cos, sin):
    nxt = pltpu.roll(seg, LANE - 1, axis=1)
    prv = pltpu.roll(seg, 1, axis=1)
    even = (lax.broadcasted_iota(jnp.int32, seg.shape, 1) % 2) == 0
    return seg * cos + jnp.where(even, nxt, prv) * sin


def _inproj_kernel(x_ref, ml_ref, mc_ref, w_ref, cos_ref, sin_ref, *rest, C, tm, tn, D, n_rope, n_a, has_b):
    if has_b:
        oa_ref, ob_ref, u_scr = rest
    else:
        oa_ref, u_scr = rest
        ob_ref = None
    i = pl.program_id(1)
    j = pl.program_id(2)

    @pl.when(j == 0)
    def _():
        shift, scale = _select_mod(ml_ref[...], mc_ref[...], i * tm, tm, C, D, (0, 1))
        u_scr[...] = (x_ref[...] * (1.0 + scale) + shift).astype(bf16)

    def acc():
        return jnp.dot(u_scr[...], w_ref[...], preferred_element_type=f32)

    if n_rope > 0:
        @pl.when(j < n_rope)
        def _():
            a = acc()
            cos = cos_ref[...]
            sin = sin_ref[...]
            for c in range(tn // LANE):
                oa_ref[:, c * LANE:(c + 1) * LANE] = _rope(a[:, c * LANE:(c + 1) * LANE], cos, sin).astype(oa_ref.dtype)

    @pl.when(jnp.logical_and(j >= n_rope, j < n_a))
    def _():
        oa_ref[...] = acc().astype(oa_ref.dtype)

    if has_b:
        @pl.when(j >= n_a)
        def _():
            ob_ref[...] = acc().astype(ob_ref.dtype)


def _inproj(x, mods, w, cos, sin, *, C, n_rope_cols, n_a_cols, a_dtype, b_dtype):
    B, Lc, D = x.shape
    N = w.shape[1]
    tm = _tile(Lc, 1088)
    tn = 512
    assert N % tn == 0 and n_rope_cols % tn == 0 and n_a_cols % tn == 0
    n_a = n_a_cols // tn
    has_b = n_a_cols < N
    kern = functools.partial(_inproj_kernel, C=C, tm=tm, tn=tn, D=D, n_rope=n_rope_cols // tn, n_a=n_a, has_b=has_b)
    out_shape = [jax.ShapeDtypeStruct((B, Lc, n_a_cols), a_dtype)]
    out_specs = [pl.BlockSpec((None, tm, tn), lambda b, i, j: (b, i, jnp.minimum(j, n_a - 1)))]
    if has_b:
        out_shape.append(jax.ShapeDtypeStruct((B, Lc, N - n_a_cols), b_dtype))
        out_specs.append(pl.BlockSpec((None, tm, tn), lambda b, i, j: (b, i, jnp.maximum(j - n_a, 0))))
    nb = mods.shape[0] - 1
    return pl.pallas_call(
        kern,
        grid=(B, Lc // tm, N // tn),
        in_specs=[pl.BlockSpec((None, tm, D), lambda b, i, j: (b, i, 0)),
                  pl.BlockSpec((None, 1, 6 * D), lambda b, i, j: (b, 0, 0)),
                  pl.BlockSpec((None, 1, 6 * D), lambda b, i, j: (nb, 0, 0)),
                  pl.BlockSpec((D, tn), lambda b, i, j: (0, j)),
                  pl.BlockSpec((tm, LANE), lambda b, i, j: (i, 0)),
                  pl.BlockSpec((tm, LANE), lambda b, i, j: (i, 0))],
        out_specs=out_specs,
        out_shape=out_shape,
        scratch_shapes=[pltpu.VMEM((tm, D), bf16)],
        compiler_params=_cp("parallel", "parallel", "arbitrary"),
        name="inproj",
    )(x, mods, mods, w, cos, sin)


def _transpose_bf16(x):
    return x.astype(f32).T.astype(bf16)


ATTN_KEY_CHUNK = 512


def _attend_t(ks, q_ts, vt_ref, nkeys, scale=None):
    n = len(ks)
    chunks = [(c0, min(c0 + ATTN_KEY_CHUNK, nkeys)) for c0 in range(0, nkeys, ATTN_KEY_CHUNK)]

    def scores(c):
        out = [jnp.dot(ks[i][c[0]:c[1]], q_ts[i], preferred_element_type=f32) for i in range(n)]
        return out if scale is None else [t * scale for t in out]

    m, l, acc = [None] * n, [None] * n, [None] * n
    s_next = scores(chunks[0])
    for j, (c0, c1) in enumerate(chunks):
        s_cur = s_next
        if j + 1 < len(chunks):
            s_next = scores(chunks[j + 1])
        v_c = vt_ref[:, c0:c1]
        for i in range(n):
            mc = jnp.max(s_cur[i], axis=0, keepdims=True)
            m_new = mc if m[i] is None else jnp.maximum(m[i], mc)
            p = jnp.exp(s_cur[i] - m_new)
            pv = jnp.dot(v_c, p.astype(bf16), preferred_element_type=f32)
            if m[i] is None:
                l[i], acc[i] = jnp.sum(p, axis=0, keepdims=True), pv
            else:
                alpha = jnp.exp(m[i] - m_new)
                l[i] = alpha * l[i] + jnp.sum(p, axis=0, keepdims=True)
                acc[i] = alpha * acc[i] + pv
            m[i] = m_new
    return [acc[i] / l[i] for i in range(n)]


def _diff_attn_kernel(q_ref, k_ref, v_ref, lam_ref, sub_ref, o_ref, vt_scr, *, C, tq, lam_init):
    qi = pl.program_id(2)

    @pl.when(qi == 0)
    def _():
        vt_scr[...] = _transpose_bf16(v_ref[...])

    lv = lam_ref[...]
    lam = (jnp.exp(jnp.sum(lv[0:1] * lv[1:2], axis=1, keepdims=True))
           - jnp.exp(jnp.sum(lv[2:3] * lv[3:4], axis=1, keepdims=True)) + lam_init)

    def run(nkeys):
        q_t = _transpose_bf16(q_ref[...])
        k = k_ref[0:nkeys, :]
        hd = DA_HEAD_DIM
        o1, o2 = _attend_t([k[:, 0:hd], k[:, hd:2 * hd]], [q_t[0:hd, :], q_t[hd:2 * hd, :]], vt_scr, nkeys)
        o = (o1 - lam * o2).T
        o = o * lax.rsqrt(jnp.mean(o * o, axis=1, keepdims=True) + 1e-6) * sub_ref[...]
        o_ref[...] = (o * (1.0 - lam_init)).astype(o_ref.dtype)

    @pl.when(qi * tq < C)
    def _():
        run(C)

    @pl.when(qi * tq >= C)
    def _():
        run(k_ref.shape[0])


def _diff_attention(qkv, da_lam, da_subln, *, C, lam_init):
    B, Lc, _ = qkv.shape
    H = DA_HEADS
    tq = _tile(C, 256)
    kern = functools.partial(_diff_attn_kernel, C=C, tq=tq, lam_init=lam_init)
    return pl.pallas_call(
        kern,
        grid=(B, H, Lc // tq),
        in_specs=[pl.BlockSpec((None, tq, LANE), lambda b, h, i: (b, i, h)),
                  pl.BlockSpec((None, Lc, LANE), lambda b, h, i: (b, 0, H + h)),
                  pl.BlockSpec((None, Lc, LANE), lambda b, h, i: (b, 0, 2 * H + h)),
                  pl.BlockSpec((4, DA_HEAD_DIM), lambda b, h, i: (0, 0)),
                  pl.BlockSpec((1, DA_V_DIM), lambda b, h, i: (0, 0))],
        out_specs=pl.BlockSpec((None, tq, LANE), lambda b, h, i: (b, i, h)),
        out_shape=jax.ShapeDtypeStruct((B, Lc, H * DA_V_DIM), bf16),
        scratch_shapes=[pltpu.VMEM((DA_V_DIM, Lc), bf16)],
        compiler_params=_cp("parallel", "parallel", "arbitrary"),
        name="diff_attention",
    )(qkv, qkv, qkv, da_lam, da_subln.reshape(1, DA_V_DIM))


def _mla_attn_kernel(qn_ref, qr_ref, kn_ref, kr_ref, v_ref, o_ref, vt_scr, *, C, tq, scale):
    qi = pl.program_id(2)

    @pl.when(qi == 0)
    def _():
        vt_scr[...] = _transpose_bf16(v_ref[...])

    def run(nkeys):
        q_t = jnp.concatenate([_transpose_bf16(qn_ref[...]), _transpose_bf16(qr_ref[...])], axis=0)
        k = jnp.concatenate([kn_ref[0:nkeys, :], kr_ref[0:nkeys, :]], axis=1)
        o_ref[...] = _attend_t([k], [q_t], vt_scr, nkeys, scale)[0].T.astype(o_ref.dtype)

    @pl.when(qi * tq < C)
    def _():
        run(C)

    @pl.when(qi * tq >= C)
    def _():
        run(kn_ref.shape[0])


def _mla_attention(q, kv, kr, *, C):
    B, Lc, _ = q.shape
    H = MLA_HEADS
    tq = _tile(C, 256)
    kern = functools.partial(_mla_attn_kernel, C=C, tq=tq, scale=(MLA_NOPE + MLA_ROPE) ** -0.5)
    return pl.pallas_call(
        kern,
        grid=(B, H, Lc // tq),
        in_specs=[pl.BlockSpec((None, tq, LANE), lambda b, h, i: (b, i, h)),
                  pl.BlockSpec((None, tq, LANE), lambda b, h, i: (b, i, H + h)),
                  pl.BlockSpec((None, Lc, LANE), lambda b, h, i: (b, 0, h)),
                  pl.BlockSpec((None, Lc, LANE), lambda b, h, i: (b, 0, 0)),
                  pl.BlockSpec((None, Lc, LANE), lambda b, h, i: (b, 0, H + h))],
        out_specs=pl.BlockSpec((None, tq, LANE), lambda b, h, i: (b, i, h)),
        out_shape=jax.ShapeDtypeStruct((B, Lc, H * MLA_V), bf16),
        scratch_shapes=[pltpu.VMEM((MLA_V, Lc), bf16)],
        compiler_params=_cp("parallel", "parallel", "arbitrary"),
        name="mla_attention",
    )(q, q, kv, kr, kv)


def _rms(x, w, eps=1e-6):
    return x * lax.rsqrt(jnp.mean(x * x, axis=1, keepdims=True) + eps) * w


def _mla_up_kernel(cq_ref, ckv_ref, kr_ref, qn_ref, kvn_ref, wq_ref, wkv_ref, cos_ref, sin_ref,
                   q_ref, kv_ref, kro_ref):
    nq = MLA_HEADS * MLA_NOPE
    cos = cos_ref[...]
    sin = sin_ref[...]
    q = jnp.dot(_rms(cq_ref[...], qn_ref[...]).astype(bf16), wq_ref[...], preferred_element_type=f32)
    q_ref[:, 0:nq] = q[:, 0:nq].astype(q_ref.dtype)
    for h in range(MLA_HEADS):
        c0 = nq + h * LANE
        q_ref[:, c0:c0 + LANE] = _rope(q[:, c0:c0 + LANE], cos, sin).astype(q_ref.dtype)
    kv = jnp.dot(_rms(ckv_ref[...], kvn_ref[...]).astype(bf16), wkv_ref[...], preferred_element_type=f32)
    kv_ref[...] = kv.astype(kv_ref.dtype)
    kro_ref[...] = _rope(kr_ref[...], cos, sin).astype(kro_ref.dtype)


def _mla_up(p, q_norm, kv_norm, wq, wkv, cos, sin):
    B, Lc, _ = p.shape
    tm = _tile(Lc, 544)
    nq = wq.shape[1]
    nkv = wkv.shape[1]
    return pl.pallas_call(
        _mla_up_kernel,
        grid=(B, Lc // tm),
        in_specs=[pl.BlockSpec((None, tm, MLA_Q_RANK), lambda b, i: (b, i, O_CQ // MLA_Q_RANK)),
                  pl.BlockSpec((None, tm, MLA_KV_RANK), lambda b, i: (b, i, O_CKV // MLA_KV_RANK)),
                  pl.BlockSpec((None, tm, LANE), lambda b, i: (b, i, O_KR // LANE)),
                  pl.BlockSpec((1, MLA_Q_RANK), lambda b, i: (0, 0)),
                  pl.BlockSpec((1, MLA_KV_RANK), lambda b, i: (0, 0)),
                  pl.BlockSpec((MLA_Q_RANK, nq), lambda b, i: (0, 0)),
                  pl.BlockSpec((MLA_KV_RANK, nkv), lambda b, i: (0, 0)),
                  pl.BlockSpec((tm, LANE), lambda b, i: (i, 0)),
                  pl.BlockSpec((tm, LANE), lambda b, i: (i, 0))],
        out_specs=[pl.BlockSpec((None, tm, nq), lambda b, i: (b, i, 0)),
                   pl.BlockSpec((None, tm, nkv), lambda b, i: (b, i, 0)),
                   pl.BlockSpec((None, tm, LANE), lambda b, i: (b, i, 0))],
        out_shape=[jax.ShapeDtypeStruct((B, Lc, nq), bf16),
                   jax.ShapeDtypeStruct((B, Lc, nkv), bf16),
                   jax.ShapeDtypeStruct((B, Lc, LANE), bf16)],
        compiler_params=_cp("parallel", "parallel"),
        name="mla_up",
    )(p, p, p, q_norm.reshape(1, -1), kv_norm.reshape(1, -1), wq, wkv, cos, sin)


def _s5_finish_kernel(y_ref, u_ref, d_ref, w_ref, b_ref, o_ref):
    y = y_ref[...] + d_ref[...] * u_ref[...]
    y = 0.5 * y * (1.0 + jnp.tanh(math.sqrt(2.0 / math.pi) * (y + 0.044715 * (y * y * y))))
    z = jnp.dot(y.astype(bf16), w_ref[...], preferred_element_type=f32) + b_ref[...]
    o_ref[...] = (y * _sigmoid(z)).astype(o_ref.dtype)


def _s5_finish(y, u, d_skip, glu_w, glu_b):
    B, Lc, N = u.shape
    tm = _tile(Lc, 1088)
    row = lambda b, i: (b, i, 0)
    const = lambda b, i: (0, 0)
    return pl.pallas_call(
        _s5_finish_kernel,
        grid=(B, Lc // tm),
        in_specs=[pl.BlockSpec((None, tm, N), row), pl.BlockSpec((None, tm, N), row),
                  pl.BlockSpec((1, N), const), pl.BlockSpec((N, N), const), pl.BlockSpec((1, N), const)],
        out_specs=pl.BlockSpec((None, tm, N), row),
        out_shape=jax.ShapeDtypeStruct((B, Lc, N), bf16),
        compiler_params=_cp("parallel", "parallel"),
        name="s5_finish",
    )(y, u, d_skip.reshape(1, N), glu_w.astype(bf16), glu_b.reshape(1, N))


def _gdn_finish_kernel(of_ref, ob_ref, z_ref, w_ref, o_ref):
    o = of_ref[...] + ob_ref[...]
    z = z_ref[...]
    w = w_ref[...]
    for h in range(GDN_HEADS):
        sl = slice(h * GDN_DV, (h + 1) * GDN_DV)
        zz = z[:, sl]
        o_ref[:, sl] = (_rms(o[:, sl], w) * (zz * _sigmoid(zz))).astype(o_ref.dtype)


def _gdn_finish(of, ob, p, norm_w):
    B, Lc, N = of.shape
    tm = _tile(Lc, 1088)
    row = lambda b, i: (b, i, 0)
    return pl.pallas_call(
        _gdn_finish_kernel,
        grid=(B, Lc // tm),
        in_specs=[pl.BlockSpec((None, tm, N), row), pl.BlockSpec((None, tm, N), row),
                  pl.BlockSpec((None, tm, N), lambda b, i: (b, i, O_Z // N)),
                  pl.BlockSpec((1, GDN_DV), lambda b, i: (0, 0))],
        out_specs=pl.BlockSpec((None, tm, N), row),
        out_shape=jax.ShapeDtypeStruct((B, Lc, N), bf16),
        compiler_params=_cp("parallel", "parallel"),
        name="gdn_finish",
    )(of, ob, p, norm_w.reshape(1, GDN_DV))


def _layer_norm(y, g, b, eps=1e-5):
    mu = jnp.mean(y, axis=1, keepdims=True)
    d = y - mu
    var = jnp.mean(d * d, axis=1, keepdims=True)
    return d * lax.rsqrt(var + eps) * g + b


def _merge_kernel(x_ref, a1_ref, a2_ref, w1_ref, w2_ref, ml_ref, mc_ref, g_ref, b_ref, rh_ref, rl_ref,
                  xo_ref, u_ref, lg_ref, *, C, tm, D):
    i = pl.program_id(1)
    o = (jnp.dot(a1_ref[...], w1_ref[...], preferred_element_type=f32)
         + jnp.dot(a2_ref[...], w2_ref[...], preferred_element_type=f32))
    gate, shift, scale = _select_mod(ml_ref[...], mc_ref[...], i * tm, tm, C, D, (2, 3, 4))
    xn = _layer_norm(DEEPNORM_ALPHA * x_ref[...] + gate * o, g_ref[...], b_ref[...])
    xo_ref[...] = xn
    u = xn * (1.0 + scale) + shift
    u_ref[...] = u
    uh = u.astype(bf16)
    ul = (u - uh.astype(f32)).astype(bf16)
    rh = rh_ref[...]
    lg_ref[...] = (jnp.dot(uh, rh, preferred_element_type=f32) + jnp.dot(ul, rh, preferred_element_type=f32)
                   + jnp.dot(uh, rl_ref[...], preferred_element_type=f32))


def _merge(x, a1, a2, w1, w2, mods, ln_g, ln_b, r_hi, r_lo, *, C):
    B, Lc, D = x.shape
    K1, K2 = a1.shape[2], a2.shape[2]
    tm = _tile(Lc, 272)
    nb = mods.shape[0] - 1
    row = lambda b, i: (b, i, 0)
    const = lambda b, i: (0, 0)
    kern = functools.partial(_merge_kernel, C=C, tm=tm, D=D)
    return pl.pallas_call(
        kern,
        grid=(B, Lc // tm),
        in_specs=[pl.BlockSpec((None, tm, D), row),
                  pl.BlockSpec((None, tm, K1), row),
                  pl.BlockSpec((None, tm, K2), row),
                  pl.BlockSpec((K1, D), const),
                  pl.BlockSpec((K2, D), const),
                  pl.BlockSpec((None, 1, 6 * D), lambda b, i: (b, 0, 0)),
                  pl.BlockSpec((None, 1, 6 * D), lambda b, i: (nb, 0, 0)),
                  pl.BlockSpec((1, D), const),
                  pl.BlockSpec((1, D), const),
                  pl.BlockSpec((D, LANE), const),
                  pl.BlockSpec((D, LANE), const)],
        out_specs=[pl.BlockSpec((None, tm, D), row),
                   pl.BlockSpec((None, tm, D), row),
                   pl.BlockSpec((None, tm, LANE), row)],
        out_shape=[jax.ShapeDtypeStruct((B, Lc, D), f32),
                   jax.ShapeDtypeStruct((B, Lc, D), f32),
                   jax.ShapeDtypeStruct((B, Lc, LANE), f32)],
        compiler_params=_cp("parallel", "parallel"),
        name="merge",
    )(x, a1, a2, w1, w2, mods, mods, ln_g.reshape(1, D), ln_b.reshape(1, D), r_hi, r_lo)


def _router_kernel(lg_ref, bias_ref, o_ref):
    lg = lg_ref[...]
    shape = lg.shape
    lane = lax.broadcasted_iota(jnp.int32, shape, 1)
    valid = lane < N_EXPERTS
    neg = -jnp.inf
    scores = _sigmoid(lg)
    biased = jnp.where(valid, scores + bias_ref[...], neg)

    def first_argmax(v):
        m = jnp.max(v, axis=1, keepdims=True)
        idx = jnp.min(jnp.where(v == m, lane, LANE), axis=1, keepdims=True)
        return m, idx

    best_score = None
    best_group = None
    for g in range(N_GROUPS):
        in_g = jnp.logical_and(lane >= g * EXPERTS_PER_GROUP, lane < (g + 1) * EXPERTS_PER_GROUP)
        vals = jnp.where(in_g, biased, neg)
        m1, i1 = first_argmax(vals)
        m2, _ = first_argmax(jnp.where(lane == i1, neg, vals))
        gs = m1 + m2
        if g == 0:
            best_score, best_group = gs, jnp.zeros_like(i1)
        else:
            better = gs > best_score
            best_score = jnp.where(better, gs, best_score)
            best_group = jnp.where(better, g, best_group)
    in_best = jnp.logical_and(lane >= best_group * EXPERTS_PER_GROUP, lane < (best_group + 1) * EXPERTS_PER_GROUP)
    vals = jnp.where(in_best, biased, neg)
    _, e0 = first_argmax(vals)
    _, e1 = first_argmax(jnp.where(lane == e0, neg, vals))
    w0 = jnp.sum(jnp.where(lane == e0, scores, 0.0), axis=1, keepdims=True)
    w1 = jnp.sum(jnp.where(lane == e1, scores, 0.0), axis=1, keepdims=True)
    tot = w0 + w1
    out = jnp.where(lane == 0, e0.astype(f32), jnp.where(lane == 1, e1.astype(f32),
                    jnp.where(lane == 2, w0 / tot, jnp.where(lane == 3, w1 / tot, 0.0))))
    o_ref[...] = out


def _router(logits, bias_row):
    T = logits.shape[0]
    tm = _tile(T, 1088, 8)
    return pl.pallas_call(
        _router_kernel,
        grid=(T // tm,),
        in_specs=[pl.BlockSpec((tm, LANE), lambda i: (i, 0)), pl.BlockSpec((1, LANE), lambda i: (0, 0))],
        out_specs=pl.BlockSpec((tm, LANE), lambda i: (i, 0)),
        out_shape=jax.ShapeDtypeStruct((T, LANE), f32),
        compiler_params=_cp("parallel"),
        name="router",
    )(logits, bias_row)


MOE_CODE_BITS = 16


def _moe_kernel(te_ref, na_ref, code_ref, u_hbm, wg_ref, wu_ref, wd_ref, y_hbm, xbuf, ybuf, gsem, ssem, *, tm):
    i = pl.program_id(0)
    na = na_ref[0]
    slot = i % 2

    def gather_row(tile, sl, r):
        tok = code_ref[tile * tm + r] >> MOE_CODE_BITS
        pltpu.make_async_copy(u_hbm.at[pl.ds(tok, 1), :], xbuf.at[sl, pl.ds(r, 1), :], gsem.at[sl]).start()

    def wait_gather(sl):
        pltpu.make_async_copy(u_hbm.at[pl.ds(0, tm), :], xbuf.at[sl], gsem.at[sl]).wait()

    def wait_scatter(sl):
        pltpu.make_async_copy(ybuf.at[sl], y_hbm.at[pl.ds(0, tm), :], ssem.at[sl]).wait()

    @pl.when(i < na)
    def _():
        @pl.when(i == 0)
        def _():
            lax.fori_loop(0, tm, lambda r, c: (gather_row(0, 0, r), c)[1], 0)
            ybuf[1] = jnp.zeros(ybuf.shape[1:], f32)
            spare = pltpu.make_async_copy(ybuf.at[1], y_hbm.at[pl.ds(y_hbm.shape[0] - tm, tm), :], ssem.at[1])
            spare.start()
            spare.wait()

        wait_gather(slot)

        @pl.when(i >= 2)
        def _():
            wait_scatter(slot)

        nxt = jnp.minimum(i + 1, na - 1)
        for r in range(tm):
            gather_row(nxt, 1 - slot, r)
        x = xbuf[slot].astype(bf16)
        g = jnp.dot(x, wg_ref[...], preferred_element_type=f32)
        u = jnp.dot(x, wu_ref[...], preferred_element_type=f32)
        h = (g * _sigmoid(g) * u).astype(bf16)
        ybuf[slot] = jnp.dot(h, wd_ref[...], preferred_element_type=f32)
        for r in range(tm):
            dst = code_ref[i * tm + r] & ((1 << MOE_CODE_BITS) - 1)
            pltpu.make_async_copy(ybuf.at[slot, pl.ds(r, 1), :], y_hbm.at[pl.ds(dst, 1), :], ssem.at[slot]).start()

        @pl.when(i == na - 1)
        def _():
            wait_gather(1 - slot)
            wait_scatter(slot)

            @pl.when(na >= 2)
            def _():
                wait_scatter(1 - slot)


def _moe_experts(u, code, tile_expert, n_active, wg, wu, wd, n_out_rows):
    T, D = u.shape
    F = wg.shape[2]
    tm = MOE_TM
    n_tiles = code.shape[0] // tm
    grid_spec = pltpu.PrefetchScalarGridSpec(
        num_scalar_prefetch=3,
        grid=(n_tiles,),
        in_specs=[pl.BlockSpec(memory_space=pl.ANY),
                  pl.BlockSpec((None, D, F), lambda i, te, na, cd: (te[i], 0, 0)),
                  pl.BlockSpec((None, D, F), lambda i, te, na, cd: (te[i], 0, 0)),
                  pl.BlockSpec((None, F, D), lambda i, te, na, cd: (te[i], 0, 0))],
        out_specs=pl.BlockSpec(memory_space=pl.ANY),
        scratch_shapes=[pltpu.VMEM((2, tm, D), f32), pltpu.VMEM((2, tm, D), f32),
                        pltpu.SemaphoreType.DMA((2,)), pltpu.SemaphoreType.DMA((2,))],
    )
    return pl.pallas_call(
        functools.partial(_moe_kernel, tm=tm),
        grid_spec=grid_spec,
        out_shape=jax.ShapeDtypeStruct((n_out_rows, D), f32),
        compiler_params=_cp("arbitrary"),
        name="moe_experts",
    )(tile_expert, n_active, code, u, wg, wu, wd)


def _final_ln_kernel(x_ref, y0_ref, y1_ref, r_ref, ml_ref, mc_ref, g_ref, b_ref, o_ref, *, C, tm, D):
    i = pl.program_id(1)
    (gate,) = _select_mod(ml_ref[...], mc_ref[...], i * tm, tm, C, D, (5,))
    r = r_ref[...]
    f = r[:, 2:3] * y0_ref[...] + r[:, 3:4] * y1_ref[...]
    o_ref[...] = _layer_norm(DEEPNORM_ALPHA * x_ref[...] + gate * f, g_ref[...], b_ref[...])


def _final_ln(x, y2, r, mods, ln_g, ln_b, *, C):
    B, Lc, D = x.shape
    tm = _tile(Lc, 544)
    nI = Lc // tm
    nb = mods.shape[0] - 1
    row = lambda b, i: (b, i, 0)
    kern = functools.partial(_final_ln_kernel, C=C, tm=tm, D=D)
    return pl.pallas_call(
        kern,
        grid=(B, nI),
        in_specs=[pl.BlockSpec((None, tm, D), row),
                  pl.BlockSpec((tm, D), lambda b, i: (b * nI + i, 0)),
                  pl.BlockSpec((tm, D), lambda b, i: (B * nI + b * nI + i, 0)),
                  pl.BlockSpec((tm, LANE), lambda b, i: (b * nI + i, 0)),
                  pl.BlockSpec((None, 1, 6 * D), lambda b, i: (b, 0, 0)),
                  pl.BlockSpec((None, 1, 6 * D), lambda b, i: (nb, 0, 0)),
                  pl.BlockSpec((1, D), lambda b, i: (0, 0)), pl.BlockSpec((1, D), lambda b, i: (0, 0))],
        out_specs=pl.BlockSpec((None, tm, D), row),
        out_shape=jax.ShapeDtypeStruct((B, Lc, D), f32),
        compiler_params=_cp("parallel", "parallel"),
        name="final_ln",
    )(x, y2, y2, r, mods, mods, ln_g.reshape(1, D), ln_b.reshape(1, D))


def _seq_reverse(t, C):
    return jnp.concatenate([jnp.flip(t[:, :C], axis=1), jnp.flip(t[:, C:], axis=1)], axis=1)


S5_PAIRS = S5_GROUPS // 2
S5_ROWS_PER_BLOCK = 8


def _s5_pair_tables(lam_re, lam_im, b_re, b_im, c_re, c_im, log_dt):
    T1, G, P, J = S5_BLOCK, S5_GROUPS, S5_STATE, S5_GROUP
    n = T1 * J
    t_idx = jnp.arange(T1)
    lag = t_idx[None, :] - t_idx[:, None]
    toep = 0.0
    bc_secs, cc_secs, a_secs = [], [], []
    for d in range(2):
        tb = _s5_tables_dir(lam_re[d], lam_im[d], b_re[d], b_im[d], c_re[d], c_im[d], log_dt[d])
        kern, pw, b_bar, c = tb
        use = (lag >= 0) if d == 0 else (lag <= 0)
        kk = jnp.transpose(kern[jnp.clip(jnp.abs(lag), 0, T1 - 1)], (2, 0, 4, 1, 3))
        toep = toep + jnp.where(use[None, :, None, :, None], kk, 0.0).reshape(G, n, n)
        p_in = pw[T1 - 1 - t_idx] if d == 0 else pw[t_idx]
        bc = jnp.transpose(p_in[:, :, :, None] * b_bar[None], (1, 0, 3, 2)).reshape(G, n, P)
        p_out = pw[t_idx + 1] if d == 0 else pw[T1 - t_idx]
        cc = jnp.transpose(c[None] * p_out[:, :, None, :], (1, 3, 0, 2)).reshape(G, P, n)
        bc_secs += [bc.real, bc.imag]
        cc_secs += [cc.real, -cc.imag]
        a_secs += [pw[T1].real, pw[T1].imag]
    eye2 = jnp.eye(2, dtype=f32)
    bc = jnp.stack(bc_secs, axis=2).reshape(S5_PAIRS, 2, n, 4, P)
    bc = (bc[:, :, :, :, None, :] * eye2[None, :, None, None, :, None]).reshape(S5_PAIRS, 2 * n, 4 * 2 * P)
    cc = jnp.stack(cc_secs, axis=1).reshape(S5_PAIRS, 2, 4, P, n)
    cc = jnp.transpose(cc, (0, 2, 1, 3, 4))
    cc = (cc[:, :, :, :, None, :] * eye2[None, None, :, None, :, None]).reshape(S5_PAIRS, 4 * 2 * P, 2 * n)
    a = jnp.transpose(jnp.stack(a_secs, axis=1).reshape(S5_PAIRS, 2, 4, P), (0, 2, 1, 3)).reshape(S5_PAIRS, 1, 4 * 2 * P)
    return toep.reshape(S5_PAIRS, 2, n, n).astype(bf16), bc.astype(bf16), cc.astype(bf16), a


def _s5_tables_dir(lam_re, lam_im, b_re, b_im, c_re, c_im, log_dt):
    T1 = S5_BLOCK
    lam = lax.complex(lam_re.astype(f32), lam_im.astype(f32))
    lam_dt = lam * jnp.exp(log_dt.astype(f32))[:, None]
    lam_bar = jnp.exp(lam_dt)
    b_bar = ((lam_bar - 1.0) / lam)[..., None] * lax.complex(b_re.astype(f32), b_im.astype(f32))
    c = lax.complex(c_re.astype(f32), c_im.astype(f32))
    pw = jnp.exp(lam_dt[None] * jnp.arange(T1 + 1, dtype=f32)[:, None, None])
    kern = jnp.einsum('gip,tgp,gpj->tgij', c, pw[:T1], b_bar, precision=lax.Precision.HIGHEST).real
    return kern, pw, b_bar, c


def _s5_kernel(u_ref, toep_ref, bc_ref, cc_ref, a_ref, y_ref, gin_scr, hp_scr, *, nb, nb_ctx):
    RB = S5_ROWS_PER_BLOCK
    u = u_ref[...]
    gin_scr[...] = jnp.dot(u, bc_ref[...], preferred_element_type=f32)
    a = a_ref[...]
    afr, afi, abr, abi = (a[:, k * LANE:(k + 1) * LANE] for k in range(4))

    def body(c, carry):
        hfr, hfi, hbr, hbi = carry
        cb = jnp.where(c < nb_ctx, nb_ctx - 1 - c, nb - 1 - (c - nb_ctx))
        rf = pl.multiple_of(c * RB, RB)
        rb = pl.multiple_of(cb * RB, RB)
        hp_scr[pl.ds(rf, RB), 0:LANE] = hfr
        hp_scr[pl.ds(rf, RB), LANE:2 * LANE] = hfi
        hp_scr[pl.ds(rb, RB), 2 * LANE:3 * LANE] = hbr
        hp_scr[pl.ds(rb, RB), 3 * LANE:4 * LANE] = hbi
        gf = gin_scr[pl.ds(rf, RB), 0:2 * LANE]
        gb = gin_scr[pl.ds(rb, RB), 2 * LANE:4 * LANE]
        return (afr * hfr - afi * hfi + gf[:, :LANE], afr * hfi + afi * hfr + gf[:, LANE:],
                abr * hbr - abi * hbi + gb[:, :LANE], abr * hbi + abi * hbr + gb[:, LANE:])

    z = jnp.zeros((RB, LANE), f32)
    lax.fori_loop(0, nb, body, (z, z, z, z))
    n = S5_BLOCK * S5_GROUP
    y = jnp.dot(hp_scr[...].astype(bf16), cc_ref[...], preferred_element_type=f32)
    y_ref[:, 0:n] = y[:, 0:n] + jnp.dot(u[:, 0:n], toep_ref[0], preferred_element_type=f32)
    y_ref[:, n:2 * n] = y[:, n:2 * n] + jnp.dot(u[:, n:2 * n], toep_ref[1], preferred_element_type=f32)


def _s5_mixer(s, C, lam_re, lam_im, b_re, b_im, c_re, c_im, log_dt):
    B, Lc, _ = s.shape
    T1, J, RB = S5_BLOCK, S5_GROUP, S5_ROWS_PER_BLOCK
    assert B <= RB and C % T1 == 0 and Lc % T1 == 0
    nb = Lc // T1
    n = T1 * J
    toep, bc, cc, a = _s5_pair_tables(lam_re, lam_im, b_re, b_im, c_re, c_im, log_dt)
    u = s.astype(bf16).reshape(B, nb, T1, S5_PAIRS, 2, J).transpose(3, 1, 0, 4, 2, 5)
    u = jnp.pad(u, ((0, 0), (0, 0), (0, RB - B), (0, 0), (0, 0), (0, 0))).reshape(S5_PAIRS, nb * RB, 2 * n)
    R = nb * RB
    kern = functools.partial(_s5_kernel, nb=nb, nb_ctx=C // T1)
    y = pl.pallas_call(
        kern,
        grid=(S5_PAIRS,),
        in_specs=[pl.BlockSpec((None, R, 2 * n), lambda q: (q, 0, 0)),
                  pl.BlockSpec((None, 2, n, n), lambda q: (q, 0, 0, 0)),
                  pl.BlockSpec((None, 2 * n, 4 * LANE), lambda q: (q, 0, 0)),
                  pl.BlockSpec((None, 4 * LANE, 2 * n), lambda q: (q, 0, 0)),
                  pl.BlockSpec((None, 1, 4 * LANE), lambda q: (q, 0, 0))],
        out_specs=pl.BlockSpec((None, R, 2 * n), lambda q: (q, 0, 0)),
        out_shape=jax.ShapeDtypeStruct((S5_PAIRS, R, 2 * n), f32),
        scratch_shapes=[pltpu.VMEM((R, 4 * LANE), f32), pltpu.VMEM((R, 4 * LANE), f32)],
        compiler_params=_cp("parallel"),
        name="s5_scan",
    )(u, toep, bc, cc, a)
    y = y.reshape(S5_PAIRS, nb, RB, 2, T1, J)[:, :, :B].transpose(2, 1, 4, 0, 3, 5)
    return y.reshape(B, Lc, S5_CHANNELS)


def _dotT(a, b):
    return lax.dot_general(a.astype(bf16), b.astype(bf16), (((1,), (1,)), ((), ())), preferred_element_type=f32)


def _dot(a, b):
    return jnp.dot(a.astype(bf16), b.astype(bf16), preferred_element_type=f32)


def _split(a):
    hi = a.astype(bf16)
    return hi, (a - hi.astype(f32)).astype(bf16)


def _dot_split(a, b):
    n = a.shape[1]
    ah, al = _split(a)
    bh, bl = _split(b)
    lhs = jnp.concatenate([ah, al], axis=1)
    rhs = jnp.concatenate([jnp.concatenate([bh, bl], axis=1),
                           jnp.concatenate([bh, jnp.zeros_like(bl)], axis=1)], axis=0)
    out = jnp.dot(lhs, rhs, preferred_element_type=f32)
    return out[:, :n] + out[:, n:]


def _gdn_chunk_index(n, n_ctx, n_all, reverse):
    if not reverse:
        return n
    return jnp.where(n < n_ctx, n_ctx - 1 - n, n_all - 1 - (n - n_ctx))


def _gdn_kernel(x_ref, xp_ref, xn_ref, gt_ref, cw_ref, alog_ref, dtb_ref, o_ref, s_scr, *, C, Lc, Cn, reverse):
    n = pl.program_id(1)
    n_ctx, n_all = C // Cn, Lc // Cn
    ci = _gdn_chunk_index(n, n_ctx, n_all, reverse)
    H, dk, dv = GDN_HEADS, GDN_DK, GDN_DV
    d = 1 if reverse else 0

    @pl.when(n == 0)
    def _():
        s_scr[...] = jnp.zeros_like(s_scr)

    x = x_ref[...]
    loc = lax.broadcasted_iota(jnp.int32, (Cn, 1), 0)
    row = ci * Cn + loc
    prev = jnp.where(loc == 0, xp_ref[7:8, :], pltpu.roll(x, 1, axis=0))
    prev = jnp.where(jnp.logical_or(row == 0, row == C), 0.0, prev)
    nxt = jnp.where(loc == Cn - 1, xn_ref[0:1, :], pltpu.roll(x, Cn - 1, axis=0))
    nxt = jnp.where(jnp.logical_or(row == C - 1, row == Lc - 1), 0.0, nxt)
    y = prev * cw_ref[0:1, :] + x * cw_ref[1:2, :] + nxt * cw_ref[2:3, :]
    y = y * _sigmoid(y)

    graw = gt_ref[...]
    z = graw + dtb_ref[...]
    g_all = -jnp.exp(alog_ref[...]) * (jnp.maximum(z, 0.0) + jnp.log(1.0 + jnp.exp(-jnp.abs(z))))
    beta_all = _sigmoid(graw)

    ii = lax.broadcasted_iota(jnp.int32, (Cn, Cn), 0)
    jj = lax.broadcasted_iota(jnp.int32, (Cn, Cn), 1)
    incl = (ii <= jj) if reverse else (ii >= jj)
    strict = (ii < jj) if reverse else (ii > jj)
    eye = (ii == jj).astype(f32)
    tri = incl.astype(bf16)
    g_hi = g_all.astype(bf16)
    g_lo = (g_all - g_hi.astype(f32)).astype(bf16)
    gc_all = jnp.dot(tri, g_hi, preferred_element_type=f32) + jnp.dot(tri, g_lo, preferred_element_type=f32)
    gc_all_t = gc_all.T
    last = 0 if reverse else Cn - 1

    heads = range(H)
    gc = [gc_all[:, 8 * d + h:8 * d + h + 1] for h in heads]
    gl = [gc_all[last:last + 1, 8 * d + h:8 * d + h + 1] for h in heads]
    beta = [beta_all[:, 16 + 8 * d + h:17 + 8 * d + h] for h in heads]
    decay = [jnp.where(incl, jnp.exp(jnp.where(incl, gc[h] - gc_all_t[8 * d + h:8 * d + h + 1, :], 0.0)), 0.0)
             for h in heads]
    l2 = lambda t: t * lax.rsqrt(jnp.sum(t * t, axis=1, keepdims=True) + 1e-6)
    q = [l2(y[:, h * dk:(h + 1) * dk]) * (dk ** -0.5) for h in heads]
    k = [l2(y[:, H * dk + h * dk:H * dk + (h + 1) * dk]) for h in heads]
    v = [y[:, 2 * H * dk + h * dv:2 * H * dk + (h + 1) * dv] for h in heads]
    kb = [k[h] * beta[h] for h in heads]
    kq = [_dotT(jnp.concatenate([kb[h], q[h]], axis=0), k[h]) for h in heads]
    m = [jnp.where(strict, kq[h][:Cn] * decay[h], 0.0) for h in heads]
    a = [jnp.where(incl, kq[h][Cn:] * decay[h], 0.0) for h in heads]
    t_inv = [eye - m[h] for h in heads]
    pw = m
    for _ in range(int(math.log2(Cn)) - 1):
        pw = [_dot_split(pw[h], pw[h]) for h in heads]
        t_inv = [t_inv[h] + _dot_split(t_inv[h], pw[h]) for h in heads]
    e_gc = [jnp.exp(gc[h]) for h in heads]
    uw = [_dot(t_inv[h], jnp.concatenate([v[h] * beta[h], kb[h] * e_gc[h]], axis=1)) for h in heads]
    s_old = [s_scr[h] for h in heads]
    r = [_dot(jnp.concatenate([uw[h][:, dv:], q[h] * e_gc[h]], axis=0), s_old[h]) for h in heads]
    v_new = [uw[h][:, :dv] - r[h][:Cn] for h in heads]
    for h in heads:
        o_ref[:, h * dv:(h + 1) * dv] = r[h][Cn:] + _dot(a[h], v_new[h])
    for h in heads:
        kd = k[h] * jnp.exp(gl[h] - gc[h])
        s_scr[h] = s_old[h] * jnp.exp(gl[h]) + lax.dot_general(
            kd.astype(bf16), v_new[h].astype(bf16), (((0,), (0,)), ((), ())), preferred_element_type=f32)


def _gdn_direction(p, conv_w, alog_row, dtb_row, *, C, reverse):
    B, Lc, _ = p.shape
    Cn = LANE
    n_ctx, n_all = C // Cn, Lc // Cn
    nq = GDN_QKV
    ci = lambda n: _gdn_chunk_index(n, n_ctx, n_all, reverse)
    sub = Cn // 8
    kern = functools.partial(_gdn_kernel, C=C, Lc=Lc, Cn=Cn, reverse=reverse)
    return pl.pallas_call(
        kern,
        grid=(B, n_all),
        in_specs=[pl.BlockSpec((None, Cn, nq), lambda b, n: (b, ci(n), 0)),
                  pl.BlockSpec((None, 8, nq), lambda b, n: (b, jnp.maximum(ci(n) * sub - 1, 0), 0)),
                  pl.BlockSpec((None, 8, nq), lambda b, n: (b, jnp.minimum((ci(n) + 1) * sub, Lc // 8 - 1), 0)),
                  pl.BlockSpec((None, Cn, LANE), lambda b, n: (b, ci(n), O_GATE // LANE)),
                  pl.BlockSpec((3, nq), lambda b, n: (0, 0)),
                  pl.BlockSpec((1, LANE), lambda b, n: (0, 0)),
                  pl.BlockSpec((1, LANE), lambda b, n: (0, 0))],
        out_specs=pl.BlockSpec((None, Cn, GDN_HEADS * GDN_DV), lambda b, n: (b, ci(n), 0)),
        out_shape=jax.ShapeDtypeStruct((B, Lc, GDN_HEADS * GDN_DV), f32),
        scratch_shapes=[pltpu.VMEM((GDN_HEADS, GDN_DK, GDN_DV), f32)],
        compiler_params=_cp("parallel", "arbitrary"),
        name="gdn_rev" if reverse else "gdn_fwd",
    )(p, p, p, p, conv_w, alog_row, dtb_row)


def _gdn_pallas(p, C, conv_w, a_log, dt_bias):
    outs = []
    for d in range(2):
        pad = lambda t: jnp.zeros((1, LANE), f32).at[0, 8 * d:8 * d + GDN_HEADS].set(t[d].astype(f32))
        outs.append(_gdn_direction(p, conv_w, pad(a_log), pad(dt_bias), C=C, reverse=(d == 1)))
    return outs


def _dispatch_plan(e_idx, n_tiles):
    T = e_idx.shape[0]
    tm = MOE_TM
    assert 2 * T + tm <= (1 << MOE_CODE_BITS) and T < (1 << (31 - MOE_CODE_BITS))
    flat = e_idx.reshape(-1)
    onehot = (flat[:, None] == jnp.arange(N_EXPERTS)[None, :]).astype(jnp.int32)
    rank = jnp.sum((jnp.cumsum(onehot, axis=0) - onehot) * onehot, axis=1)
    counts = jnp.sum(onehot, axis=0)
    padded = ((counts + tm - 1) // tm) * tm
    ends = jnp.cumsum(padded)
    offs = ends - padded
    pos = offs[flat] + rank
    slot_id = jnp.arange(2 * T, dtype=jnp.int32)
    tok, k = slot_id // 2, slot_id % 2
    spare = 2 * T + jnp.arange(n_tiles * tm, dtype=jnp.int32) % tm
    code = spare.at[pos].set((tok << MOE_CODE_BITS) | (k * T + tok))
    tile_start = jnp.arange(n_tiles, dtype=jnp.int32) * tm
    tile_expert = jnp.minimum(jnp.sum((tile_start[:, None] >= ends[None, :]).astype(jnp.int32), axis=1), N_EXPERTS - 1)
    n_active = (ends[-1] // tm).astype(jnp.int32).reshape(1)
    last_e = tile_expert[jnp.maximum(n_active[0] - 1, 0)]
    tile_expert = jnp.where(tile_start < ends[-1], tile_expert, last_e).astype(jnp.int32)
    return code, tile_expert, n_active


def _moe(u, logits, bias_row, wg, wu, wd):
    B, Lc, D = u.shape
    T = B * Lc
    r = _router(logits.reshape(T, LANE), bias_row)
    n_tiles = (2 * T + MOE_TM - 1) // MOE_TM + N_EXPERTS
    code, tile_expert, n_active = _dispatch_plan(r[:, 0:2].astype(jnp.int32), n_tiles)
    return _moe_experts(u.reshape(T, D), code, tile_expert, n_active, wg, wu, wd, 2 * T + MOE_TM), r


def _rope_tables(L, C):
    rows = L // GRID_W
    t_row = jnp.repeat(jnp.arange(rows), GRID_W).astype(f32)
    t_col = jnp.tile(jnp.arange(GRID_W), rows).astype(f32)
    n_freq = ROPE_DIM // 4
    inv_freq = ROPE_BASE ** (-jnp.arange(n_freq, dtype=f32) / n_freq)
    ang = jnp.concatenate([t_row[:, None] * inv_freq, t_col[:, None] * inv_freq], axis=-1)
    cos = jnp.repeat(jnp.cos(ang), 2, axis=1)
    sin = jnp.repeat(jnp.sin(ang), 2, axis=1) * jnp.tile(jnp.array([-1.0, 1.0], f32), ROPE_DIM // 2)
    one = jnp.ones((L, ROPE_DIM), f32)
    zero = jnp.zeros((L, ROPE_DIM), f32)
    ctx = lambda t, fill: jnp.concatenate([jnp.full((C, LANE), fill, f32), t], axis=0)
    return (ctx(jnp.concatenate([cos, cos], 1), 1.0), ctx(jnp.concatenate([sin, sin], 1), 0.0),
            ctx(jnp.concatenate([cos, one], 1), 1.0), ctx(jnp.concatenate([sin, zero], 1), 0.0))


def _prep_even_w_in(w):
    D = w.shape[0]
    nqk = 2 * DA_HEADS * DA_HEAD_DIM
    heads_first = lambda t: t.reshape(D, 2, DA_HEADS, DA_HEAD_DIM).transpose(0, 2, 1, 3).reshape(D, nqk)
    q = heads_first(w[:, :nqk]) * (DA_HEAD_DIM ** -0.5)
    k = heads_first(w[:, nqk:2 * nqk])
    return jnp.concatenate([q, k, w[:, 2 * nqk:]], axis=1).astype(bf16)


def _prep_odd_w_in(w):
    D = w.shape[0]
    z = lambda n: jnp.zeros((D, n), w.dtype)
    c0 = MLA_Q_RANK
    c1 = c0 + MLA_KV_RANK
    c2 = c1 + MLA_ROPE
    c3 = c2 + GDN_QKV
    c4 = c3 + GDN_HEADS * GDN_DV
    parts = [w[:, c2:c3], w[:, c3:c4], w[:, :c0], w[:, c0:c1], w[:, c1:c2], z(LANE - MLA_ROPE),
             w[:, c4:], z(LANE - 4 * GDN_HEADS)]
    out = jnp.concatenate(parts, axis=1)
    assert out.shape[1] == O_IN
    return out.astype(bf16)


def _prep_mla_w(w_uq, w_ukv):
    H = MLA_HEADS
    rq = w_uq.shape[0]
    wq = w_uq.reshape(rq, H, MLA_NOPE + MLA_ROPE)
    nope = wq[:, :, :MLA_NOPE].reshape(rq, H * MLA_NOPE)
    rope = jnp.concatenate([wq[:, :, MLA_NOPE:], jnp.zeros((rq, H, LANE - MLA_ROPE), w_uq.dtype)], axis=2)
    wq_p = jnp.concatenate([nope, rope.reshape(rq, H * LANE)], axis=1).astype(bf16)
    rkv = w_ukv.shape[0]
    wkv = w_ukv.reshape(rkv, H, MLA_NOPE + MLA_V)
    wkv_p = jnp.concatenate([wkv[:, :, :MLA_NOPE].reshape(rkv, H * MLA_NOPE),
                             wkv[:, :, MLA_NOPE:].reshape(rkv, H * MLA_V)], axis=1).astype(bf16)
    return wq_p, wkv_p


def kernel(x, c, ctx, c_ctx, mod_w, mod_b, ln_g, ln_b, e_w_in, e_w_out, da_lam, da_subln, s5_lam_re, s5_lam_im, s5_b_re, s5_b_im, s5_c_re, s5_c_im, s5_log_dt, s5_d, s5_glu_w, s5_glu_b, o_w_in, o_w_out, mla_q_norm, mla_kv_norm, mla_w_uq, mla_w_ukv, gdn_conv, gdn_a_log, gdn_dt_bias, gdn_norm, router_w, router_bias, moe_w_gate, moe_w_up, moe_w_down):
    B, L, D = x.shape
    C = ctx.shape[1]
    depth = mod_w.shape[0]
    assert depth == DEPTH

    R = -(-(B + 1) // 8) * 8
    cc = jnp.concatenate([c, c_ctx[None, :], jnp.zeros((R - B - 1, D), f32)], axis=0)
    mods_all = _modulation(cc, mod_w, mod_b)[:, :B + 1].reshape(depth, B + 1, 1, 6 * D)

    cos_e, sin_e, cos_o, sin_o = _rope_tables(L, C)
    r_pad = jnp.concatenate([router_w.astype(f32), jnp.zeros((D, LANE - N_EXPERTS), f32)], axis=1)
    r_hi = r_pad.astype(bf16)
    r_lo = (r_pad - r_hi.astype(f32)).astype(bf16)
    bias_row = jnp.concatenate([router_bias.astype(f32), jnp.zeros((LANE - N_EXPERTS,), f32)]).reshape(1, LANE)

    xs = jnp.concatenate([ctx, x], axis=1)
    n_att = DA_HEADS * DA_V_DIM
    for l in range(depth):
        mods = mods_all[l]
        i = l // 2
        if l % 2 == 0:
            qkv, s = _inproj(xs, mods, _prep_even_w_in(e_w_in[i]), cos_e, sin_e, C=C,
                             n_rope_cols=4 * DA_HEADS * DA_HEAD_DIM, n_a_cols=3072, a_dtype=bf16, b_dtype=f32)
            a1 = _diff_attention(qkv, da_lam[i].astype(f32), da_subln[i].astype(f32), C=C,
                                 lam_init=0.8 - 0.6 * math.exp(-0.3 * l))
            y = _s5_mixer(s, C, s5_lam_re[i], s5_lam_im[i], s5_b_re[i], s5_b_im[i], s5_c_re[i], s5_c_im[i],
                          s5_log_dt[i])
            a2 = _s5_finish(y, s, s5_d[i], s5_glu_w[i], s5_glu_b[i])
            w_out = e_w_out[i].astype(bf16)
        else:
            (p,) = _inproj(xs, mods, _prep_odd_w_in(o_w_in[i]), cos_e, sin_e, C=C,
                           n_rope_cols=0, n_a_cols=O_IN, a_dtype=f32, b_dtype=f32)
            wq_p, wkv_p = _prep_mla_w(mla_w_uq[i], mla_w_ukv[i])
            q, kv, kr = _mla_up(p, mla_q_norm[i], mla_kv_norm[i], wq_p, wkv_p, cos_o, sin_o)
            a1 = _mla_attention(q, kv, kr, C=C)
            of, ob = _gdn_pallas(p, C, gdn_conv[i].astype(f32), gdn_a_log[i], gdn_dt_bias[i])
            a2 = _gdn_finish(of, ob, p, gdn_norm[i])
            w_out = o_w_out[i].astype(bf16)
        xs, u, logits = _merge(xs, a1, a2, w_out[:n_att], w_out[n_att:], mods, ln_g[l, 0], ln_b[l, 0],
                               r_hi, r_lo, C=C)
        y2, r = _moe(u, logits, bias_row, moe_w_gate[l].astype(bf16), moe_w_up[l].astype(bf16),
                     moe_w_down[l].astype(bf16))
        xs = _final_ln(xs, y2, r, mods, ln_g[l, 1], ln_b[l, 1], C=C)
    return xs[:, C:, :]
```

```python
import functools
import math

import jax
import jax.numpy as jnp
import numpy as np
from jax import lax
from jax.experimental import pallas as pl
from jax.experimental.pallas import tpu as pltpu

f32 = jnp.float32
bf16 = jnp.bfloat16

GRID_W = 64
ROPE_DIM = 64
ROPE_BASE = 10000.0
DA_HEADS = 8
DA_HEAD_DIM = 64
DA_V_DIM = 128
S5_CHANNELS = 512
S5_GROUP = 16
S5_GROUPS = 32
S5_STATE = 64
S5_BLOCK = 16
MLA_HEADS = 8
MLA_Q_RANK = 512
MLA_KV_RANK = 256
MLA_NOPE = 128
MLA_ROPE = 64
MLA_V = 128
GDN_HEADS = 8
GDN_DK = 128
GDN_DV = 128
GDN_QKV = 3072
GDN_CHUNK = 64
N_EXPERTS = 16
N_GROUPS = 4
EXPERTS_PER_GROUP = 4
DEPTH = 4
DEEPNORM_ALPHA = (2 * DEPTH) ** 0.25
LANE = 128
MOE_TM = 256
VMEM_LIMIT = 56 * 1024 * 1024

O_QKV, O_Z, O_CQ, O_CKV, O_KR, O_GATE, O_IN = 0, 3072, 4096, 4608, 4864, 4992, 5120


def _cp(*sem):
    return pltpu.CompilerParams(dimension_semantics=sem, vmem_limit_bytes=VMEM_LIMIT)


def _tile(n, cap, mult=16):
    best = None
    for t in range(mult, min(n, cap) + 1, mult):
        if n % t == 0:
            best = t
    assert best is not None, (n, cap, mult)
    return best


def _sigmoid(x):
    return 1.0 / (1.0 + jnp.exp(-x))


def _mod_kernel(a_ref, w_ref, b_ref, o_ref):
    a = a_ref[...]
    act = (a * _sigmoid(a)).astype(bf16)
    o_ref[...] = jnp.dot(act, w_ref[...].astype(bf16), preferred_element_type=f32) + b_ref[...]


def _modulation(cc, mod_w, mod_b):
    depth, D, N = mod_w.shape
    R = cc.shape[0]
    tn = _tile(N, 1024, LANE)
    return pl.pallas_call(
        _mod_kernel,
        grid=(depth, N // tn),
        in_specs=[pl.BlockSpec((R, D), lambda l, j: (0, 0)),
                  pl.BlockSpec((None, D, tn), lambda l, j: (l, 0, j)),
                  pl.BlockSpec((None, 1, tn), lambda l, j: (l, 0, j))],
        out_specs=pl.BlockSpec((None, R, tn), lambda l, j: (l, 0, j)),
        out_shape=jax.ShapeDtypeStruct((depth, R, N), f32),
        compiler_params=_cp("parallel", "parallel"),
        name="modulation",
    )(cc, mod_w, mod_b.reshape(depth, 1, N))


def _select_mod(ml, mc, row0, tm, C, D, chunks):
    is_ctx = (row0 + lax.broadcasted_iota(jnp.int32, (tm, 1), 0)) < C
    return [jnp.where(is_ctx, mc[:, k * D:(k + 1) * D], ml[:, k * D:(k + 1) * D]) for k in chunks]


def _rope(seg, cos, sin):
    nxt = pltpu.roll(seg, LANE - 1, axis=1)
    prv = pltpu.roll(seg, 1, axis=1)
    even = (lax.broadcasted_iota(jnp.int32, seg.shape, 1) % 2) == 0
    return seg * cos + jnp.where(even, nxt, prv) * sin


def _inproj_kernel(x_ref, ml_ref, mc_ref, w_ref, cos_ref, sin_ref, *rest, C, tm, tn, D, n_rope, n_a, has_b):
    if has_b:
        oa_ref, ob_ref, u_scr = rest
    else:
        oa_ref, u_scr = rest
        ob_ref = None
    i = pl.program_id(1)
    j = pl.program_id(2)

    @pl.when(j == 0)
    def _():
        shift, scale = _select_mod(ml_ref[...], mc_ref[...], i * tm, tm, C, D, (0, 1))
        u_scr[...] = (x_ref[...] * (1.0 + scale) + shift).astype(bf16)

    def acc():
        return jnp.dot(u_scr[...], w_ref[...], preferred_element_type=f32)

    if n_rope > 0:
        @pl.when(j < n_rope)
        def _():
            a = acc()
            cos = cos_ref[...]
            sin = sin_ref[...]
            for c in range(tn // LANE):
                oa_ref[:, c * LANE:(c + 1) * LANE] = _rope(a[:, c * LANE:(c + 1) * LANE], cos, sin).astype(oa_ref.dtype)

    @pl.when(jnp.logical_and(j >= n_rope, j < n_a))
    def _():
        oa_ref[...] = acc().astype(oa_ref.dtype)

    if has_b:
        @pl.when(j >= n_a)
        def _():
            ob_ref[...] = acc().astype(ob_ref.dtype)


def _inproj(x, mods, w, cos, sin, *, C, n_rope_cols, n_a_cols, a_dtype, b_dtype):
    B, Lc, D = x.shape
    N = w.shape[1]
    tm = _tile(Lc, 1088)
    tn = 512
    assert N % tn == 0 and n_rope_cols % tn == 0 and n_a_cols % tn == 0
    n_a = n_a_cols // tn
    has_b = n_a_cols < N
    kern = functools.partial(_inproj_kernel, C=C, tm=tm, tn=tn, D=D, n_rope=n_rope_cols // tn, n_a=n_a, has_b=has_b)
    out_shape = [jax.ShapeDtypeStruct((B, Lc, n_a_cols), a_dtype)]
    out_specs = [pl.BlockSpec((None, tm, tn), lambda b, i, j: (b, i, jnp.minimum(j, n_a - 1)))]
    if has_b:
        out_shape.append(jax.ShapeDtypeStruct((B, Lc, N - n_a_cols), b_dtype))
        out_specs.append(pl.BlockSpec((None, tm, tn), lambda b, i, j: (b, i, jnp.maximum(j - n_a, 0))))
    nb = mods.shape[0] - 1
    return pl.pallas_call(
        kern,
        grid=(B, Lc // tm, N // tn),
        in_specs=[pl.BlockSpec((None, tm, D), lambda b, i, j: (b, i, 0)),
                  pl.BlockSpec((None, 1, 6 * D), lambda b, i, j: (b, 0, 0)),
                  pl.BlockSpec((None, 1, 6 * D), lambda b, i, j: (nb, 0, 0)),
                  pl.BlockSpec((D, tn), lambda b, i, j: (0, j)),
                  pl.BlockSpec((tm, LANE), lambda b, i, j: (i, 0)),
                  pl.BlockSpec((tm, LANE), lambda b, i, j: (i, 0))],
        out_specs=out_specs,
        out_shape=out_shape,
        scratch_shapes=[pltpu.VMEM((tm, D), bf16)],
        compiler_params=_cp("parallel", "parallel", "arbitrary"),
        name="inproj",
    )(x, mods, mods, w, cos, sin)


def _transpose_bf16(x):
    return x.astype(f32).T.astype(bf16)


ATTN_KEY_CHUNK = 512


def _attend_t(ks, q_ts, vt_ref, nkeys, scale=None):
    n = len(ks)
    chunks = [(c0, min(c0 + ATTN_KEY_CHUNK, nkeys)) for c0 in range(0, nkeys, ATTN_KEY_CHUNK)]

    def scores(c):
        out = [jnp.dot(ks[i][c[0]:c[1]], q_ts[i], preferred_element_type=f32) for i in range(n)]
        return out if scale is None else [t * scale for t in out]

    m, l, acc = [None] * n, [None] * n, [None] * n
    s_next = scores(chunks[0])
    for j, (c0, c1) in enumerate(chunks):
        s_cur = s_next
        if j + 1 < len(chunks):
            s_next = scores(chunks[j + 1])
        v_c = vt_ref[:, c0:c1]
        for i in range(n):
            mc = jnp.max(s_cur[i], axis=0, keepdims=True)
            m_new = mc if m[i] is None else jnp.maximum(m[i], mc)
            p = jnp.exp(s_cur[i] - m_new)
            pv = jnp.dot(v_c, p.astype(bf16), preferred_element_type=f32)
            if m[i] is None:
                l[i], acc[i] = jnp.sum(p, axis=0, keepdims=True), pv
            else:
                alpha = jnp.exp(m[i] - m_new)
                l[i] = alpha * l[i] + jnp.sum(p, axis=0, keepdims=True)
                acc[i] = alpha * acc[i] + pv
            m[i] = m_new
    return [acc[i] / l[i] for i in range(n)]


def _diff_attn_kernel(q_ref, k_ref, v_ref, lam_ref, sub_ref, o_ref, vt_scr, *, C, tq, lam_init):
    qi = pl.program_id(2)

    @pl.when(qi == 0)
    def _():
        vt_scr[...] = _transpose_bf16(v_ref[...])

    lv = lam_ref[...]
    lam = (jnp.exp(jnp.sum(lv[0:1] * lv[1:2], axis=1, keepdims=True))
           - jnp.exp(jnp.sum(lv[2:3] * lv[3:4], axis=1, keepdims=True)) + lam_init)

    def run(nkeys):
        q_t = _transpose_bf16(q_ref[...])
        k = k_ref[0:nkeys, :]
        hd = DA_HEAD_DIM
        o1, o2 = _attend_t([k[:, 0:hd], k[:, hd:2 * hd]], [q_t[0:hd, :], q_t[hd:2 * hd, :]], vt_scr, nkeys)
        o = (o1 - lam * o2).T
        o = o * lax.rsqrt(jnp.mean(o * o, axis=1, keepdims=True) + 1e-6) * sub_ref[...]
        o_ref[...] = (o * (1.0 - lam_init)).astype(o_ref.dtype)

    @pl.when(qi * tq < C)
    def _():
        run(C)

    @pl.when(qi * tq >= C)
    def _():
        run(k_ref.shape[0])


def _diff_attention(qkv, da_lam, da_subln, *, C, lam_init):
    B, Lc, _ = qkv.shape
    H = DA_HEADS
    tq = _tile(C, 256)
    kern = functools.partial(_diff_attn_kernel, C=C, tq=tq, lam_init=lam_init)
    return pl.pallas_call(
        kern,
        grid=(B, H, Lc // tq),
        in_specs=[pl.BlockSpec((None, tq, LANE), lambda b, h, i: (b, i, h)),
                  pl.BlockSpec((None, Lc, LANE), lambda b, h, i: (b, 0, H + h)),
                  pl.BlockSpec((None, Lc, LANE), lambda b, h, i: (b, 0, 2 * H + h)),
                  pl.BlockSpec((4, DA_HEAD_DIM), lambda b, h, i: (0, 0)),
                  pl.BlockSpec((1, DA_V_DIM), lambda b, h, i: (0, 0))],
        out_specs=pl.BlockSpec((None, tq, LANE), lambda b, h, i: (b, i, h)),
        out_shape=jax.ShapeDtypeStruct((B, Lc, H * DA_V_DIM), bf16),
        scratch_shapes=[pltpu.VMEM((DA_V_DIM, Lc), bf16)],
        compiler_params=_cp("parallel", "parallel", "arbitrary"),
        name="diff_attention",
    )(qkv, qkv, qkv, da_lam, da_subln.reshape(1, DA_V_DIM))


def _mla_attn_kernel(qn_ref, qr_ref, kn_ref, kr_ref, v_ref, o_ref, vt_scr, *, C, tq, scale):
    qi = pl.program_id(2)

    @pl.when(qi == 0)
    def _():
        vt_scr[...] = _transpose_bf16(v_ref[...])

    def run(nkeys):
        q_t = jnp.concatenate([_transpose_bf16(qn_ref[...]), _transpose_bf16(qr_ref[...])], axis=0)
        k = jnp.concatenate([kn_ref[0:nkeys, :], kr_ref[0:nkeys, :]], axis=1)
        o_ref[...] = _attend_t([k], [q_t], vt_scr, nkeys, scale)[0].T.astype(o_ref.dtype)

    @pl.when(qi * tq < C)
    def _():
        run(C)

    @pl.when(qi * tq >= C)
    def _():
        run(kn_ref.shape[0])


def _mla_attention(q, kv, kr, *, C):
    B, Lc, _ = q.shape
    H = MLA_HEADS
    tq = _tile(C, 256)
    kern = functools.partial(_mla_attn_kernel, C=C, tq=tq, scale=(MLA_NOPE + MLA_ROPE) ** -0.5)
    return pl.pallas_call(
        kern,
        grid=(B, H, Lc // tq),
        in_specs=[pl.BlockSpec((None, tq, LANE), lambda b, h, i: (b, i, h)),
                  pl.BlockSpec((None, tq, LANE), lambda b, h, i: (b, i, H + h)),
                  pl.BlockSpec((None, Lc, LANE), lambda b, h, i: (b, 0, h)),
                  pl.BlockSpec((None, Lc, LANE), lambda b, h, i: (b, 0, 0)),
                  pl.BlockSpec((None, Lc, LANE), lambda b, h, i: (b, 0, H + h))],
        out_specs=pl.BlockSpec((None, tq, LANE), lambda b, h, i: (b, i, h)),
        out_shape=jax.ShapeDtypeStruct((B, Lc, H * MLA_V), bf16),
        scratch_shapes=[pltpu.VMEM((MLA_V, Lc), bf16)],
        compiler_params=_cp("parallel", "parallel", "arbitrary"),
        name="mla_attention",
    )(q, q, kv, kr, kv)


def _rms(x, w, eps=1e-6):
    return x * lax.rsqrt(jnp.mean(x * x, axis=1, keepdims=True) + eps) * w


def _mla_up_kernel(cq_ref, ckv_ref, kr_ref, qn_ref, kvn_ref, wq_ref, wkv_ref, cos_ref, sin_ref,
                   q_ref, kv_ref, kro_ref):
    nq = MLA_HEADS * MLA_NOPE
    cos = cos_ref[...]
    sin = sin_ref[...]
    q = jnp.dot(_rms(cq_ref[...], qn_ref[...]).astype(bf16), wq_ref[...], preferred_element_type=f32)
    q_ref[:, 0:nq] = q[:, 0:nq].astype(q_ref.dtype)
    for h in range(MLA_HEADS):
        c0 = nq + h * LANE
        q_ref[:, c0:c0 + LANE] = _rope(q[:, c0:c0 + LANE], cos, sin).astype(q_ref.dtype)
    kv = jnp.dot(_rms(ckv_ref[...], kvn_ref[...]).astype(bf16), wkv_ref[...], preferred_element_type=f32)
    kv_ref[...] = kv.astype(kv_ref.dtype)
    kro_ref[...] = _rope(kr_ref[...], cos, sin).astype(kro_ref.dtype)


def _mla_up(p, q_norm, kv_norm, wq, wkv, cos, sin):
    B, Lc, _ = p.shape
    tm = _tile(Lc, 544)
    nq = wq.shape[1]
    nkv = wkv.shape[1]
    return pl.pallas_call(
        _mla_up_kernel,
        grid=(B, Lc // tm),
        in_specs=[pl.BlockSpec((None, tm, MLA_Q_RANK), lambda b, i: (b, i, O_CQ // MLA_Q_RANK)),
                  pl.BlockSpec((None, tm, MLA_KV_RANK), lambda b, i: (b, i, O_CKV // MLA_KV_RANK)),
                  pl.BlockSpec((None, tm, LANE), lambda b, i: (b, i, O_KR // LANE)),
                  pl.BlockSpec((1, MLA_Q_RANK), lambda b, i: (0, 0)),
                  pl.BlockSpec((1, MLA_KV_RANK), lambda b, i: (0, 0)),
                  pl.BlockSpec((MLA_Q_RANK, nq), lambda b, i: (0, 0)),
                  pl.BlockSpec((MLA_KV_RANK, nkv), lambda b, i: (0, 0)),
                  pl.BlockSpec((tm, LANE), lambda b, i: (i, 0)),
                  pl.BlockSpec((tm, LANE), lambda b, i: (i, 0))],
        out_specs=[pl.BlockSpec((None, tm, nq), lambda b, i: (b, i, 0)),
                   pl.BlockSpec((None, tm, nkv), lambda b, i: (b, i, 0)),
                   pl.BlockSpec((None, tm, LANE), lambda b, i: (b, i, 0))],
        out_shape=[jax.ShapeDtypeStruct((B, Lc, nq), bf16),
                   jax.ShapeDtypeStruct((B, Lc, nkv), bf16),
                   jax.ShapeDtypeStruct((B, Lc, LANE), bf16)],
        compiler_params=_cp("parallel", "parallel"),
        name="mla_up",
    )(p, p, p, q_norm.reshape(1, -1), kv_norm.reshape(1, -1), wq, wkv, cos, sin)


def _s5_finish_kernel(y_ref, u_ref, d_ref, w_ref, b_ref, o_ref):
    y = y_ref[...] + d_ref[...] * u_ref[...]
    y = 0.5 * y * (1.0 + jnp.tanh(math.sqrt(2.0 / math.pi) * (y + 0.044715 * (y * y * y))))
    z = jnp.dot(y.astype(bf16), w_ref[...], preferred_element_type=f32) + b_ref[...]
    o_ref[...] = (y * _sigmoid(z)).astype(o_ref.dtype)


def _s5_finish(y, u, d_skip, glu_w, glu_b):
    B, Lc, N = u.shape
    tm = _tile(Lc, 1088)
    row = lambda b, i: (b, i, 0)
    const = lambda b, i: (0, 0)
    return pl.pallas_call(
        _s5_finish_kernel,
        grid=(B, Lc // tm),
        in_specs=[pl.BlockSpec((None, tm, N), row), pl.BlockSpec((None, tm, N), row),
                  pl.BlockSpec((1, N), const), pl.BlockSpec((N, N), const), pl.BlockSpec((1, N), const)],
        out_specs=pl.BlockSpec((None, tm, N), row),
        out_shape=jax.ShapeDtypeStruct((B, Lc, N), bf16),
        compiler_params=_cp("parallel", "parallel"),
        name="s5_finish",
    )(y, u, d_skip.reshape(1, N), glu_w.astype(bf16), glu_b.reshape(1, N))


def _gdn_finish_kernel(of_ref, ob_ref, z_ref, w_ref, o_ref):
    o = of_ref[...] + ob_ref[...]
    z = z_ref[...]
    w = w_ref[...]
    for h in range(GDN_HEADS):
        sl = slice(h * GDN_DV, (h + 1) * GDN_DV)
        zz = z[:, sl]
        o_ref[:, sl] = (_rms(o[:, sl], w) * (zz * _sigmoid(zz))).astype(o_ref.dtype)


def _gdn_finish(of, ob, p, norm_w):
    B, Lc, N = of.shape
    tm = _tile(Lc, 1088)
    row = lambda b, i: (b, i, 0)
    return pl.pallas_call(
        _gdn_finish_kernel,
        grid=(B, Lc // tm),
        in_specs=[pl.BlockSpec((None, tm, N), row), pl.BlockSpec((None, tm, N), row),
                  pl.BlockSpec((None, tm, N), lambda b, i: (b, i, O_Z // N)),
                  pl.BlockSpec((1, GDN_DV), lambda b, i: (0, 0))],
        out_specs=pl.BlockSpec((None, tm, N), row),
        out_shape=jax.ShapeDtypeStruct((B, Lc, N), bf16),
        compiler_params=_cp("parallel", "parallel"),
        name="gdn_finish",
    )(of, ob, p, norm_w.reshape(1, GDN_DV))


def _layer_norm(y, g, b, eps=1e-5):
    mu = jnp.mean(y, axis=1, keepdims=True)
    d = y - mu
    var = jnp.mean(d * d, axis=1, keepdims=True)
    return d * lax.rsqrt(var + eps) * g + b


def _merge_kernel(x_ref, a1_ref, a2_ref, w1_ref, w2_ref, ml_ref, mc_ref, g_ref, b_ref, rh_ref, rl_ref,
                  xo_ref, u_ref, lg_ref, *, C, tm, D):
    i = pl.program_id(1)
    o = (jnp.dot(a1_ref[...], w1_ref[...], preferred_element_type=f32)
         + jnp.dot(a2_ref[...], w2_ref[...], preferred_element_type=f32))
    gate, shift, scale = _select_mod(ml_ref[...], mc_ref[...], i * tm, tm, C, D, (2, 3, 4))
    xn = _layer_norm(DEEPNORM_ALPHA * x_ref[...] + gate * o, g_ref[...], b_ref[...])
    xo_ref[...] = xn
    u = xn * (1.0 + scale) + shift
    u_ref[...] = u
    uh = u.astype(bf16)
    ul = (u - uh.astype(f32)).astype(bf16)
    rh = rh_ref[...]
    lg_ref[...] = (jnp.dot(uh, rh, preferred_element_type=f32) + jnp.dot(ul, rh, preferred_element_type=f32)
                   + jnp.dot(uh, rl_ref[...], preferred_element_type=f32))


def _merge(x, a1, a2, w1, w2, mods, ln_g, ln_b, r_hi, r_lo, *, C):
    B, Lc, D = x.shape
    K1, K2 = a1.shape[2], a2.shape[2]
    tm = _tile(Lc, 272)
    nb = mods.shape[0] - 1
    row = lambda b, i: (b, i, 0)
    const = lambda b, i: (0, 0)
    kern = functools.partial(_merge_kernel, C=C, tm=tm, D=D)
    return pl.pallas_call(
        kern,
        grid=(B, Lc // tm),
        in_specs=[pl.BlockSpec((None, tm, D), row),
                  pl.BlockSpec((None, tm, K1), row),
                  pl.BlockSpec((None, tm, K2), row),
                  pl.BlockSpec((K1, D), const),
                  pl.BlockSpec((K2, D), const),
                  pl.BlockSpec((None, 1, 6 * D), lambda b, i: (b, 0, 0)),
                  pl.BlockSpec((None, 1, 6 * D), lambda b, i: (nb, 0, 0)),
                  pl.BlockSpec((1, D), const),
                  pl.BlockSpec((1, D), const),
                  pl.BlockSpec((D, LANE), const),
                  pl.BlockSpec((D, LANE), const)],
        out_specs=[pl.BlockSpec((None, tm, D), row),
                   pl.BlockSpec((None, tm, D), row),
                   pl.BlockSpec((None, tm, LANE), row)],
        out_shape=[jax.ShapeDtypeStruct((B, Lc, D), f32),
                   jax.ShapeDtypeStruct((B, Lc, D), f32),
                   jax.ShapeDtypeStruct((B, Lc, LANE), f32)],
        compiler_params=_cp("parallel", "parallel"),
        name="merge",
    )(x, a1, a2, w1, w2, mods, mods, ln_g.reshape(1, D), ln_b.reshape(1, D), r_hi, r_lo)


def _router_kernel(lg_ref, bias_ref, o_ref):
    lg = lg_ref[...]
    shape = lg.shape
    lane = lax.broadcasted_iota(jnp.int32, shape, 1)
    valid = lane < N_EXPERTS
    neg = -jnp.inf
    scores = _sigmoid(lg)
    biased = jnp.where(valid, scores + bias_ref[...], neg)

    def first_argmax(v):
        m = jnp.max(v, axis=1, keepdims=True)
        idx = jnp.min(jnp.where(v == m, lane, LANE), axis=1, keepdims=True)
        return m, idx

    best_score = None
    best_group = None
    for g in range(N_GROUPS):
        in_g = jnp.logical_and(lane >= g * EXPERTS_PER_GROUP, lane < (g + 1) * EXPERTS_PER_GROUP)
        vals = jnp.where(in_g, biased, neg)
        m1, i1 = first_argmax(vals)
        m2, _ = first_argmax(jnp.where(lane == i1, neg, vals))
        gs = m1 + m2
        if g == 0:
            best_score, best_group = gs, jnp.zeros_like(i1)
        else:
            better = gs > best_score
            best_score = jnp.where(better, gs, best_score)
            best_group = jnp.where(better, g, best_group)
    in_best = jnp.logical_and(lane >= best_group * EXPERTS_PER_GROUP, lane < (best_group + 1) * EXPERTS_PER_GROUP)
    vals = jnp.where(in_best, biased, neg)
    _, e0 = first_argmax(vals)
    _, e1 = first_argmax(jnp.where(lane == e0, neg, vals))
    w0 = jnp.sum(jnp.where(lane == e0, scores, 0.0), axis=1, keepdims=True)
    w1 = jnp.sum(jnp.where(lane == e1, scores, 0.0), axis=1, keepdims=True)
    tot = w0 + w1
    out = jnp.where(lane == 0, e0.astype(f32), jnp.where(lane == 1, e1.astype(f32),
                    jnp.where(lane == 2, w0 / tot, jnp.where(lane == 3, w1 / tot, 0.0))))
    o_ref[...] = out


def _router(logits, bias_row):
    T = logits.shape[0]
    tm = _tile(T, 1088, 8)
    return pl.pallas_call(
        _router_kernel,
        grid=(T // tm,),
        in_specs=[pl.BlockSpec((tm, LANE), lambda i: (i, 0)), pl.BlockSpec((1, LANE), lambda i: (0, 0))],
        out_specs=pl.BlockSpec((tm, LANE), lambda i: (i, 0)),
        out_shape=jax.ShapeDtypeStruct((T, LANE), f32),
        compiler_params=_cp("parallel"),
        name="router",
    )(logits, bias_row)


MOE_CODE_BITS = 16


def _moe_kernel(te_ref, na_ref, code_ref, u_hbm, wg_ref, wu_ref, wd_ref, y_hbm, xbuf, ybuf, gsem, ssem, *, tm):
    i = pl.program_id(0)
    na = na_ref[0]
    slot = i % 2

    def gather_row(tile, sl, r):
        tok = code_ref[tile * tm + r] >> MOE_CODE_BITS
        pltpu.make_async_copy(u_hbm.at[pl.ds(tok, 1), :], xbuf.at[sl, pl.ds(r, 1), :], gsem.at[sl]).start()

    def wait_gather(sl):
        pltpu.make_async_copy(u_hbm.at[pl.ds(0, tm), :], xbuf.at[sl], gsem.at[sl]).wait()

    def wait_scatter(sl):
        pltpu.make_async_copy(ybuf.at[sl], y_hbm.at[pl.ds(0, tm), :], ssem.at[sl]).wait()

    @pl.when(i < na)
    def _():
        @pl.when(i == 0)
        def _():
            lax.fori_loop(0, tm, lambda r, c: (gather_row(0, 0, r), c)[1], 0)
            ybuf[1] = jnp.zeros(ybuf.shape[1:], f32)
            spare = pltpu.make_async_copy(ybuf.at[1], y_hbm.at[pl.ds(y_hbm.shape[0] - tm, tm), :], ssem.at[1])
            spare.start()
            spare.wait()

        wait_gather(slot)

        @pl.when(i >= 2)
        def _():
            wait_scatter(slot)

        nxt = jnp.minimum(i + 1, na - 1)
        for r in range(tm):
            gather_row(nxt, 1 - slot, r)
        x = xbuf[slot].astype(bf16)
        g = jnp.dot(x, wg_ref[...], preferred_element_type=f32)
        u = jnp.dot(x, wu_ref[...], preferred_element_type=f32)
        h = (g * _sigmoid(g) * u).astype(bf16)
        ybuf[slot] = jnp.dot(h, wd_ref[...], preferred_element_type=f32)
        for r in range(tm):
            dst = code_ref[i * tm + r] & ((1 << MOE_CODE_BITS) - 1)
            pltpu.make_async_copy(ybuf.at[slot, pl.ds(r, 1), :], y_hbm.at[pl.ds(dst, 1), :], ssem.at[slot]).start()

        @pl.when(i == na - 1)
        def _():
            wait_gather(1 - slot)
            wait_scatter(slot)

            @pl.when(na >= 2)
            def _():
                wait_scatter(1 - slot)


def _moe_experts(u, code, tile_expert, n_active, wg, wu, wd, layer, n_out_rows):
    T, D = u.shape
    F = wg.shape[3]
    tm = MOE_TM
    n_tiles = code.shape[0] // tm
    grid_spec = pltpu.PrefetchScalarGridSpec(
        num_scalar_prefetch=3,
        grid=(n_tiles,),
        in_specs=[pl.BlockSpec(memory_space=pl.ANY),
                  pl.BlockSpec((None, None, D, F), lambda i, te, na, cd: (layer, te[i], 0, 0)),
                  pl.BlockSpec((None, None, D, F), lambda i, te, na, cd: (layer, te[i], 0, 0)),
                  pl.BlockSpec((None, None, F, D), lambda i, te, na, cd: (layer, te[i], 0, 0))],
        out_specs=pl.BlockSpec(memory_space=pl.ANY),
        scratch_shapes=[pltpu.VMEM((2, tm, D), f32), pltpu.VMEM((2, tm, D), f32),
                        pltpu.SemaphoreType.DMA((2,)), pltpu.SemaphoreType.DMA((2,))],
    )
    return pl.pallas_call(
        functools.partial(_moe_kernel, tm=tm),
        grid_spec=grid_spec,
        out_shape=jax.ShapeDtypeStruct((n_out_rows, D), f32),
        compiler_params=_cp("arbitrary"),
        name="moe_experts",
    )(tile_expert, n_active, code, u, wg, wu, wd)


def _final_ln_kernel(x_ref, y0_ref, y1_ref, r_ref, ml_ref, mc_ref, g_ref, b_ref, o_ref, *, C, tm, D):
    i = pl.program_id(1)
    (gate,) = _select_mod(ml_ref[...], mc_ref[...], i * tm, tm, C, D, (5,))
    r = r_ref[...]
    f = r[:, 2:3] * y0_ref[...] + r[:, 3:4] * y1_ref[...]
    o_ref[...] = _layer_norm(DEEPNORM_ALPHA * x_ref[...] + gate * f, g_ref[...], b_ref[...])


def _final_ln(x, y2, r, mods, ln_g, ln_b, *, C):
    B, Lc, D = x.shape
    tm = _tile(Lc, 544)
    nI = Lc // tm
    nb = mods.shape[0] - 1
    row = lambda b, i: (b, i, 0)
    kern = functools.partial(_final_ln_kernel, C=C, tm=tm, D=D)
    return pl.pallas_call(
        kern,
        grid=(B, nI),
        in_specs=[pl.BlockSpec((None, tm, D), row),
                  pl.BlockSpec((tm, D), lambda b, i: (b * nI + i, 0)),
                  pl.BlockSpec((tm, D), lambda b, i: (B * nI + b * nI + i, 0)),
                  pl.BlockSpec((tm, LANE), lambda b, i: (b * nI + i, 0)),
                  pl.BlockSpec((None, 1, 6 * D), lambda b, i: (b, 0, 0)),
                  pl.BlockSpec((None, 1, 6 * D), lambda b, i: (nb, 0, 0)),
                  pl.BlockSpec((1, D), lambda b, i: (0, 0)), pl.BlockSpec((1, D), lambda b, i: (0, 0))],
        out_specs=pl.BlockSpec((None, tm, D), row),
        out_shape=jax.ShapeDtypeStruct((B, Lc, D), f32),
        compiler_params=_cp("parallel", "parallel"),
        name="final_ln",
    )(x, y2, y2, r, mods, mods, ln_g.reshape(1, D), ln_b.reshape(1, D))


S5_PAIRS = S5_GROUPS // 2
S5_TILE = 8


def _s5_pair_tables(lam_re, lam_im, b_re, b_im, c_re, c_im, log_dt):
    T1, G, P, J = S5_BLOCK, S5_GROUPS, S5_STATE, S5_GROUP
    n = T1 * J
    t_idx = jnp.arange(T1)
    lag = t_idx[None, :] - t_idx[:, None]
    toep = 0.0
    bc_secs, cc_secs, a_secs = [], [], []
    for d in range(2):
        tb = _s5_tables_dir(lam_re[d], lam_im[d], b_re[d], b_im[d], c_re[d], c_im[d], log_dt[d])
        kern, pw, b_bar, c, lam_dt = tb
        use = (lag >= 0) if d == 0 else (lag <= 0)
        kk = jnp.transpose(kern[jnp.clip(jnp.abs(lag), 0, T1 - 1)], (2, 0, 4, 1, 3))
        toep = toep + jnp.where(use[None, :, None, :, None], kk, 0.0).reshape(G, n, n)
        p_in = pw[T1 - 1 - t_idx] if d == 0 else pw[t_idx]
        bc = jnp.transpose(p_in[:, :, :, None] * b_bar[None], (1, 0, 3, 2)).reshape(G, n, P)
        p_out = pw[t_idx + 1] if d == 0 else pw[T1 - t_idx]
        cc = jnp.transpose(c[None] * p_out[:, :, None, :], (1, 3, 0, 2)).reshape(G, P, n)
        bc_secs += [bc.real, bc.imag]
        cc_secs += [cc.real, -cc.imag]
        for k in range(1, S5_TILE + 1):
            ak = jnp.exp(lam_dt * float(T1 * k))
            a_secs += [ak.real, ak.imag]
    diag = jnp.eye(2, dtype=f32)[None, None, :, None, None, :, None]
    widen = lambda t: jnp.transpose(t, (0, 2, 1, 3, 4, 5))[:, :, :, :, :, None, :] * diag
    tp = widen(toep.reshape(S5_PAIRS, 2, T1, J, T1, J)).reshape(S5_PAIRS, 2 * n, 2 * n)
    bc = widen(jnp.stack(bc_secs, axis=2).reshape(S5_PAIRS, 2, T1, J, 4, P)).reshape(S5_PAIRS, 2 * n, 4 * 2 * P)
    cc = widen(jnp.stack(cc_secs, axis=1).reshape(S5_PAIRS, 2, 4, P, T1, J)).reshape(S5_PAIRS, 4 * 2 * P, 2 * n)
    apw = jnp.stack(a_secs, axis=0).reshape(2, S5_TILE, 2, S5_PAIRS, 2 * P)
    apw = jnp.transpose(apw, (3, 0, 1, 2, 4)).reshape(S5_PAIRS, 4 * S5_TILE, 2 * P)
    order = (jnp.arange(S5_TILE), jnp.arange(S5_TILE - 1, -1, -1))
    atile = jnp.stack([apw[:, d * 2 * S5_TILE + 2 * order[d] + ri] for d in range(2) for ri in range(2)], axis=1)
    return tp.astype(bf16), bc.astype(bf16), cc.astype(bf16), apw, atile.reshape(S5_PAIRS, 4 * S5_TILE, 2 * P)


def _s5_tables_dir(lam_re, lam_im, b_re, b_im, c_re, c_im, log_dt):
    T1 = S5_BLOCK
    lam = lax.complex(lam_re.astype(f32), lam_im.astype(f32))
    lam_dt = lam * jnp.exp(log_dt.astype(f32))[:, None]
    lam_bar = jnp.exp(lam_dt)
    b_bar = ((lam_bar - 1.0) / lam)[..., None] * lax.complex(b_re.astype(f32), b_im.astype(f32))
    c = lax.complex(c_re.astype(f32), c_im.astype(f32))
    pw = jnp.exp(lam_dt[None] * jnp.arange(T1 + 1, dtype=f32)[:, None, None])
    kern = jnp.einsum('gip,tgp,gpj->tgij', c, pw[:T1], b_bar, precision=lax.Precision.HIGHEST).real
    return kern, pw, b_bar, c, lam_dt


def _s5_kernel(s_ref, toep_ref, bc_ref, cc_ref, apw_ref, atile_ref, y_ref, *, nb, nb_ctx):
    T1, W, NT = S5_BLOCK, 2 * S5_GROUP, S5_TILE
    Q = LANE // W
    nt, nt_ctx = nb // NT, nb_ctx // NT
    slabs = [s_ref[pl.ds(t, nb, stride=T1), :] for t in range(T1)]
    grp = lax.broadcasted_iota(jnp.int32, (nb, LANE), 1) // W
    row = lax.broadcasted_iota(jnp.int32, (nb, LANE), 0)
    row8 = row % NT
    roll_rows = lambda x, k: pltpu.roll(x, k % nb, axis=0)
    out = [None] * T1
    for q in range(Q):
        cols = []
        for v in range(T1 // Q):
            acc = None
            for r in range(Q):
                shift = ((r - q) * W) % LANE
                piece = slabs[Q * v + r] if shift == 0 else pltpu.roll(slabs[Q * v + r], shift, axis=1)
                acc = piece if acc is None else jnp.where(grp == r, piece, acc)
            cols.append(acc)
        u = jnp.concatenate(cols, axis=1).astype(bf16)
        gin = jnp.dot(u, bc_ref[q], preferred_element_type=f32)
        states = []
        for d in range(2):
            apow = lambda k, ri: apw_ref[q, d * 2 * NT + 2 * (k - 1) + ri:d * 2 * NT + 2 * (k - 1) + ri + 1, :]
            hr, hi = gin[:, 2 * d * LANE:(2 * d + 1) * LANE], gin[:, (2 * d + 1) * LANE:(2 * d + 2) * LANE]
            for k in (1, 2, 4):
                sr, si = (roll_rows(hr, k), roll_rows(hi, k)) if d == 0 else (roll_rows(hr, -k), roll_rows(hi, -k))
                keep = (row8 >= k) if d == 0 else (row8 < NT - k)
                sr, si = jnp.where(keep, sr, 0.0), jnp.where(keep, si, 0.0)
                ar, ai = apow(k, 0), apow(k, 1)
                hr, hi = hr + ar * sr - ai * si, hi + ar * si + ai * sr
            a8r, a8i = apow(NT, 0), apow(NT, 1)
            order = range(nt) if d == 0 else list(range(nt_ctx - 1, -1, -1)) + list(range(nt - 1, nt_ctx - 1, -1))
            end = NT - 1 if d == 0 else 0
            cr = ci = jnp.zeros((1, LANE), f32)
            enter_r, enter_i = [None] * nt, [None] * nt
            for t in order:
                enter_r[t], enter_i[t] = cr, ci
                er, ei = hr[t * NT + end:t * NT + end + 1, :], hi[t * NT + end:t * NT + end + 1, :]
                cr, ci = a8r * cr - a8i * ci + er, a8r * ci + a8i * cr + ei
            cfr = jnp.concatenate([jnp.broadcast_to(c, (NT, LANE)) for c in enter_r], axis=0)
            cfi = jnp.concatenate([jnp.broadcast_to(c, (NT, LANE)) for c in enter_i], axis=0)
            tr = jnp.concatenate([atile_ref[q, 2 * d * NT:(2 * d + 1) * NT, :]] * nt, axis=0)
            ti = jnp.concatenate([atile_ref[q, (2 * d + 1) * NT:(2 * d + 2) * NT, :]] * nt, axis=0)
            hr, hi = hr + tr * cfr - ti * cfi, hi + tr * cfi + ti * cfr
            pr, pi = (roll_rows(hr, 1), roll_rows(hi, 1)) if d == 0 else (roll_rows(hr, -1), roll_rows(hi, -1))
            first = (row == 0) if d == 0 else (row == nb_ctx - 1)
            states += [jnp.where(first, 0.0, pr), jnp.where(first, 0.0, pi)]
        hp = jnp.concatenate(states, axis=1).astype(bf16)
        y = (jnp.dot(u, toep_ref[q], preferred_element_type=f32)
             + jnp.dot(hp, cc_ref[q], preferred_element_type=f32))
        for t in range(T1):
            v, r = divmod(t, Q)
            shift = ((q - r) * W) % LANE
            col = y[:, v * LANE:(v + 1) * LANE]
            piece = col if shift == 0 else pltpu.roll(col, shift, axis=1)
            out[t] = piece if out[t] is None else jnp.where(grp == q, piece, out[t])
    for t in range(T1):
        y_ref[pl.ds(t, nb, stride=T1), :] = out[t]


def _s5_mixer(s, C, lam_re, lam_im, b_re, b_im, c_re, c_im, log_dt):
    B, Lc, N = s.shape
    T1 = S5_BLOCK
    assert C % (T1 * S5_TILE) == 0 and Lc % (T1 * S5_TILE) == 0
    nb = Lc // T1
    n2 = 2 * T1 * S5_GROUP
    Q = LANE // (2 * S5_GROUP)
    toep, bc, cc, apw, atile = _s5_pair_tables(lam_re, lam_im, b_re, b_im, c_re, c_im, log_dt)
    kern = functools.partial(_s5_kernel, nb=nb, nb_ctx=C // T1)
    tab = lambda rows, cols: pl.BlockSpec((Q, rows, cols), lambda cb, b: (cb, 0, 0))
    return pl.pallas_call(
        kern,
        grid=(N // LANE, B),
        in_specs=[pl.BlockSpec((None, Lc, LANE), lambda cb, b: (b, 0, cb)),
                  tab(n2, n2), tab(n2, 4 * LANE), tab(4 * LANE, n2), tab(4 * S5_TILE, LANE), tab(4 * S5_TILE, LANE)],
        out_specs=pl.BlockSpec((None, Lc, LANE), lambda cb, b: (b, 0, cb)),
        out_shape=jax.ShapeDtypeStruct((B, Lc, N), f32),
        compiler_params=_cp("parallel", "parallel"),
        name="s5_scan",
    )(s, toep, bc, cc, apw, atile)


def _dotT(a, b):
    return lax.dot_general(a.astype(bf16), b.astype(bf16), (((1,), (1,)), ((), ())), preferred_element_type=f32)


def _dot(a, b):
    return jnp.dot(a.astype(bf16), b.astype(bf16), preferred_element_type=f32)


def _split(a):
    hi = a.astype(bf16)
    return hi, (a - hi.astype(f32)).astype(bf16)


def _dot_split(a, b):
    n = a.shape[1]
    ah, al = _split(a)
    bh, bl = _split(b)
    lhs = jnp.concatenate([ah, al], axis=1)
    rhs = jnp.concatenate([jnp.concatenate([bh, bl], axis=1),
                           jnp.concatenate([bh, jnp.zeros_like(bl)], axis=1)], axis=0)
    out = jnp.dot(lhs, rhs, preferred_element_type=f32)
    return out[:, :n] + out[:, n:]


def _gdn_chunk_index(n, n_ctx, n_all, reverse):
    if not reverse:
        return n
    return jnp.where(n < n_ctx, n_ctx - 1 - n, n_all - 1 - (n - n_ctx))


def _gdn_kernel(x_ref, xp_ref, xn_ref, gt_ref, cw_ref, alog_ref, dtb_ref, o_ref, s_scr, *, C, Lc, Cn, reverse):
    n = pl.program_id(1)
    n_ctx, n_all = C // Cn, Lc // Cn
    ci = _gdn_chunk_index(n, n_ctx, n_all, reverse)
    H, dk, dv = GDN_HEADS, GDN_DK, GDN_DV
    d = 1 if reverse else 0

    @pl.when(n == 0)
    def _():
        s_scr[...] = jnp.zeros_like(s_scr)

    x = x_ref[...]
    loc = lax.broadcasted_iota(jnp.int32, (Cn, 1), 0)
    row = ci * Cn + loc
    prev = jnp.where(loc == 0, xp_ref[7:8, :], pltpu.roll(x, 1, axis=0))
    prev = jnp.where(jnp.logical_or(row == 0, row == C), 0.0, prev)
    nxt = jnp.where(loc == Cn - 1, xn_ref[0:1, :], pltpu.roll(x, Cn - 1, axis=0))
    nxt = jnp.where(jnp.logical_or(row == C - 1, row == Lc - 1), 0.0, nxt)
    y = prev * cw_ref[0:1, :] + x * cw_ref[1:2, :] + nxt * cw_ref[2:3, :]
    y = y * _sigmoid(y)

    graw = gt_ref[...]
    z = graw + dtb_ref[...]
    g_all = -jnp.exp(alog_ref[...]) * (jnp.maximum(z, 0.0) + jnp.log(1.0 + jnp.exp(-jnp.abs(z))))
    beta_all = _sigmoid(graw)

    ii = lax.broadcasted_iota(jnp.int32, (Cn, Cn), 0)
    jj = lax.broadcasted_iota(jnp.int32, (Cn, Cn), 1)
    incl = (ii <= jj) if reverse else (ii >= jj)
    strict = (ii < jj) if reverse else (ii > jj)
    eye = (ii == jj).astype(f32)
    tri = incl.astype(bf16)
    g_hi = g_all.astype(bf16)
    g_lo = (g_all - g_hi.astype(f32)).astype(bf16)
    gc_all = jnp.dot(tri, g_hi, preferred_element_type=f32) + jnp.dot(tri, g_lo, preferred_element_type=f32)
    gc_all_t = gc_all.T
    last = 0 if reverse else Cn - 1

    heads = range(H)
    gc = [gc_all[:, 8 * d + h:8 * d + h + 1] for h in heads]
    gl = [gc_all[last:last + 1, 8 * d + h:8 * d + h + 1] for h in heads]
    beta = [beta_all[:, 16 + 8 * d + h:17 + 8 * d + h] for h in heads]
    decay = [jnp.where(incl, jnp.exp(jnp.where(incl, gc[h] - gc_all_t[8 * d + h:8 * d + h + 1, :], 0.0)), 0.0)
             for h in heads]
    l2 = lambda t: t * lax.rsqrt(jnp.sum(t * t, axis=1, keepdims=True) + 1e-6)
    q = [l2(y[:, h * dk:(h + 1) * dk]) * (dk ** -0.5) for h in heads]
    k = [l2(y[:, H * dk + h * dk:H * dk + (h + 1) * dk]) for h in heads]
    v = [y[:, 2 * H * dk + h * dv:2 * H * dk + (h + 1) * dv] for h in heads]
    kb = [k[h] * beta[h] for h in heads]
    kq = [_dotT(jnp.concatenate([kb[h], q[h]], axis=0), k[h]) for h in heads]
    m = [jnp.where(strict, kq[h][:Cn] * decay[h], 0.0) for h in heads]
    a = [jnp.where(incl, kq[h][Cn:] * decay[h], 0.0) for h in heads]
    t_inv = [eye - m[h] for h in heads]
    pw = m
    for _ in range(int(math.log2(Cn)) - 1):
        pw = [_dot_split(pw[h], pw[h]) for h in heads]
        t_inv = [t_inv[h] + _dot_split(t_inv[h], pw[h]) for h in heads]
    e_gc = [jnp.exp(gc[h]) for h in heads]
    uw = [_dot(t_inv[h], jnp.concatenate([v[h] * beta[h], kb[h] * e_gc[h]], axis=1)) for h in heads]
    s_old = [s_scr[h] for h in heads]
    r = [_dot(jnp.concatenate([uw[h][:, dv:], q[h] * e_gc[h]], axis=0), s_old[h]) for h in heads]
    v_new = [uw[h][:, :dv] - r[h][:Cn] for h in heads]
    for h in heads:
        o_ref[:, h * dv:(h + 1) * dv] = r[h][Cn:] + _dot(a[h], v_new[h])
    for h in heads:
        kd = k[h] * jnp.exp(gl[h] - gc[h])
        s_scr[h] = s_old[h] * jnp.exp(gl[h]) + lax.dot_general(
            kd.astype(bf16), v_new[h].astype(bf16), (((0,), (0,)), ((), ())), preferred_element_type=f32)


def _gdn_direction(p, conv_w, alog_row, dtb_row, *, C, reverse):
    B, Lc, _ = p.shape
    Cn = LANE
    n_ctx, n_all = C // Cn, Lc // Cn
    nq = GDN_QKV
    ci = lambda n: _gdn_chunk_index(n, n_ctx, n_all, reverse)
    sub = Cn // 8
    kern = functools.partial(_gdn_kernel, C=C, Lc=Lc, Cn=Cn, reverse=reverse)
    return pl.pallas_call(
        kern,
        grid=(B, n_all),
        in_specs=[pl.BlockSpec((None, Cn, nq), lambda b, n: (b, ci(n), 0)),
                  pl.BlockSpec((None, 8, nq), lambda b, n: (b, jnp.maximum(ci(n) * sub - 1, 0), 0)),
                  pl.BlockSpec((None, 8, nq), lambda b, n: (b, jnp.minimum((ci(n) + 1) * sub, Lc // 8 - 1), 0)),
                  pl.BlockSpec((None, Cn, LANE), lambda b, n: (b, ci(n), O_GATE // LANE)),
                  pl.BlockSpec((3, nq), lambda b, n: (0, 0)),
                  pl.BlockSpec((1, LANE), lambda b, n: (0, 0)),
                  pl.BlockSpec((1, LANE), lambda b, n: (0, 0))],
        out_specs=pl.BlockSpec((None, Cn, GDN_HEADS * GDN_DV), lambda b, n: (b, ci(n), 0)),
        out_shape=jax.ShapeDtypeStruct((B, Lc, GDN_HEADS * GDN_DV), f32),
        scratch_shapes=[pltpu.VMEM((GDN_HEADS, GDN_DK, GDN_DV), f32)],
        compiler_params=_cp("parallel", "arbitrary"),
        name="gdn_rev" if reverse else "gdn_fwd",
    )(p, p, p, p, conv_w, alog_row, dtb_row)


def _gdn_pallas(p, C, conv_w, a_log, dt_bias):
    outs = []
    for d in range(2):
        pad = lambda t: jnp.zeros((1, LANE), f32).at[0, 8 * d:8 * d + GDN_HEADS].set(t[d].astype(f32))
        outs.append(_gdn_direction(p, conv_w, pad(a_log), pad(dt_bias), C=C, reverse=(d == 1)))
    return outs


def _dispatch_plan(e_idx, n_tiles):
    T = e_idx.shape[0]
    tm = MOE_TM
    assert 2 * T + tm <= (1 << MOE_CODE_BITS) and T < (1 << (31 - MOE_CODE_BITS))
    flat = e_idx.reshape(-1)
    onehot = (flat[:, None] == jnp.arange(N_EXPERTS)[None, :]).astype(jnp.int32)
    rank = jnp.sum((jnp.cumsum(onehot, axis=0) - onehot) * onehot, axis=1)
    counts = jnp.sum(onehot, axis=0)
    padded = ((counts + tm - 1) // tm) * tm
    ends = jnp.cumsum(padded)
    offs = ends - padded
    pos = offs[flat] + rank
    slot_id = jnp.arange(2 * T, dtype=jnp.int32)
    tok, k = slot_id // 2, slot_id % 2
    spare = 2 * T + jnp.arange(n_tiles * tm, dtype=jnp.int32) % tm
    code = spare.at[pos].set((tok << MOE_CODE_BITS) | (k * T + tok))
    tile_start = jnp.arange(n_tiles, dtype=jnp.int32) * tm
    tile_expert = jnp.minimum(jnp.sum((tile_start[:, None] >= ends[None, :]).astype(jnp.int32), axis=1), N_EXPERTS - 1)
    n_active = (ends[-1] // tm).astype(jnp.int32).reshape(1)
    last_e = tile_expert[jnp.maximum(n_active[0] - 1, 0)]
    tile_expert = jnp.where(tile_start < ends[-1], tile_expert, last_e).astype(jnp.int32)
    return code, tile_expert, n_active


def _moe(u, logits, bias_row, wg, wu, wd, layer):
    B, Lc, D = u.shape
    T = B * Lc
    r = _router(logits.reshape(T, LANE), bias_row)
    n_tiles = (2 * T + MOE_TM - 1) // MOE_TM + N_EXPERTS
    code, tile_expert, n_active = _dispatch_plan(r[:, 0:2].astype(jnp.int32), n_tiles)
    return _moe_experts(u.reshape(T, D), code, tile_expert, n_active, wg, wu, wd, layer, 2 * T + MOE_TM), r


def _rope_tables(L, C):
    rows = L // GRID_W
    t_row = jnp.repeat(jnp.arange(rows), GRID_W).astype(f32)
    t_col = jnp.tile(jnp.arange(GRID_W), rows).astype(f32)
    n_freq = ROPE_DIM // 4
    inv_freq = ROPE_BASE ** (-jnp.arange(n_freq, dtype=f32) / n_freq)
    ang = jnp.concatenate([t_row[:, None] * inv_freq, t_col[:, None] * inv_freq], axis=-1)
    cos = jnp.repeat(jnp.cos(ang), 2, axis=1)
    sin = jnp.repeat(jnp.sin(ang), 2, axis=1) * jnp.tile(jnp.array([-1.0, 1.0], f32), ROPE_DIM // 2)
    one = jnp.ones((L, ROPE_DIM), f32)
    zero = jnp.zeros((L, ROPE_DIM), f32)
    ctx = lambda t, fill: jnp.concatenate([jnp.full((C, LANE), fill, f32), t], axis=0)
    return (ctx(jnp.concatenate([cos, cos], 1), 1.0), ctx(jnp.concatenate([sin, sin], 1), 0.0),
            ctx(jnp.concatenate([cos, one], 1), 1.0), ctx(jnp.concatenate([sin, zero], 1), 0.0))


def _prep_even_w_in(w):
    D = w.shape[0]
    nqk = 2 * DA_HEADS * DA_HEAD_DIM
    heads_first = lambda t: t.reshape(D, 2, DA_HEADS, DA_HEAD_DIM).transpose(0, 2, 1, 3).reshape(D, nqk)
    q = heads_first(w[:, :nqk]) * (DA_HEAD_DIM ** -0.5)
    k = heads_first(w[:, nqk:2 * nqk])
    return jnp.concatenate([q, k, w[:, 2 * nqk:]], axis=1).astype(bf16)


def _prep_odd_w_in(w):
    D = w.shape[0]
    z = lambda n: jnp.zeros((D, n), w.dtype)
    c0 = MLA_Q_RANK
    c1 = c0 + MLA_KV_RANK
    c2 = c1 + MLA_ROPE
    c3 = c2 + GDN_QKV
    c4 = c3 + GDN_HEADS * GDN_DV
    parts = [w[:, c2:c3], w[:, c3:c4], w[:, :c0], w[:, c0:c1], w[:, c1:c2], z(LANE - MLA_ROPE),
             w[:, c4:], z(LANE - 4 * GDN_HEADS)]
    out = jnp.concatenate(parts, axis=1)
    assert out.shape[1] == O_IN
    return out.astype(bf16)


def _prep_mla_w(w_uq, w_ukv):
    H = MLA_HEADS
    rq = w_uq.shape[0]
    wq = w_uq.reshape(rq, H, MLA_NOPE + MLA_ROPE)
    nope = wq[:, :, :MLA_NOPE].reshape(rq, H * MLA_NOPE)
    rope = jnp.concatenate([wq[:, :, MLA_NOPE:], jnp.zeros((rq, H, LANE - MLA_ROPE), w_uq.dtype)], axis=2)
    wq_p = jnp.concatenate([nope, rope.reshape(rq, H * LANE)], axis=1).astype(bf16)
    rkv = w_ukv.shape[0]
    wkv = w_ukv.reshape(rkv, H, MLA_NOPE + MLA_V)
    wkv_p = jnp.concatenate([wkv[:, :, :MLA_NOPE].reshape(rkv, H * MLA_NOPE),
                             wkv[:, :, MLA_NOPE:].reshape(rkv, H * MLA_V)], axis=1).astype(bf16)
    return wq_p, wkv_p


def kernel(x, c, ctx, c_ctx, mod_w, mod_b, ln_g, ln_b, e_w_in, e_w_out, da_lam, da_subln, s5_lam_re, s5_lam_im, s5_b_re, s5_b_im, s5_c_re, s5_c_im, s5_log_dt, s5_d, s5_glu_w, s5_glu_b, o_w_in, o_w_out, mla_q_norm, mla_kv_norm, mla_w_uq, mla_w_ukv, gdn_conv, gdn_a_log, gdn_dt_bias, gdn_norm, router_w, router_bias, moe_w_gate, moe_w_up, moe_w_down):
    B, L, D = x.shape
    C = ctx.shape[1]
    depth = mod_w.shape[0]
    assert depth == DEPTH

    R = -(-(B + 1) // 8) * 8
    cc = jnp.concatenate([c, c_ctx[None, :], jnp.zeros((R - B - 1, D), f32)], axis=0)
    mods_all = _modulation(cc, mod_w, mod_b)[:, :B + 1].reshape(depth, B + 1, 1, 6 * D)

    cos_e, sin_e, cos_o, sin_o = _rope_tables(L, C)
    wg_all, wu_all, wd_all = moe_w_gate.astype(bf16), moe_w_up.astype(bf16), moe_w_down.astype(bf16)
    r_pad = jnp.concatenate([router_w.astype(f32), jnp.zeros((D, LANE - N_EXPERTS), f32)], axis=1)
    r_hi = r_pad.astype(bf16)
    r_lo = (r_pad - r_hi.astype(f32)).astype(bf16)
    bias_row = jnp.concatenate([router_bias.astype(f32), jnp.zeros((LANE - N_EXPERTS,), f32)]).reshape(1, LANE)

    xs = jnp.concatenate([ctx, x], axis=1)
    n_att = DA_HEADS * DA_V_DIM
    for l in range(depth):
        mods = mods_all[l]
        i = l // 2
        if l % 2 == 0:
            qkv, s = _inproj(xs, mods, _prep_even_w_in(e_w_in[i]), cos_e, sin_e, C=C,
                             n_rope_cols=4 * DA_HEADS * DA_HEAD_DIM, n_a_cols=3072, a_dtype=bf16, b_dtype=f32)
            a1 = _diff_attention(qkv, da_lam[i].astype(f32), da_subln[i].astype(f32), C=C,
                                 lam_init=0.8 - 0.6 * math.exp(-0.3 * l))
            y = _s5_mixer(s, C, s5_lam_re[i], s5_lam_im[i], s5_b_re[i], s5_b_im[i], s5_c_re[i], s5_c_im[i],
                          s5_log_dt[i])
            a2 = _s5_finish(y, s, s5_d[i], s5_glu_w[i], s5_glu_b[i])
            w_out = e_w_out[i].astype(bf16)
        else:
            (p,) = _inproj(xs, mods, _prep_odd_w_in(o_w_in[i]), cos_e, sin_e, C=C,
                           n_rope_cols=0, n_a_cols=O_IN, a_dtype=f32, b_dtype=f32)
            wq_p, wkv_p = _prep_mla_w(mla_w_uq[i], mla_w_ukv[i])
            q, kv, kr = _mla_up(p, mla_q_norm[i], mla_kv_norm[i], wq_p, wkv_p, cos_o, sin_o)
            a1 = _mla_attention(q, kv, kr, C=C)
            of, ob = _gdn_pallas(p, C, gdn_conv[i].astype(f32), gdn_a_log[i], gdn_dt_bias[i])
            a2 = _gdn_finish(of, ob, p, gdn_norm[i])
            w_out = o_w_out[i].astype(bf16)
        xs, u, logits = _merge(xs, a1, a2, w_out[:n_att], w_out[n_att:], mods, ln_g[l, 0], ln_b[l, 0],
                               r_hi, r_lo, C=C)
        y2, r = _moe(u, logits, bias_row, wg_all, wu_all, wd_all, l)
        xs = _final_ln(xs, y2, r, mods, ln_g[l, 1], ln_b[l, 1], C=C)
    return xs[:, C:, :]
```

```python
import functools
import math

import jax
import jax.numpy as jnp
import numpy as np
from jax import lax
from jax.experimental import pallas as pl
from jax.experimental.pallas import tpu as pltpu

f32 = jnp.float32
bf16 = jnp.bfloat16

GRID_W = 64
ROPE_DIM = 64
ROPE_BASE = 10000.0
DA_HEADS = 8
DA_HEAD_DIM = 64
DA_V_DIM = 128
S5_CHANNELS = 512
S5_GROUP = 16
S5_GROUPS = 32
S5_STATE = 64
S5_BLOCK = 16
MLA_HEADS = 8
MLA_Q_RANK = 512
MLA_KV_RANK = 256
MLA_NOPE = 128
MLA_ROPE = 64
MLA_V = 128
GDN_HEADS = 8
GDN_DK = 128
GDN_DV = 128
GDN_QKV = 3072
GDN_CHUNK = 64
N_EXPERTS = 16
N_GROUPS = 4
EXPERTS_PER_GROUP = 4
DEPTH = 4
DEEPNORM_ALPHA = (2 * DEPTH) ** 0.25
LANE = 128
MOE_TM = 256
VMEM_LIMIT = 56 * 1024 * 1024

O_QKV, O_Z, O_CQ, O_CKV, O_KR, O_GATE, O_IN = 0, 3072, 4096, 4608, 4864, 4992, 5120


def _cp(*sem):
    return pltpu.CompilerParams(dimension_semantics=sem, vmem_limit_bytes=VMEM_LIMIT)


def _tile(n, cap, mult=16):
    best = None
    for t in range(mult, min(n, cap) + 1, mult):
        if n % t == 0:
            best = t
    assert best is not None, (n, cap, mult)
    return best


def _sigmoid(x):
    return 1.0 / (1.0 + jnp.exp(-x))


def _mod_kernel(a_ref, w_ref, b_ref, o_ref):
    a = a_ref[...]
    act = (a * _sigmoid(a)).astype(bf16)
    o_ref[...] = jnp.dot(act, w_ref[...].astype(bf16), preferred_element_type=f32) + b_ref[...]


def _modulation(cc, mod_w, mod_b):
    depth, D, N = mod_w.shape
    R = cc.shape[0]
    tn = _tile(N, 1024, LANE)
    return pl.pallas_call(
        _mod_kernel,
        grid=(depth, N // tn),
        in_specs=[pl.BlockSpec((R, D), lambda l, j: (0, 0)),
                  pl.BlockSpec((None, D, tn), lambda l, j: (l, 0, j)),
                  pl.BlockSpec((None, 1, tn), lambda l, j: (l, 0, j))],
        out_specs=pl.BlockSpec((None, R, tn), lambda l, j: (l, 0, j)),
        out_shape=jax.ShapeDtypeStruct((depth, R, N), f32),
        compiler_params=_cp("parallel", "parallel"),
        name="modulation",
    )(cc, mod_w, mod_b.reshape(depth, 1, N))


def _select_mod(ml, mc, row0, tm, C, D, chunks):
    is_ctx = (row0 + lax.broadcasted_iota(jnp.int32, (tm, 1), 0)) < C
    return [jnp.where(is_ctx, mc[:, k * D:(k + 1) * D], ml[:, k * D:(k + 1) * D]) for k in chunks]


def _rope(seg, cos, sin):
    nxt = pltpu.roll(seg, LANE - 1, axis=1)
    prv = pltpu.roll(seg, 1, axis=1)
    even = (lax.broadcasted_iota(jnp.int32, seg.shape, 1) % 2) == 0
    return seg * cos + jnp.where(even, nxt, prv) * sin


def _inproj_kernel(x_ref, ml_ref, mc_ref, w_ref, cos_ref, sin_ref, *rest, C, tm, tn, D, n_rope, n_a, has_b):
    if has_b:
        oa_ref, ob_ref, u_scr = rest
    else:
        oa_ref, u_scr = rest
        ob_ref = None
    i = pl.program_id(1)
    j = pl.program_id(2)

    @pl.when(j == 0)
    def _():
        shift, scale = _select_mod(ml_ref[...], mc_ref[...], i * tm, tm, C, D, (0, 1))
        u_scr[...] = (x_ref[...] * (1.0 + scale) + shift).astype(bf16)

    half = tm // 2
    spans = [(0, half), (half, tm)]

    def emit(write):
        acc = lambda r0, r1: jnp.dot(u_scr[r0:r1, :], w_ref[...], preferred_element_type=f32)
        a_next = acc(*spans[0])
        for n, (r0, r1) in enumerate(spans):
            a = a_next
            if n + 1 < len(spans):
                a_next = acc(*spans[n + 1])
            write(a, r0, r1)

    def write_rope(a, r0, r1):
        cos = cos_ref[r0:r1, :]
        sin = sin_ref[r0:r1, :]
        for c in range(tn // LANE):
            oa_ref[r0:r1, c * LANE:(c + 1) * LANE] = _rope(a[:, c * LANE:(c + 1) * LANE], cos, sin).astype(oa_ref.dtype)

    def write_a(a, r0, r1):
        oa_ref[r0:r1, :] = a.astype(oa_ref.dtype)

    def write_b(a, r0, r1):
        ob_ref[r0:r1, :] = a.astype(ob_ref.dtype)

    if n_rope > 0:
        @pl.when(j < n_rope)
        def _():
            emit(write_rope)

    @pl.when(jnp.logical_and(j >= n_rope, j < n_a))
    def _():
        emit(write_a)

    if has_b:
        @pl.when(j >= n_a)
        def _():
            emit(write_b)


def _inproj(x, mods, w, cos, sin, *, C, n_rope_cols, n_a_cols, a_dtype, b_dtype):
    B, Lc, D = x.shape
    N = w.shape[1]
    tm = _tile(Lc, 1088)
    tn = 512
    assert N % tn == 0 and n_rope_cols % tn == 0 and n_a_cols % tn == 0
    n_a = n_a_cols // tn
    has_b = n_a_cols < N
    kern = functools.partial(_inproj_kernel, C=C, tm=tm, tn=tn, D=D, n_rope=n_rope_cols // tn, n_a=n_a, has_b=has_b)
    out_shape = [jax.ShapeDtypeStruct((B, Lc, n_a_cols), a_dtype)]
    out_specs = [pl.BlockSpec((None, tm, tn), lambda b, i, j: (b, i, jnp.minimum(j, n_a - 1)))]
    if has_b:
        out_shape.append(jax.ShapeDtypeStruct((B, Lc, N - n_a_cols), b_dtype))
        out_specs.append(pl.BlockSpec((None, tm, tn), lambda b, i, j: (b, i, jnp.maximum(j - n_a, 0))))
    nb = mods.shape[0] - 1
    return pl.pallas_call(
        kern,
        grid=(B, Lc // tm, N // tn),
        in_specs=[pl.BlockSpec((None, tm, D), lambda b, i, j: (b, i, 0)),
                  pl.BlockSpec((None, 1, 6 * D), lambda b, i, j: (b, 0, 0)),
                  pl.BlockSpec((None, 1, 6 * D), lambda b, i, j: (nb, 0, 0)),
                  pl.BlockSpec((D, tn), lambda b, i, j: (0, j)),
                  pl.BlockSpec((tm, LANE), lambda b, i, j: (i, 0)),
                  pl.BlockSpec((tm, LANE), lambda b, i, j: (i, 0))],
        out_specs=out_specs,
        out_shape=out_shape,
        scratch_shapes=[pltpu.VMEM((tm, D), bf16)],
        compiler_params=_cp("parallel", "parallel", "arbitrary"),
        name="inproj",
    )(x, mods, mods, w, cos, sin)


def _transpose_bf16(x):
    return x.astype(f32).T.astype(bf16)


ATTN_KEY_CHUNK = 1024


def _attend_t(ks, q_ts, vt_ref, nkeys, scale=None):
    n = len(ks)
    chunks = [(c0, min(c0 + ATTN_KEY_CHUNK, nkeys)) for c0 in range(0, nkeys, ATTN_KEY_CHUNK)]

    def scores(c):
        out = [jnp.dot(ks[i][c[0]:c[1]], q_ts[i], preferred_element_type=f32) for i in range(n)]
        return out if scale is None else [t * scale for t in out]

    m, l, acc = [None] * n, [None] * n, [None] * n
    s_next = scores(chunks[0])
    for j, (c0, c1) in enumerate(chunks):
        s_cur = s_next
        if j + 1 < len(chunks):
            s_next = scores(chunks[j + 1])
        v_c = vt_ref[:, c0:c1]
        for i in range(n):
            mc = jnp.max(s_cur[i], axis=0, keepdims=True)
            m_new = mc if m[i] is None else jnp.maximum(m[i], mc)
            p = jnp.exp(s_cur[i] - m_new)
            pv = jnp.dot(v_c, p.astype(bf16), preferred_element_type=f32)
            if m[i] is None:
                l[i], acc[i] = jnp.sum(p, axis=0, keepdims=True), pv
            else:
                alpha = jnp.exp(m[i] - m_new)
                l[i] = alpha * l[i] + jnp.sum(p, axis=0, keepdims=True)
                acc[i] = alpha * acc[i] + pv
            m[i] = m_new
    return [acc[i] / l[i] for i in range(n)]


def _diff_attn_kernel(q_ref, k_ref, v_ref, lam_ref, sub_ref, o_ref, vt_scr, *, C, tq, lam_init):
    qi = pl.program_id(2)

    @pl.when(qi == 0)
    def _():
        vt_scr[...] = _transpose_bf16(v_ref[...])

    lv = lam_ref[...]
    lam = (jnp.exp(jnp.sum(lv[0:1] * lv[1:2], axis=1, keepdims=True))
           - jnp.exp(jnp.sum(lv[2:3] * lv[3:4], axis=1, keepdims=True)) + lam_init)

    def run(nkeys):
        q_t = _transpose_bf16(q_ref[...])
        k = k_ref[0:nkeys, :]
        hd = DA_HEAD_DIM
        o1, o2 = _attend_t([k[:, 0:hd], k[:, hd:2 * hd]], [q_t[0:hd, :], q_t[hd:2 * hd, :]], vt_scr, nkeys)
        o = (o1 - lam * o2).T
        o = o * lax.rsqrt(jnp.mean(o * o, axis=1, keepdims=True) + 1e-6) * sub_ref[...]
        o_ref[...] = (o * (1.0 - lam_init)).astype(o_ref.dtype)

    @pl.when(qi * tq < C)
    def _():
        run(C)

    @pl.when(qi * tq >= C)
    def _():
        run(k_ref.shape[0])


def _diff_attention(qkv, da_lam, da_subln, *, C, lam_init):
    B, Lc, _ = qkv.shape
    H = DA_HEADS
    tq = _tile(C, 256)
    kern = functools.partial(_diff_attn_kernel, C=C, tq=tq, lam_init=lam_init)
    return pl.pallas_call(
        kern,
        grid=(B, H, Lc // tq),
        in_specs=[pl.BlockSpec((None, tq, LANE), lambda b, h, i: (b, i, h)),
                  pl.BlockSpec((None, Lc, LANE), lambda b, h, i: (b, 0, H + h)),
                  pl.BlockSpec((None, Lc, LANE), lambda b, h, i: (b, 0, 2 * H + h)),
                  pl.BlockSpec((4, DA_HEAD_DIM), lambda b, h, i: (0, 0)),
                  pl.BlockSpec((1, DA_V_DIM), lambda b, h, i: (0, 0))],
        out_specs=pl.BlockSpec((None, tq, LANE), lambda b, h, i: (b, i, h)),
        out_shape=jax.ShapeDtypeStruct((B, Lc, H * DA_V_DIM), bf16),
        scratch_shapes=[pltpu.VMEM((DA_V_DIM, Lc), bf16)],
        compiler_params=_cp("parallel", "parallel", "arbitrary"),
        name="diff_attention",
    )(qkv, qkv, qkv, da_lam, da_subln.reshape(1, DA_V_DIM))


def _mla_attn_kernel(qn_ref, qr_ref, kn_ref, kr_ref, v_ref, o_ref, vt_scr, *, C, tq, scale):
    qi = pl.program_id(2)

    @pl.when(qi == 0)
    def _():
        vt_scr[...] = _transpose_bf16(v_ref[...])

    def run(nkeys):
        q_t = jnp.concatenate([_transpose_bf16(qn_ref[...]), _transpose_bf16(qr_ref[...])], axis=0)
        k = jnp.concatenate([kn_ref[0:nkeys, :], kr_ref[0:nkeys, :]], axis=1)
        o_ref[...] = _attend_t([k], [q_t], vt_scr, nkeys, scale)[0].T.astype(o_ref.dtype)

    @pl.when(qi * tq < C)
    def _():
        run(C)

    @pl.when(qi * tq >= C)
    def _():
        run(kn_ref.shape[0])


def _mla_attention(q, kv, kr, *, C):
    B, Lc, _ = q.shape
    H = MLA_HEADS
    tq = _tile(C, 256)
    kern = functools.partial(_mla_attn_kernel, C=C, tq=tq, scale=(MLA_NOPE + MLA_ROPE) ** -0.5)
    return pl.pallas_call(
        kern,
        grid=(B, H, Lc // tq),
        in_specs=[pl.BlockSpec((None, tq, LANE), lambda b, h, i: (b, i, h)),
                  pl.BlockSpec((None, tq, LANE), lambda b, h, i: (b, i, H + h)),
                  pl.BlockSpec((None, Lc, LANE), lambda b, h, i: (b, 0, h)),
                  pl.BlockSpec((None, Lc, LANE), lambda b, h, i: (b, 0, 0)),
                  pl.BlockSpec((None, Lc, LANE), lambda b, h, i: (b, 0, H + h))],
        out_specs=pl.BlockSpec((None, tq, LANE), lambda b, h, i: (b, i, h)),
        out_shape=jax.ShapeDtypeStruct((B, Lc, H * MLA_V), bf16),
        scratch_shapes=[pltpu.VMEM((MLA_V, Lc), bf16)],
        compiler_params=_cp("parallel", "parallel", "arbitrary"),
        name="mla_attention",
    )(q, q, kv, kr, kv)


def _rms(x, w, eps=1e-6):
    return x * lax.rsqrt(jnp.mean(x * x, axis=1, keepdims=True) + eps) * w


def _mla_up_kernel(cq_ref, ckv_ref, kr_ref, qn_ref, kvn_ref, wq_ref, wkv_ref, cos_ref, sin_ref,
                   q_ref, kv_ref, kro_ref):
    nq = MLA_HEADS * MLA_NOPE
    cos = cos_ref[...]
    sin = sin_ref[...]
    q = jnp.dot(_rms(cq_ref[...], qn_ref[...]).astype(bf16), wq_ref[...], preferred_element_type=f32)
    q_ref[:, 0:nq] = q[:, 0:nq].astype(q_ref.dtype)
    for h in range(MLA_HEADS):
        c0 = nq + h * LANE
        q_ref[:, c0:c0 + LANE] = _rope(q[:, c0:c0 + LANE], cos, sin).astype(q_ref.dtype)
    kv = jnp.dot(_rms(ckv_ref[...], kvn_ref[...]).astype(bf16), wkv_ref[...], preferred_element_type=f32)
    kv_ref[...] = kv.astype(kv_ref.dtype)
    kro_ref[...] = _rope(kr_ref[...], cos, sin).astype(kro_ref.dtype)


def _mla_up(p, q_norm, kv_norm, wq, wkv, cos, sin):
    B, Lc, _ = p.shape
    tm = _tile(Lc, 544)
    nq = wq.shape[1]
    nkv = wkv.shape[1]
    return pl.pallas_call(
        _mla_up_kernel,
        grid=(B, Lc // tm),
        in_specs=[pl.BlockSpec((None, tm, MLA_Q_RANK), lambda b, i: (b, i, O_CQ // MLA_Q_RANK)),
                  pl.BlockSpec((None, tm, MLA_KV_RANK), lambda b, i: (b, i, O_CKV // MLA_KV_RANK)),
                  pl.BlockSpec((None, tm, LANE), lambda b, i: (b, i, O_KR // LANE)),
                  pl.BlockSpec((1, MLA_Q_RANK), lambda b, i: (0, 0)),
                  pl.BlockSpec((1, MLA_KV_RANK), lambda b, i: (0, 0)),
                  pl.BlockSpec((MLA_Q_RANK, nq), lambda b, i: (0, 0)),
                  pl.BlockSpec((MLA_KV_RANK, nkv), lambda b, i: (0, 0)),
                  pl.BlockSpec((tm, LANE), lambda b, i: (i, 0)),
                  pl.BlockSpec((tm, LANE), lambda b, i: (i, 0))],
        out_specs=[pl.BlockSpec((None, tm, nq), lambda b, i: (b, i, 0)),
                   pl.BlockSpec((None, tm, nkv), lambda b, i: (b, i, 0)),
                   pl.BlockSpec((None, tm, LANE), lambda b, i: (b, i, 0))],
        out_shape=[jax.ShapeDtypeStruct((B, Lc, nq), bf16),
                   jax.ShapeDtypeStruct((B, Lc, nkv), bf16),
                   jax.ShapeDtypeStruct((B, Lc, LANE), bf16)],
        compiler_params=_cp("parallel", "parallel"),
        name="mla_up",
    )(p, p, p, q_norm.reshape(1, -1), kv_norm.reshape(1, -1), wq, wkv, cos, sin)


def _s5_finish_kernel(y_ref, u_ref, d_ref, w_ref, b_ref, o_ref):
    y = y_ref[...] + d_ref[...] * u_ref[...]
    y = 0.5 * y * (1.0 + jnp.tanh(math.sqrt(2.0 / math.pi) * (y + 0.044715 * (y * y * y))))
    z = jnp.dot(y.astype(bf16), w_ref[...], preferred_element_type=f32) + b_ref[...]
    o_ref[...] = (y * _sigmoid(z)).astype(o_ref.dtype)


def _s5_finish(y, u, d_skip, glu_w, glu_b):
    B, Lc, N = u.shape
    tm = _tile(Lc, 1088)
    row = lambda b, i: (b, i, 0)
    const = lambda b, i: (0, 0)
    return pl.pallas_call(
        _s5_finish_kernel,
        grid=(B, Lc // tm),
        in_specs=[pl.BlockSpec((None, tm, N), row), pl.BlockSpec((None, tm, N), row),
                  pl.BlockSpec((1, N), const), pl.BlockSpec((N, N), const), pl.BlockSpec((1, N), const)],
        out_specs=pl.BlockSpec((None, tm, N), row),
        out_shape=jax.ShapeDtypeStruct((B, Lc, N), bf16),
        compiler_params=_cp("parallel", "parallel"),
        name="s5_finish",
    )(y, u, d_skip.reshape(1, N), glu_w.astype(bf16), glu_b.reshape(1, N))


def _gdn_finish_kernel(of_ref, ob_ref, z_ref, w_ref, o_ref):
    o = of_ref[...] + ob_ref[...]
    z = z_ref[...]
    w = w_ref[...]
    for h in range(GDN_HEADS):
        sl = slice(h * GDN_DV, (h + 1) * GDN_DV)
        zz = z[:, sl]
        o_ref[:, sl] = (_rms(o[:, sl], w) * (zz * _sigmoid(zz))).astype(o_ref.dtype)


def _gdn_finish(of, ob, p, norm_w):
    B, Lc, N = of.shape
    tm = _tile(Lc, 1088)
    row = lambda b, i: (b, i, 0)
    return pl.pallas_call(
        _gdn_finish_kernel,
        grid=(B, Lc // tm),
        in_specs=[pl.BlockSpec((None, tm, N), row), pl.BlockSpec((None, tm, N), row),
                  pl.BlockSpec((None, tm, N), lambda b, i: (b, i, O_Z // N)),
                  pl.BlockSpec((1, GDN_DV), lambda b, i: (0, 0))],
        out_specs=pl.BlockSpec((None, tm, N), row),
        out_shape=jax.ShapeDtypeStruct((B, Lc, N), bf16),
        compiler_params=_cp("parallel", "parallel"),
        name="gdn_finish",
    )(of, ob, p, norm_w.reshape(1, GDN_DV))


def _layer_norm(y, g, b, eps=1e-5):
    mu = jnp.mean(y, axis=1, keepdims=True)
    d = y - mu
    var = jnp.mean(d * d, axis=1, keepdims=True)
    return d * lax.rsqrt(var + eps) * g + b


def _merge_kernel(x_ref, a1_ref, a2_ref, w1_ref, w2_ref, ml_ref, mc_ref, g_ref, b_ref, rh_ref, rl_ref,
                  xo_ref, u_ref, lg_ref, *, C, tm, D):
    i = pl.program_id(1)
    half = tm // 2
    spans = [(0, half), (half, tm)]

    def project(r0, r1):
        return (jnp.dot(a1_ref[r0:r1, :], w1_ref[...], preferred_element_type=f32)
                + jnp.dot(a2_ref[r0:r1, :], w2_ref[...], preferred_element_type=f32))

    def finish(o, r0, r1):
        gate, shift, scale = _select_mod(ml_ref[...], mc_ref[...], i * tm + r0, r1 - r0, C, D, (2, 3, 4))
        xn = _layer_norm(DEEPNORM_ALPHA * x_ref[r0:r1, :] + gate * o, g_ref[...], b_ref[...])
        xo_ref[r0:r1, :] = xn
        u = xn * (1.0 + scale) + shift
        u_ref[r0:r1, :] = u
        uh = u.astype(bf16)
        ul = (u - uh.astype(f32)).astype(bf16)
        rh = rh_ref[...]
        lg_ref[r0:r1, :] = (jnp.dot(uh, rh, preferred_element_type=f32) + jnp.dot(ul, rh, preferred_element_type=f32)
                            + jnp.dot(uh, rl_ref[...], preferred_element_type=f32))

    o_next = project(*spans[0])
    for n, (r0, r1) in enumerate(spans):
        o = o_next
        if n + 1 < len(spans):
            o_next = project(*spans[n + 1])
        finish(o, r0, r1)


def _merge(x, a1, a2, w1, w2, mods, ln_g, ln_b, r_hi, r_lo, *, C):
    B, Lc, D = x.shape
    K1, K2 = a1.shape[2], a2.shape[2]
    tm = _tile(Lc, 544, 32)
    nb = mods.shape[0] - 1
    row = lambda b, i: (b, i, 0)
    const = lambda b, i: (0, 0)
    kern = functools.partial(_merge_kernel, C=C, tm=tm, D=D)
    return pl.pallas_call(
        kern,
        grid=(B, Lc // tm),
        in_specs=[pl.BlockSpec((None, tm, D), row),
                  pl.BlockSpec((None, tm, K1), row),
                  pl.BlockSpec((None, tm, K2), row),
                  pl.BlockSpec((K1, D), const, pipeline_mode=pl.Buffered(1)),
                  pl.BlockSpec((K2, D), const, pipeline_mode=pl.Buffered(1)),
                  pl.BlockSpec((None, 1, 6 * D), lambda b, i: (b, 0, 0)),
                  pl.BlockSpec((None, 1, 6 * D), lambda b, i: (nb, 0, 0)),
                  pl.BlockSpec((1, D), const),
                  pl.BlockSpec((1, D), const),
                  pl.BlockSpec((D, LANE), const),
                  pl.BlockSpec((D, LANE), const)],
        out_specs=[pl.BlockSpec((None, tm, D), row),
                   pl.BlockSpec((None, tm, D), row),
                   pl.BlockSpec((None, tm, LANE), row)],
        out_shape=[jax.ShapeDtypeStruct((B, Lc, D), f32),
                   jax.ShapeDtypeStruct((B, Lc, D), f32),
                   jax.ShapeDtypeStruct((B, Lc, LANE), f32)],
        compiler_params=_cp("parallel", "parallel"),
        name="merge",
    )(x, a1, a2, w1, w2, mods, mods, ln_g.reshape(1, D), ln_b.reshape(1, D), r_hi, r_lo)


def _router_kernel(lg_ref, bias_ref, o_ref):
    lg = lg_ref[...]
    shape = lg.shape
    lane = lax.broadcasted_iota(jnp.int32, shape, 1)
    valid = lane < N_EXPERTS
    neg = -jnp.inf
    scores = _sigmoid(lg)
    biased = jnp.where(valid, scores + bias_ref[...], neg)

    def first_argmax(v):
        m = jnp.max(v, axis=1, keepdims=True)
        idx = jnp.min(jnp.where(v == m, lane, LANE), axis=1, keepdims=True)
        return m, idx

    best_score = None
    best_group = None
    for g in range(N_GROUPS):
        in_g = jnp.logical_and(lane >= g * EXPERTS_PER_GROUP, lane < (g + 1) * EXPERTS_PER_GROUP)
        vals = jnp.where(in_g, biased, neg)
        m1, i1 = first_argmax(vals)
        m2, _ = first_argmax(jnp.where(lane == i1, neg, vals))
        gs = m1 + m2
        if g == 0:
            best_score, best_group = gs, jnp.zeros_like(i1)
        else:
            better = gs > best_score
            best_score = jnp.where(better, gs, best_score)
            best_group = jnp.where(better, g, best_group)
    in_best = jnp.logical_and(lane >= best_group * EXPERTS_PER_GROUP, lane < (best_group + 1) * EXPERTS_PER_GROUP)
    vals = jnp.where(in_best, biased, neg)
    _, e0 = first_argmax(vals)
    _, e1 = first_argmax(jnp.where(lane == e0, neg, vals))
    w0 = jnp.sum(jnp.where(lane == e0, scores, 0.0), axis=1, keepdims=True)
    w1 = jnp.sum(jnp.where(lane == e1, scores, 0.0), axis=1, keepdims=True)
    tot = w0 + w1
    out = jnp.where(lane == 0, e0.astype(f32), jnp.where(lane == 1, e1.astype(f32),
                    jnp.where(lane == 2, w0 / tot, jnp.where(lane == 3, w1 / tot, 0.0))))
    o_ref[...] = out


def _router(logits, bias_row):
    T = logits.shape[0]
    tm = _tile(T, 1088, 8)
    return pl.pallas_call(
        _router_kernel,
        grid=(T // tm,),
        in_specs=[pl.BlockSpec((tm, LANE), lambda i: (i, 0)), pl.BlockSpec((1, LANE), lambda i: (0, 0))],
        out_specs=pl.BlockSpec((tm, LANE), lambda i: (i, 0)),
        out_shape=jax.ShapeDtypeStruct((T, LANE), f32),
        compiler_params=_cp("parallel"),
        name="router",
    )(logits, bias_row)


MOE_CODE_BITS = 16


def _moe_kernel(te_ref, na_ref, code_ref, u_hbm, wg_ref, wu_ref, wd_ref, y_hbm, xbuf, ybuf, gsem, ssem, *, tm):
    i = pl.program_id(0)
    na = na_ref[0]
    slot = i % 2

    def gather_row(tile, sl, r):
        tok = code_ref[tile * tm + r] >> MOE_CODE_BITS
        pltpu.make_async_copy(u_hbm.at[pl.ds(tok, 1), :], xbuf.at[sl, pl.ds(r, 1), :], gsem.at[sl]).start()

    def wait_gather(sl):
        pltpu.make_async_copy(u_hbm.at[pl.ds(0, tm), :], xbuf.at[sl], gsem.at[sl]).wait()

    def wait_scatter(sl):
        pltpu.make_async_copy(ybuf.at[sl], y_hbm.at[pl.ds(0, tm), :], ssem.at[sl]).wait()

    @pl.when(i < na)
    def _():
        @pl.when(i == 0)
        def _():
            lax.fori_loop(0, tm, lambda r, c: (gather_row(0, 0, r), c)[1], 0)
            ybuf[1] = jnp.zeros(ybuf.shape[1:], f32)
            spare = pltpu.make_async_copy(ybuf.at[1], y_hbm.at[pl.ds(y_hbm.shape[0] - tm, tm), :], ssem.at[1])
            spare.start()
            spare.wait()

        wait_gather(slot)

        @pl.when(i >= 2)
        def _():
            wait_scatter(slot)

        nxt = jnp.minimum(i + 1, na - 1)
        for r in range(tm):
            gather_row(nxt, 1 - slot, r)
        x = xbuf[slot].astype(bf16)
        g = jnp.dot(x, wg_ref[...], preferred_element_type=f32)
        u = jnp.dot(x, wu_ref[...], preferred_element_type=f32)
        h = (g * _sigmoid(g) * u).astype(bf16)
        ybuf[slot] = jnp.dot(h, wd_ref[...], preferred_element_type=f32)
        for r in range(tm):
            dst = code_ref[i * tm + r] & ((1 << MOE_CODE_BITS) - 1)
            pltpu.make_async_copy(ybuf.at[slot, pl.ds(r, 1), :], y_hbm.at[pl.ds(dst, 1), :], ssem.at[slot]).start()

        @pl.when(i == na - 1)
        def _():
            wait_gather(1 - slot)
            wait_scatter(slot)

            @pl.when(na >= 2)
            def _():
                wait_scatter(1 - slot)


def _moe_experts(u, code, tile_expert, n_active, wg, wu, wd, layer, n_out_rows):
    T, D = u.shape
    F = wg.shape[3]
    tm = MOE_TM
    n_tiles = code.shape[0] // tm
    grid_spec = pltpu.PrefetchScalarGridSpec(
        num_scalar_prefetch=3,
        grid=(n_tiles,),
        in_specs=[pl.BlockSpec(memory_space=pl.ANY),
                  pl.BlockSpec((None, None, D, F), lambda i, te, na, cd: (layer, te[i], 0, 0)),
                  pl.BlockSpec((None, None, D, F), lambda i, te, na, cd: (layer, te[i], 0, 0)),
                  pl.BlockSpec((None, None, F, D), lambda i, te, na, cd: (layer, te[i], 0, 0))],
        out_specs=pl.BlockSpec(memory_space=pl.ANY),
        scratch_shapes=[pltpu.VMEM((2, tm, D), f32), pltpu.VMEM((2, tm, D), f32),
                        pltpu.SemaphoreType.DMA((2,)), pltpu.SemaphoreType.DMA((2,))],
    )
    return pl.pallas_call(
        functools.partial(_moe_kernel, tm=tm),
        grid_spec=grid_spec,
        out_shape=jax.ShapeDtypeStruct((n_out_rows, D), f32),
        compiler_params=_cp("arbitrary"),
        name="moe_experts",
    )(tile_expert, n_active, code, u, wg, wu, wd)


def _final_ln_kernel(x_ref, y0_ref, y1_ref, r_ref, ml_ref, mc_ref, g_ref, b_ref, o_ref, *, C, tm, D):
    i = pl.program_id(1)
    (gate,) = _select_mod(ml_ref[...], mc_ref[...], i * tm, tm, C, D, (5,))
    r = r_ref[...]
    f = r[:, 2:3] * y0_ref[...] + r[:, 3:4] * y1_ref[...]
    o_ref[...] = _layer_norm(DEEPNORM_ALPHA * x_ref[...] + gate * f, g_ref[...], b_ref[...])


def _final_ln(x, y2, r, mods, ln_g, ln_b, *, C):
    B, Lc, D = x.shape
    tm = _tile(Lc, 544)
    nI = Lc // tm
    nb = mods.shape[0] - 1
    row = lambda b, i: (b, i, 0)
    kern = functools.partial(_final_ln_kernel, C=C, tm=tm, D=D)
    return pl.pallas_call(
        kern,
        grid=(B, nI),
        in_specs=[pl.BlockSpec((None, tm, D), row),
                  pl.BlockSpec((tm, D), lambda b, i: (b * nI + i, 0)),
                  pl.BlockSpec((tm, D), lambda b, i: (B * nI + b * nI + i, 0)),
                  pl.BlockSpec((tm, LANE), lambda b, i: (b * nI + i, 0)),
                  pl.BlockSpec((None, 1, 6 * D), lambda b, i: (b, 0, 0)),
                  pl.BlockSpec((None, 1, 6 * D), lambda b, i: (nb, 0, 0)),
                  pl.BlockSpec((1, D), lambda b, i: (0, 0)), pl.BlockSpec((1, D), lambda b, i: (0, 0))],
        out_specs=pl.BlockSpec((None, tm, D), row),
        out_shape=jax.ShapeDtypeStruct((B, Lc, D), f32),
        compiler_params=_cp("parallel", "parallel"),
        name="final_ln",
    )(x, y2, y2, r, mods, mods, ln_g.reshape(1, D), ln_b.reshape(1, D))


S5_PAIRS = S5_GROUPS // 2
S5_TILE = 8


def _s5_pair_tables(lam_re, lam_im, b_re, b_im, c_re, c_im, log_dt):
    T1, G, P, J = S5_BLOCK, S5_GROUPS, S5_STATE, S5_GROUP
    n = T1 * J
    t_idx = jnp.arange(T1)
    lag = t_idx[None, :] - t_idx[:, None]
    toep = 0.0
    bc_secs, cc_secs, a_secs = [], [], []
    for d in range(2):
        tb = _s5_tables_dir(lam_re[d], lam_im[d], b_re[d], b_im[d], c_re[d], c_im[d], log_dt[d])
        kern, pw, b_bar, c, lam_dt = tb
        use = (lag >= 0) if d == 0 else (lag <= 0)
        kk = jnp.transpose(kern[jnp.clip(jnp.abs(lag), 0, T1 - 1)], (2, 0, 4, 1, 3))
        toep = toep + jnp.where(use[None, :, None, :, None], kk, 0.0).reshape(G, n, n)
        p_in = pw[T1 - 1 - t_idx] if d == 0 else pw[t_idx]
        bc = jnp.transpose(p_in[:, :, :, None] * b_bar[None], (1, 0, 3, 2)).reshape(G, n, P)
        p_out = pw[t_idx + 1] if d == 0 else pw[T1 - t_idx]
        cc = jnp.transpose(c[None] * p_out[:, :, None, :], (1, 3, 0, 2)).reshape(G, P, n)
        bc_secs += [bc.real, bc.imag]
        cc_secs += [cc.real, -cc.imag]
        for k in range(1, S5_TILE + 1):
            ak = jnp.exp(lam_dt * float(T1 * k))
            a_secs += [ak.real, ak.imag]
    place = lambda outer, inner: np.stack([
        np.kron(np.eye(outer), np.kron(np.eye(2)[:, g2:g2 + 1], np.eye(inner))) for g2 in range(2)]).astype(np.float32)
    tok, sec = place(T1, J), place(4, P)
    put = lambda rows, t, cols: jnp.einsum('xra,pxab,xcb->prc', rows, t.reshape(S5_PAIRS, 2, n, n), cols)
    tp = put(tok, toep, tok)
    bc = put(tok, jnp.stack(bc_secs, axis=2), sec)
    cc = put(sec, jnp.stack(cc_secs, axis=1), tok)
    apw = jnp.stack(a_secs, axis=0).reshape(2, S5_TILE, 2, S5_PAIRS, 2 * P)
    apw = jnp.transpose(apw, (3, 0, 1, 2, 4)).reshape(S5_PAIRS, 4 * S5_TILE, 2 * P)
    order = (jnp.arange(S5_TILE), jnp.arange(S5_TILE - 1, -1, -1))
    atile = jnp.stack([apw[:, d * 2 * S5_TILE + 2 * order[d] + ri] for d in range(2) for ri in range(2)], axis=1)
    return tp.astype(bf16), bc.astype(bf16), cc.astype(bf16), apw, atile.reshape(S5_PAIRS, 4 * S5_TILE, 2 * P)


def _s5_tables_dir(lam_re, lam_im, b_re, b_im, c_re, c_im, log_dt):
    T1 = S5_BLOCK
    lam = lax.complex(lam_re.astype(f32), lam_im.astype(f32))
    lam_dt = lam * jnp.exp(log_dt.astype(f32))[:, None]
    lam_bar = jnp.exp(lam_dt)
    b_bar = ((lam_bar - 1.0) / lam)[..., None] * lax.complex(b_re.astype(f32), b_im.astype(f32))
    c = lax.complex(c_re.astype(f32), c_im.astype(f32))
    pw = jnp.exp(lam_dt[None] * jnp.arange(T1 + 1, dtype=f32)[:, None, None])
    kern = jnp.einsum('gip,tgp,gpj->tgij', c, pw[:T1], b_bar, precision=lax.Precision.HIGHEST).real
    return kern, pw, b_bar, c, lam_dt


def _s5_kernel(s_ref, toep_ref, bc_ref, cc_ref, apw_ref, atile_ref, y_ref, *, nb, nb_ctx):
    T1, W, NT = S5_BLOCK, 2 * S5_GROUP, S5_TILE
    Q = LANE // W
    nt, nt_ctx = nb // NT, nb_ctx // NT
    slabs = [s_ref[pl.ds(t, nb, stride=T1), :] for t in range(T1)]
    grp = lax.broadcasted_iota(jnp.int32, (nb, LANE), 1) // W
    row = lax.broadcasted_iota(jnp.int32, (nb, LANE), 0)
    row8 = row % NT
    roll_rows = lambda x, k: pltpu.roll(x, k % nb, axis=0)
    out = [None] * T1
    for q in range(Q):
        cols = []
        for v in range(T1 // Q):
            acc = None
            for r in range(Q):
                shift = ((r - q) * W) % LANE
                piece = slabs[Q * v + r] if shift == 0 else pltpu.roll(slabs[Q * v + r], shift, axis=1)
                acc = piece if acc is None else jnp.where(grp == r, piece, acc)
            cols.append(acc)
        u = jnp.concatenate(cols, axis=1).astype(bf16)
        gin = jnp.dot(u, bc_ref[q], preferred_element_type=f32)
        states = []
        for d in range(2):
            apow = lambda k, ri: apw_ref[q, d * 2 * NT + 2 * (k - 1) + ri:d * 2 * NT + 2 * (k - 1) + ri + 1, :]
            hr, hi = gin[:, 2 * d * LANE:(2 * d + 1) * LANE], gin[:, (2 * d + 1) * LANE:(2 * d + 2) * LANE]
            for k in (1, 2, 4):
                sr, si = (roll_rows(hr, k), roll_rows(hi, k)) if d == 0 else (roll_rows(hr, -k), roll_rows(hi, -k))
                keep = (row8 >= k) if d == 0 else (row8 < NT - k)
                sr, si = jnp.where(keep, sr, 0.0), jnp.where(keep, si, 0.0)
                ar, ai = apow(k, 0), apow(k, 1)
                hr, hi = hr + ar * sr - ai * si, hi + ar * si + ai * sr
            a8r, a8i = apow(NT, 0), apow(NT, 1)
            order = range(nt) if d == 0 else list(range(nt_ctx - 1, -1, -1)) + list(range(nt - 1, nt_ctx - 1, -1))
            end = NT - 1 if d == 0 else 0
            cr = ci = jnp.zeros((1, LANE), f32)
            enter_r, enter_i = [None] * nt, [None] * nt
            for t in order:
                enter_r[t], enter_i[t] = cr, ci
                er, ei = hr[t * NT + end:t * NT + end + 1, :], hi[t * NT + end:t * NT + end + 1, :]
                cr, ci = a8r * cr - a8i * ci + er, a8r * ci + a8i * cr + ei
            cfr = jnp.concatenate([jnp.broadcast_to(c, (NT, LANE)) for c in enter_r], axis=0)
            cfi = jnp.concatenate([jnp.broadcast_to(c, (NT, LANE)) for c in enter_i], axis=0)
            tr = jnp.concatenate([atile_ref[q, 2 * d * NT:(2 * d + 1) * NT, :]] * nt, axis=0)
            ti = jnp.concatenate([atile_ref[q, (2 * d + 1) * NT:(2 * d + 2) * NT, :]] * nt, axis=0)
            hr, hi = hr + tr * cfr - ti * cfi, hi + tr * cfi + ti * cfr
            pr, pi = (roll_rows(hr, 1), roll_rows(hi, 1)) if d == 0 else (roll_rows(hr, -1), roll_rows(hi, -1))
            first = (row == 0) if d == 0 else (row == nb_ctx - 1)
            states += [jnp.where(first, 0.0, pr), jnp.where(first, 0.0, pi)]
        hp = jnp.concatenate(states, axis=1).astype(bf16)
        y = (jnp.dot(u, toep_ref[q], preferred_element_type=f32)
             + jnp.dot(hp, cc_ref[q], preferred_element_type=f32))
        for t in range(T1):
            v, r = divmod(t, Q)
            shift = ((q - r) * W) % LANE
            col = y[:, v * LANE:(v + 1) * LANE]
            piece = col if shift == 0 else pltpu.roll(col, shift, axis=1)
            out[t] = piece if out[t] is None else jnp.where(grp == q, piece, out[t])
    for t in range(T1):
        y_ref[pl.ds(t, nb, stride=T1), :] = out[t]


def _s5_mixer(s, C, lam_re, lam_im, b_re, b_im, c_re, c_im, log_dt):
    B, Lc, N = s.shape
    T1 = S5_BLOCK
    assert C % (T1 * S5_TILE) == 0 and Lc % (T1 * S5_TILE) == 0
    nb = Lc // T1
    n2 = 2 * T1 * S5_GROUP
    Q = LANE // (2 * S5_GROUP)
    toep, bc, cc, apw, atile = _s5_pair_tables(lam_re, lam_im, b_re, b_im, c_re, c_im, log_dt)
    kern = functools.partial(_s5_kernel, nb=nb, nb_ctx=C // T1)
    tab = lambda rows, cols: pl.BlockSpec((Q, rows, cols), lambda cb, b: (cb, 0, 0))
    return pl.pallas_call(
        kern,
        grid=(N // LANE, B),
        in_specs=[pl.BlockSpec((None, Lc, LANE), lambda cb, b: (b, 0, cb)),
                  tab(n2, n2), tab(n2, 4 * LANE), tab(4 * LANE, n2), tab(4 * S5_TILE, LANE), tab(4 * S5_TILE, LANE)],
        out_specs=pl.BlockSpec((None, Lc, LANE), lambda cb, b: (b, 0, cb)),
        out_shape=jax.ShapeDtypeStruct((B, Lc, N), f32),
        compiler_params=_cp("parallel", "parallel"),
        name="s5_scan",
    )(s, toep, bc, cc, apw, atile)


def _dotT(a, b):
    return lax.dot_general(a.astype(bf16), b.astype(bf16), (((1,), (1,)), ((), ())), preferred_element_type=f32)


def _dot(a, b):
    return jnp.dot(a.astype(bf16), b.astype(bf16), preferred_element_type=f32)


def _split(a):
    hi = a.astype(bf16)
    return hi, (a - hi.astype(f32)).astype(bf16)


def _dot_split(a, b):
    n = a.shape[1]
    ah, al = _split(a)
    bh, bl = _split(b)
    lhs = jnp.concatenate([ah, al], axis=1)
    rhs = jnp.concatenate([jnp.concatenate([bh, bl], axis=1),
                           jnp.concatenate([bh, jnp.zeros_like(bl)], axis=1)], axis=0)
    out = jnp.dot(lhs, rhs, preferred_element_type=f32)
    return out[:, :n] + out[:, n:]


def _gdn_chunk_index(n, n_ctx, n_all, reverse):
    if not reverse:
        return n
    return jnp.where(n < n_ctx, n_ctx - 1 - n, n_all - 1 - (n - n_ctx))


def _gdn_kernel(x_ref, xp_ref, xn_ref, gt_ref, cw_ref, alog_ref, dtb_ref, o_ref, s_scr, *, C, Lc, Cn, reverse):
    n = pl.program_id(1)
    n_ctx, n_all = C // Cn, Lc // Cn
    ci = _gdn_chunk_index(n, n_ctx, n_all, reverse)
    H, dk, dv = GDN_HEADS, GDN_DK, GDN_DV
    d = 1 if reverse else 0

    @pl.when(n == 0)
    def _():
        s_scr[...] = jnp.zeros_like(s_scr)

    x = x_ref[...]
    loc = lax.broadcasted_iota(jnp.int32, (Cn, 1), 0)
    row = ci * Cn + loc
    prev = jnp.where(loc == 0, xp_ref[7:8, :], pltpu.roll(x, 1, axis=0))
    prev = jnp.where(jnp.logical_or(row == 0, row == C), 0.0, prev)
    nxt = jnp.where(loc == Cn - 1, xn_ref[0:1, :], pltpu.roll(x, Cn - 1, axis=0))
    nxt = jnp.where(jnp.logical_or(row == C - 1, row == Lc - 1), 0.0, nxt)
    y = prev * cw_ref[0:1, :] + x * cw_ref[1:2, :] + nxt * cw_ref[2:3, :]
    y = y * _sigmoid(y)

    graw = gt_ref[...]
    z = graw + dtb_ref[...]
    g_all = -jnp.exp(alog_ref[...]) * (jnp.maximum(z, 0.0) + jnp.log(1.0 + jnp.exp(-jnp.abs(z))))
    beta_all = _sigmoid(graw)

    ii = lax.broadcasted_iota(jnp.int32, (Cn, Cn), 0)
    jj = lax.broadcasted_iota(jnp.int32, (Cn, Cn), 1)
    incl = (ii <= jj) if reverse else (ii >= jj)
    strict = (ii < jj) if reverse else (ii > jj)
    eye = (ii == jj).astype(f32)
    tri = incl.astype(bf16)
    g_hi = g_all.astype(bf16)
    g_lo = (g_all - g_hi.astype(f32)).astype(bf16)
    gc_all = jnp.dot(tri, g_hi, preferred_element_type=f32) + jnp.dot(tri, g_lo, preferred_element_type=f32)
    gc_all_t = gc_all.T
    last = 0 if reverse else Cn - 1

    heads = range(H)
    gc = [gc_all[:, 8 * d + h:8 * d + h + 1] for h in heads]
    gl = [gc_all[last:last + 1, 8 * d + h:8 * d + h + 1] for h in heads]
    beta = [beta_all[:, 16 + 8 * d + h:17 + 8 * d + h] for h in heads]
    decay = [jnp.where(incl, jnp.exp(jnp.where(incl, gc[h] - gc_all_t[8 * d + h:8 * d + h + 1, :], 0.0)), 0.0)
             for h in heads]
    l2 = lambda t: t * lax.rsqrt(jnp.sum(t * t, axis=1, keepdims=True) + 1e-6)
    q = [l2(y[:, h * dk:(h + 1) * dk]) * (dk ** -0.5) for h in heads]
    k = [l2(y[:, H * dk + h * dk:H * dk + (h + 1) * dk]) for h in heads]
    v = [y[:, 2 * H * dk + h * dv:2 * H * dk + (h + 1) * dv] for h in heads]
    kb = [k[h] * beta[h] for h in heads]
    kq = [_dotT(jnp.concatenate([kb[h], q[h]], axis=0), k[h]) for h in heads]
    m = [jnp.where(strict, kq[h][:Cn] * decay[h], 0.0) for h in heads]
    a = [jnp.where(incl, kq[h][Cn:] * decay[h], 0.0) for h in heads]
    t_inv = [eye - m[h] for h in heads]
    pw = m
    for _ in range(int(math.log2(Cn)) - 1):
        pw = [_dot_split(pw[h], pw[h]) for h in heads]
        t_inv = [t_inv[h] + _dot_split(t_inv[h], pw[h]) for h in heads]
    e_gc = [jnp.exp(gc[h]) for h in heads]
    uw = [_dot(t_inv[h], jnp.concatenate([v[h] * beta[h], kb[h] * e_gc[h]], axis=1)) for h in heads]
    s_old = [s_scr[h] for h in heads]
    r = [_dot(jnp.concatenate([uw[h][:, dv:], q[h] * e_gc[h]], axis=0), s_old[h]) for h in heads]
    v_new = [uw[h][:, :dv] - r[h][:Cn] for h in heads]
    for h in heads:
        o_ref[:, h * dv:(h + 1) * dv] = r[h][Cn:] + _dot(a[h], v_new[h])
    for h in heads:
        kd = k[h] * jnp.exp(gl[h] - gc[h])
        s_scr[h] = s_old[h] * jnp.exp(gl[h]) + lax.dot_general(
            kd.astype(bf16), v_new[h].astype(bf16), (((0,), (0,)), ((), ())), preferred_element_type=f32)


def _gdn_direction(p, conv_w, alog_row, dtb_row, *, C, reverse):
    B, Lc, _ = p.shape
    Cn = LANE
    n_ctx, n_all = C // Cn, Lc // Cn
    nq = GDN_QKV
    ci = lambda n: _gdn_chunk_index(n, n_ctx, n_all, reverse)
    sub = Cn // 8
    kern = functools.partial(_gdn_kernel, C=C, Lc=Lc, Cn=Cn, reverse=reverse)
    return pl.pallas_call(
        kern,
        grid=(B, n_all),
        in_specs=[pl.BlockSpec((None, Cn, nq), lambda b, n: (b, ci(n), 0)),
                  pl.BlockSpec((None, 8, nq), lambda b, n: (b, jnp.maximum(ci(n) * sub - 1, 0), 0)),
                  pl.BlockSpec((None, 8, nq), lambda b, n: (b, jnp.minimum((ci(n) + 1) * sub, Lc // 8 - 1), 0)),
                  pl.BlockSpec((None, Cn, LANE), lambda b, n: (b, ci(n), O_GATE // LANE)),
                  pl.BlockSpec((3, nq), lambda b, n: (0, 0)),
                  pl.BlockSpec((1, LANE), lambda b, n: (0, 0)),
                  pl.BlockSpec((1, LANE), lambda b, n: (0, 0))],
        out_specs=pl.BlockSpec((None, Cn, GDN_HEADS * GDN_DV), lambda b, n: (b, ci(n), 0)),
        out_shape=jax.ShapeDtypeStruct((B, Lc, GDN_HEADS * GDN_DV), f32),
        scratch_shapes=[pltpu.VMEM((GDN_HEADS, GDN_DK, GDN_DV), f32)],
        compiler_params=_cp("parallel", "arbitrary"),
        name="gdn_rev" if reverse else "gdn_fwd",
    )(p, p, p, p, conv_w, alog_row, dtb_row)


def _gdn_pallas(p, C, conv_w, a_log, dt_bias):
    outs = []
    for d in range(2):
        pad = lambda t: jnp.zeros((1, LANE), f32).at[0, 8 * d:8 * d + GDN_HEADS].set(t[d].astype(f32))
        outs.append(_gdn_direction(p, conv_w, pad(a_log), pad(dt_bias), C=C, reverse=(d == 1)))
    return outs


def _dispatch_plan(e_idx, n_tiles):
    T = e_idx.shape[0]
    tm = MOE_TM
    assert 2 * T + tm <= (1 << MOE_CODE_BITS) and T < (1 << (31 - MOE_CODE_BITS))
    flat = e_idx.reshape(-1)
    onehot = (flat[:, None] == jnp.arange(N_EXPERTS)[None, :]).astype(jnp.int32)
    rank = jnp.sum((jnp.cumsum(onehot, axis=0) - onehot) * onehot, axis=1)
    counts = jnp.sum(onehot, axis=0)
    padded = ((counts + tm - 1) // tm) * tm
    ends = jnp.cumsum(padded)
    offs = ends - padded
    pos = offs[flat] + rank
    slot_id = jnp.arange(2 * T, dtype=jnp.int32)
    tok, k = slot_id // 2, slot_id % 2
    spare = 2 * T + jnp.arange(n_tiles * tm, dtype=jnp.int32) % tm
    code = spare.at[pos].set((tok << MOE_CODE_BITS) | (k * T + tok))
    tile_start = jnp.arange(n_tiles, dtype=jnp.int32) * tm
    tile_expert = jnp.minimum(jnp.sum((tile_start[:, None] >= ends[None, :]).astype(jnp.int32), axis=1), N_EXPERTS - 1)
    n_active = (ends[-1] // tm).astype(jnp.int32).reshape(1)
    last_e = tile_expert[jnp.maximum(n_active[0] - 1, 0)]
    tile_expert = jnp.where(tile_start < ends[-1], tile_expert, last_e).astype(jnp.int32)
    return code, tile_expert, n_active


def _moe(u, logits, bias_row, wg, wu, wd, layer):
    B, Lc, D = u.shape
    T = B * Lc
    r = _router(logits.reshape(T, LANE), bias_row)
    n_tiles = (2 * T + MOE_TM - 1) // MOE_TM + N_EXPERTS
    code, tile_expert, n_active = _dispatch_plan(r[:, 0:2].astype(jnp.int32), n_tiles)
    return _moe_experts(u.reshape(T, D), code, tile_expert, n_active, wg, wu, wd, layer, 2 * T + MOE_TM), r


def _rope_tables(L, C):
    rows = L // GRID_W
    t_row = jnp.repeat(jnp.arange(rows), GRID_W).astype(f32)
    t_col = jnp.tile(jnp.arange(GRID_W), rows).astype(f32)
    n_freq = ROPE_DIM // 4
    inv_freq = ROPE_BASE ** (-jnp.arange(n_freq, dtype=f32) / n_freq)
    ang = jnp.concatenate([t_row[:, None] * inv_freq, t_col[:, None] * inv_freq], axis=-1)
    cos = jnp.repeat(jnp.cos(ang), 2, axis=1)
    sin = jnp.repeat(jnp.sin(ang), 2, axis=1) * jnp.tile(jnp.array([-1.0, 1.0], f32), ROPE_DIM // 2)
    one = jnp.ones((L, ROPE_DIM), f32)
    zero = jnp.zeros((L, ROPE_DIM), f32)
    ctx = lambda t, fill: jnp.concatenate([jnp.full((C, LANE), fill, f32), t], axis=0)
    return (ctx(jnp.concatenate([cos, cos], 1), 1.0), ctx(jnp.concatenate([sin, sin], 1), 0.0),
            ctx(jnp.concatenate([cos, one], 1), 1.0), ctx(jnp.concatenate([sin, zero], 1), 0.0))


def _prep_even_w_in(w):
    D = w.shape[0]
    nqk = 2 * DA_HEADS * DA_HEAD_DIM
    heads_first = lambda t: t.reshape(D, 2, DA_HEADS, DA_HEAD_DIM).transpose(0, 2, 1, 3).reshape(D, nqk)
    q = heads_first(w[:, :nqk]) * (DA_HEAD_DIM ** -0.5)
    k = heads_first(w[:, nqk:2 * nqk])
    return jnp.concatenate([q, k, w[:, 2 * nqk:]], axis=1).astype(bf16)


def _prep_odd_w_in(w):
    D = w.shape[0]
    z = lambda n: jnp.zeros((D, n), w.dtype)
    c0 = MLA_Q_RANK
    c1 = c0 + MLA_KV_RANK
    c2 = c1 + MLA_ROPE
    c3 = c2 + GDN_QKV
    c4 = c3 + GDN_HEADS * GDN_DV
    parts = [w[:, c2:c3], w[:, c3:c4], w[:, :c0], w[:, c0:c1], w[:, c1:c2], z(LANE - MLA_ROPE),
             w[:, c4:], z(LANE - 4 * GDN_HEADS)]
    out = jnp.concatenate(parts, axis=1)
    assert out.shape[1] == O_IN
    return out.astype(bf16)


def _prep_mla_w(w_uq, w_ukv):
    H = MLA_HEADS
    rq = w_uq.shape[0]
    wq = w_uq.reshape(rq, H, MLA_NOPE + MLA_ROPE)
    nope = wq[:, :, :MLA_NOPE].reshape(rq, H * MLA_NOPE)
    rope = jnp.concatenate([wq[:, :, MLA_NOPE:], jnp.zeros((rq, H, LANE - MLA_ROPE), w_uq.dtype)], axis=2)
    wq_p = jnp.concatenate([nope, rope.reshape(rq, H * LANE)], axis=1).astype(bf16)
    rkv = w_ukv.shape[0]
    wkv = w_ukv.reshape(rkv, H, MLA_NOPE + MLA_V)
    wkv_p = jnp.concatenate([wkv[:, :, :MLA_NOPE].reshape(rkv, H * MLA_NOPE),
                             wkv[:, :, MLA_NOPE:].reshape(rkv, H * MLA_V)], axis=1).astype(bf16)
    return wq_p, wkv_p


def kernel(x, c, ctx, c_ctx, mod_w, mod_b, ln_g, ln_b, e_w_in, e_w_out, da_lam, da_subln, s5_lam_re, s5_lam_im, s5_b_re, s5_b_im, s5_c_re, s5_c_im, s5_log_dt, s5_d, s5_glu_w, s5_glu_b, o_w_in, o_w_out, mla_q_norm, mla_kv_norm, mla_w_uq, mla_w_ukv, gdn_conv, gdn_a_log, gdn_dt_bias, gdn_norm, router_w, router_bias, moe_w_gate, moe_w_up, moe_w_down):
    B, L, D = x.shape
    C = ctx.shape[1]
    depth = mod_w.shape[0]
    assert depth == DEPTH

    R = -(-(B + 1) // 8) * 8
    cc = jnp.concatenate([c, c_ctx[None, :], jnp.zeros((R - B - 1, D), f32)], axis=0)
    mods_all = _modulation(cc, mod_w, mod_b)[:, :B + 1].reshape(depth, B + 1, 1, 6 * D)

    cos_e, sin_e, cos_o, sin_o = _rope_tables(L, C)
    wg_all, wu_all, wd_all = moe_w_gate.astype(bf16), moe_w_up.astype(bf16), moe_w_down.astype(bf16)
    r_pad = jnp.concatenate([router_w.astype(f32), jnp.zeros((D, LANE - N_EXPERTS), f32)], axis=1)
    r_hi = r_pad.astype(bf16)
    r_lo = (r_pad - r_hi.astype(f32)).astype(bf16)
    bias_row = jnp.concatenate([router_bias.astype(f32), jnp.zeros((LANE - N_EXPERTS,), f32)]).reshape(1, LANE)

    xs = jnp.concatenate([ctx, x], axis=1)
    n_att = DA_HEADS * DA_V_DIM
    for l in range(depth):
        mods = mods_all[l]
        i = l // 2
        if l % 2 == 0:
            qkv, s = _inproj(xs, mods, _prep_even_w_in(e_w_in[i]), cos_e, sin_e, C=C,
                             n_rope_cols=4 * DA_HEADS * DA_HEAD_DIM, n_a_cols=3072, a_dtype=bf16, b_dtype=f32)
            a1 = _diff_attention(qkv, da_lam[i].astype(f32), da_subln[i].astype(f32), C=C,
                                 lam_init=0.8 - 0.6 * math.exp(-0.3 * l))
            y = _s5_mixer(s, C, s5_lam_re[i], s5_lam_im[i], s5_b_re[i], s5_b_im[i], s5_c_re[i], s5_c_im[i],
                          s5_log_dt[i])
            a2 = _s5_finish(y, s, s5_d[i], s5_glu_w[i], s5_glu_b[i])
            w_out = e_w_out[i].astype(bf16)
        else:
            (p,) = _inproj(xs, mods, _prep_odd_w_in(o_w_in[i]), cos_e, sin_e, C=C,
                           n_rope_cols=0, n_a_cols=O_IN, a_dtype=f32, b_dtype=f32)
            wq_p, wkv_p = _prep_mla_w(mla_w_uq[i], mla_w_ukv[i])
            q, kv, kr = _mla_up(p, mla_q_norm[i], mla_kv_norm[i], wq_p, wkv_p, cos_o, sin_o)
            a1 = _mla_attention(q, kv, kr, C=C)
            of, ob = _gdn_pallas(p, C, gdn_conv[i].astype(f32), gdn_a_log[i], gdn_dt_bias[i])
            a2 = _gdn_finish(of, ob, p, gdn_norm[i])
            w_out = o_w_out[i].astype(bf16)
        xs, u, logits = _merge(xs, a1, a2, w_out[:n_att], w_out[n_att:], mods, ln_g[l, 0], ln_b[l, 0],
                               r_hi, r_lo, C=C)
        y2, r = _moe(u, logits, bias_row, wg_all, wu_all, wd_all, l)
        xs = _final_ln(xs, y2, r, mods, ln_g[l, 1], ln_b[l, 1], C=C)
    return xs[:, C:, :]
```

```python
import functools
import math

import jax
import jax.numpy as jnp
import numpy as np
from jax import lax
from jax.experimental import pallas as pl
from jax.experimental.pallas import tpu as pltpu

f32 = jnp.float32
bf16 = jnp.bfloat16

GRID_W = 64
ROPE_DIM = 64
ROPE_BASE = 10000.0
DA_HEADS = 8
DA_HEAD_DIM = 64
DA_V_DIM = 128
S5_CHANNELS = 512
S5_GROUP = 16
S5_GROUPS = 32
S5_STATE = 64
S5_BLOCK = 16
MLA_HEADS = 8
MLA_Q_RANK = 512
MLA_KV_RANK = 256
MLA_NOPE = 128
MLA_ROPE = 64
MLA_V = 128
GDN_HEADS = 8
GDN_DK = 128
GDN_DV = 128
GDN_QKV = 3072
GDN_CHUNK = 64
N_EXPERTS = 16
N_GROUPS = 4
EXPERTS_PER_GROUP = 4
DEPTH = 4
DEEPNORM_ALPHA = (2 * DEPTH) ** 0.25
LANE = 128
MOE_TM = 256
VMEM_LIMIT = 56 * 1024 * 1024

O_QKV, O_Z, O_CQ, O_CKV, O_KR, O_GATE, O_IN = 0, 3072, 4096, 4608, 4864, 4992, 5120


def _cp(*sem):
    return pltpu.CompilerParams(dimension_semantics=sem, vmem_limit_bytes=VMEM_LIMIT)


def _tile(n, cap, mult=16):
    best = None
    for t in range(mult, min(n, cap) + 1, mult):
        if n % t == 0:
            best = t
    assert best is not None, (n, cap, mult)
    return best


def _sigmoid(x):
    return 1.0 / (1.0 + jnp.exp(-x))


def _mod_kernel(a_ref, w_ref, b_ref, o_ref):
    a = a_ref[...]
    act = (a * _sigmoid(a)).astype(bf16)
    o_ref[...] = jnp.dot(act, w_ref[...].astype(bf16), preferred_element_type=f32) + b_ref[...]


def _modulation(cc, mod_w, mod_b):
    depth, D, N = mod_w.shape
    R = cc.shape[0]
    tn = _tile(N, 1024, LANE)
    return pl.pallas_call(
        _mod_kernel,
        grid=(depth, N // tn),
        in_specs=[pl.BlockSpec((R, D), lambda l, j: (0, 0)),
                  pl.BlockSpec((None, D, tn), lambda l, j: (l, 0, j)),
                  pl.BlockSpec((None, 1, tn), lambda l, j: (l, 0, j))],
        out_specs=pl.BlockSpec((None, R, tn), lambda l, j: (l, 0, j)),
        out_shape=jax.ShapeDtypeStruct((depth, R, N), f32),
        compiler_params=_cp("parallel", "parallel"),
        name="modulation",
    )(cc, mod_w, mod_b.reshape(depth, 1, N))


def _select_mod(ml, mc, row0, tm, C, D, chunks):
    is_ctx = (row0 + lax.broadcasted_iota(jnp.int32, (tm, 1), 0)) < C
    return [jnp.where(is_ctx, mc[:, k * D:(k + 1) * D], ml[:, k * D:(k + 1) * D]) for k in chunks]


def _rope(seg, cos, sin):
    nxt = pltpu.roll(seg, LANE - 1, axis=1)
    prv = pltpu.roll(seg, 1, axis=1)
    even = (lax.broadcasted_iota(jnp.int32, seg.shape, 1) % 2) == 0
    return seg * cos + jnp.where(even, nxt, prv) * sin


def _inproj_kernel(x_ref, ml_ref, mc_ref, w_ref, cos_ref, sin_ref, *rest, C, tm, tn, D, n_rope, n_a, has_b):
    if has_b:
        oa_ref, ob_ref, u_scr = rest
    else:
        oa_ref, u_scr = rest
        ob_ref = None
    i = pl.program_id(1)
    j = pl.program_id(2)

    @pl.when(j == 0)
    def _():
        shift, scale = _select_mod(ml_ref[...], mc_ref[...], i * tm, tm, C, D, (0, 1))
        u_scr[...] = (x_ref[...] * (1.0 + scale) + shift).astype(bf16)

    half = tm // 2
    spans = [(0, half), (half, tm)]

    def emit(write):
        acc = lambda r0, r1: jnp.dot(u_scr[r0:r1, :], w_ref[...], preferred_element_type=f32)
        a_next = acc(*spans[0])
        for n, (r0, r1) in enumerate(spans):
            a = a_next
            if n + 1 < len(spans):
                a_next = acc(*spans[n + 1])
            write(a, r0, r1)

    def write_rope(a, r0, r1):
        cos = cos_ref[r0:r1, :]
        sin = sin_ref[r0:r1, :]
        for c in range(tn // LANE):
            oa_ref[r0:r1, c * LANE:(c + 1) * LANE] = _rope(a[:, c * LANE:(c + 1) * LANE], cos, sin).astype(oa_ref.dtype)

    def write_a(a, r0, r1):
        oa_ref[r0:r1, :] = a.astype(oa_ref.dtype)

    def write_b(a, r0, r1):
        ob_ref[r0:r1, :] = a.astype(ob_ref.dtype)

    if n_rope > 0:
        @pl.when(j < n_rope)
        def _():
            emit(write_rope)

    @pl.when(jnp.logical_and(j >= n_rope, j < n_a))
    def _():
        emit(write_a)

    if has_b:
        @pl.when(j >= n_a)
        def _():
            emit(write_b)


def _inproj(x, mods, w, cos, sin, *, C, n_rope_cols, n_a_cols, a_dtype, b_dtype):
    B, Lc, D = x.shape
    N = w.shape[1]
    tm = _tile(Lc, 1088)
    tn = 512
    assert N % tn == 0 and n_rope_cols % tn == 0 and n_a_cols % tn == 0
    n_a = n_a_cols // tn
    has_b = n_a_cols < N
    kern = functools.partial(_inproj_kernel, C=C, tm=tm, tn=tn, D=D, n_rope=n_rope_cols // tn, n_a=n_a, has_b=has_b)
    out_shape = [jax.ShapeDtypeStruct((B, Lc, n_a_cols), a_dtype)]
    out_specs = [pl.BlockSpec((None, tm, tn), lambda b, i, j: (b, i, jnp.minimum(j, n_a - 1)))]
    if has_b:
        out_shape.append(jax.ShapeDtypeStruct((B, Lc, N - n_a_cols), b_dtype))
        out_specs.append(pl.BlockSpec((None, tm, tn), lambda b, i, j: (b, i, jnp.maximum(j - n_a, 0))))
    nb = mods.shape[0] - 1
    return pl.pallas_call(
        kern,
        grid=(B, Lc // tm, N // tn),
        in_specs=[pl.BlockSpec((None, tm, D), lambda b, i, j: (b, i, 0)),
                  pl.BlockSpec((None, 1, 6 * D), lambda b, i, j: (b, 0, 0)),
                  pl.BlockSpec((None, 1, 6 * D), lambda b, i, j: (nb, 0, 0)),
                  pl.BlockSpec((D, tn), lambda b, i, j: (0, j)),
                  pl.BlockSpec((tm, LANE), lambda b, i, j: (i, 0)),
                  pl.BlockSpec((tm, LANE), lambda b, i, j: (i, 0))],
        out_specs=out_specs,
        out_shape=out_shape,
        scratch_shapes=[pltpu.VMEM((tm, D), bf16)],
        compiler_params=_cp("parallel", "parallel", "arbitrary"),
        name="inproj",
    )(x, mods, mods, w, cos, sin)


def _transpose_bf16(x):
    return x.astype(f32).T.astype(bf16)


LOG2E = math.log2(math.e)
ATTN_KEY_CHUNK = 1024


def _attend_t(ks, q_ts, vt_ref, nkeys, scale=None):
    n = len(ks)
    chunks = [(c0, min(c0 + ATTN_KEY_CHUNK, nkeys)) for c0 in range(0, nkeys, ATTN_KEY_CHUNK)]

    def scores(c):
        out = [jnp.dot(ks[i][c[0]:c[1]], q_ts[i], preferred_element_type=f32) for i in range(n)]
        return out if scale is None else [t * (scale * LOG2E) for t in out]

    m, l, acc = [None] * n, [None] * n, [None] * n
    s_next = scores(chunks[0])
    for j, (c0, c1) in enumerate(chunks):
        s_cur = s_next
        if j + 1 < len(chunks):
            s_next = scores(chunks[j + 1])
        v_c = vt_ref[:, c0:c1]
        for i in range(n):
            mc = jnp.max(s_cur[i], axis=0, keepdims=True)
            m_new = mc if m[i] is None else jnp.maximum(m[i], mc)
            p = jnp.exp2(s_cur[i] - m_new)
            pv = jnp.dot(v_c, p.astype(bf16), preferred_element_type=f32)
            if m[i] is None:
                l[i], acc[i] = jnp.sum(p, axis=0, keepdims=True), pv
            else:
                alpha = jnp.exp2(m[i] - m_new)
                l[i] = alpha * l[i] + jnp.sum(p, axis=0, keepdims=True)
                acc[i] = alpha * acc[i] + pv
            m[i] = m_new
    return [acc[i] / l[i] for i in range(n)]


def _diff_attn_kernel(q_ref, k_ref, v_ref, lam_ref, sub_ref, o_ref, vt_scr, *, C, tq, lam_init):
    qi = pl.program_id(2)

    @pl.when(qi == 0)
    def _():
        vt_scr[...] = _transpose_bf16(v_ref[...])

    lv = lam_ref[...]
    lam = (jnp.exp(jnp.sum(lv[0:1] * lv[1:2], axis=1, keepdims=True))
           - jnp.exp(jnp.sum(lv[2:3] * lv[3:4], axis=1, keepdims=True)) + lam_init)

    def run(nkeys):
        q_t = _transpose_bf16(q_ref[...])
        k = k_ref[0:nkeys, :]
        hd = DA_HEAD_DIM
        o1, o2 = _attend_t([k[:, 0:hd], k[:, hd:2 * hd]], [q_t[0:hd, :], q_t[hd:2 * hd, :]], vt_scr, nkeys)
        o = (o1 - lam * o2).T
        o = o * lax.rsqrt(jnp.mean(o * o, axis=1, keepdims=True) + 1e-6) * sub_ref[...]
        o_ref[...] = (o * (1.0 - lam_init)).astype(o_ref.dtype)

    @pl.when(qi * tq < C)
    def _():
        run(C)

    @pl.when(qi * tq >= C)
    def _():
        run(k_ref.shape[0])


def _diff_attention(qkv, da_lam, da_subln, *, C, lam_init):
    B, Lc, _ = qkv.shape
    H = DA_HEADS
    tq = _tile(C, 256)
    kern = functools.partial(_diff_attn_kernel, C=C, tq=tq, lam_init=lam_init)
    return pl.pallas_call(
        kern,
        grid=(B, H, Lc // tq),
        in_specs=[pl.BlockSpec((None, tq, LANE), lambda b, h, i: (b, i, h)),
                  pl.BlockSpec((None, Lc, LANE), lambda b, h, i: (b, 0, H + h)),
                  pl.BlockSpec((None, Lc, LANE), lambda b, h, i: (b, 0, 2 * H + h)),
                  pl.BlockSpec((4, DA_HEAD_DIM), lambda b, h, i: (0, 0)),
                  pl.BlockSpec((1, DA_V_DIM), lambda b, h, i: (0, 0))],
        out_specs=pl.BlockSpec((None, tq, LANE), lambda b, h, i: (b, i, h)),
        out_shape=jax.ShapeDtypeStruct((B, Lc, H * DA_V_DIM), bf16),
        scratch_shapes=[pltpu.VMEM((DA_V_DIM, Lc), bf16)],
        compiler_params=_cp("parallel", "parallel", "arbitrary"),
        name="diff_attention",
    )(qkv, qkv, qkv, da_lam, da_subln.reshape(1, DA_V_DIM))


def _mla_attn_kernel(qn_ref, qr_ref, kn_ref, kr_ref, v_ref, o_ref, vt_scr, *, C, tq, scale):
    qi = pl.program_id(2)

    @pl.when(qi == 0)
    def _():
        vt_scr[...] = _transpose_bf16(v_ref[...])

    def run(nkeys):
        q_t = jnp.concatenate([_transpose_bf16(qn_ref[...]), _transpose_bf16(qr_ref[...])], axis=0)
        k = jnp.concatenate([kn_ref[0:nkeys, :], kr_ref[0:nkeys, :]], axis=1)
        o_ref[...] = _attend_t([k], [q_t], vt_scr, nkeys, scale)[0].T.astype(o_ref.dtype)

    @pl.when(qi * tq < C)
    def _():
        run(C)

    @pl.when(qi * tq >= C)
    def _():
        run(kn_ref.shape[0])


def _mla_attention(q, kv, kr, *, C):
    B, Lc, _ = q.shape
    H = MLA_HEADS
    tq = _tile(C, 256)
    kern = functools.partial(_mla_attn_kernel, C=C, tq=tq, scale=(MLA_NOPE + MLA_ROPE) ** -0.5)
    return pl.pallas_call(
        kern,
        grid=(B, H, Lc // tq),
        in_specs=[pl.BlockSpec((None, tq, LANE), lambda b, h, i: (b, i, h)),
                  pl.BlockSpec((None, tq, LANE), lambda b, h, i: (b, i, H + h)),
                  pl.BlockSpec((None, Lc, LANE), lambda b, h, i: (b, 0, h)),
                  pl.BlockSpec((None, Lc, LANE), lambda b, h, i: (b, 0, 0)),
                  pl.BlockSpec((None, Lc, LANE), lambda b, h, i: (b, 0, H + h))],
        out_specs=pl.BlockSpec((None, tq, LANE), lambda b, h, i: (b, i, h)),
        out_shape=jax.ShapeDtypeStruct((B, Lc, H * MLA_V), bf16),
        scratch_shapes=[pltpu.VMEM((MLA_V, Lc), bf16)],
        compiler_params=_cp("parallel", "parallel", "arbitrary"),
        name="mla_attention",
    )(q, q, kv, kr, kv)


def _rms(x, w, eps=1e-6):
    return x * lax.rsqrt(jnp.mean(x * x, axis=1, keepdims=True) + eps) * w


def _mla_up_kernel(cq_ref, ckv_ref, kr_ref, qn_ref, kvn_ref, wq_ref, wkv_ref, cos_ref, sin_ref,
                   q_ref, kv_ref, kro_ref):
    nq = MLA_HEADS * MLA_NOPE
    cos = cos_ref[...]
    sin = sin_ref[...]
    q = jnp.dot(_rms(cq_ref[...], qn_ref[...]).astype(bf16), wq_ref[...], preferred_element_type=f32)
    q_ref[:, 0:nq] = q[:, 0:nq].astype(q_ref.dtype)
    for h in range(MLA_HEADS):
        c0 = nq + h * LANE
        q_ref[:, c0:c0 + LANE] = _rope(q[:, c0:c0 + LANE], cos, sin).astype(q_ref.dtype)
    kv = jnp.dot(_rms(ckv_ref[...], kvn_ref[...]).astype(bf16), wkv_ref[...], preferred_element_type=f32)
    kv_ref[...] = kv.astype(kv_ref.dtype)
    kro_ref[...] = _rope(kr_ref[...], cos, sin).astype(kro_ref.dtype)


def _mla_up(p, q_norm, kv_norm, wq, wkv, cos, sin):
    B, Lc, _ = p.shape
    tm = _tile(Lc, 544)
    nq = wq.shape[1]
    nkv = wkv.shape[1]
    return pl.pallas_call(
        _mla_up_kernel,
        grid=(B, Lc // tm),
        in_specs=[pl.BlockSpec((None, tm, MLA_Q_RANK), lambda b, i: (b, i, O_CQ // MLA_Q_RANK)),
                  pl.BlockSpec((None, tm, MLA_KV_RANK), lambda b, i: (b, i, O_CKV // MLA_KV_RANK)),
                  pl.BlockSpec((None, tm, LANE), lambda b, i: (b, i, O_KR // LANE)),
                  pl.BlockSpec((1, MLA_Q_RANK), lambda b, i: (0, 0)),
                  pl.BlockSpec((1, MLA_KV_RANK), lambda b, i: (0, 0)),
                  pl.BlockSpec((MLA_Q_RANK, nq), lambda b, i: (0, 0)),
                  pl.BlockSpec((MLA_KV_RANK, nkv), lambda b, i: (0, 0)),
                  pl.BlockSpec((tm, LANE), lambda b, i: (i, 0)),
                  pl.BlockSpec((tm, LANE), lambda b, i: (i, 0))],
        out_specs=[pl.BlockSpec((None, tm, nq), lambda b, i: (b, i, 0)),
                   pl.BlockSpec((None, tm, nkv), lambda b, i: (b, i, 0)),
                   pl.BlockSpec((None, tm, LANE), lambda b, i: (b, i, 0))],
        out_shape=[jax.ShapeDtypeStruct((B, Lc, nq), bf16),
                   jax.ShapeDtypeStruct((B, Lc, nkv), bf16),
                   jax.ShapeDtypeStruct((B, Lc, LANE), bf16)],
        compiler_params=_cp("parallel", "parallel"),
        name="mla_up",
    )(p, p, p, q_norm.reshape(1, -1), kv_norm.reshape(1, -1), wq, wkv, cos, sin)


def _s5_finish_kernel(y_ref, u_ref, d_ref, w_ref, b_ref, o_ref):
    y = y_ref[...] + d_ref[...] * u_ref[...]
    y = 0.5 * y * (1.0 + jnp.tanh(math.sqrt(2.0 / math.pi) * (y + 0.044715 * (y * y * y))))
    z = jnp.dot(y.astype(bf16), w_ref[...], preferred_element_type=f32) + b_ref[...]
    o_ref[...] = (y * _sigmoid(z)).astype(o_ref.dtype)


def _s5_finish(y, u, d_skip, glu_w, glu_b):
    B, Lc, N = u.shape
    tm = _tile(Lc, 1088)
    row = lambda b, i: (b, i, 0)
    const = lambda b, i: (0, 0)
    return pl.pallas_call(
        _s5_finish_kernel,
        grid=(B, Lc // tm),
        in_specs=[pl.BlockSpec((None, tm, N), row), pl.BlockSpec((None, tm, N), row),
                  pl.BlockSpec((1, N), const), pl.BlockSpec((N, N), const), pl.BlockSpec((1, N), const)],
        out_specs=pl.BlockSpec((None, tm, N), row),
        out_shape=jax.ShapeDtypeStruct((B, Lc, N), bf16),
        compiler_params=_cp("parallel", "parallel"),
        name="s5_finish",
    )(y, u, d_skip.reshape(1, N), glu_w.astype(bf16), glu_b.reshape(1, N))


def _gdn_finish_kernel(of_ref, ob_ref, z_ref, w_ref, o_ref):
    o = of_ref[...] + ob_ref[...]
    z = z_ref[...]
    w = w_ref[...]
    for h in range(GDN_HEADS):
        sl = slice(h * GDN_DV, (h + 1) * GDN_DV)
        zz = z[:, sl]
        o_ref[:, sl] = (_rms(o[:, sl], w) * (zz * _sigmoid(zz))).astype(o_ref.dtype)


def _gdn_finish(of, ob, p, norm_w):
    B, Lc, N = of.shape
    tm = _tile(Lc, 1088)
    row = lambda b, i: (b, i, 0)
    return pl.pallas_call(
        _gdn_finish_kernel,
        grid=(B, Lc // tm),
        in_specs=[pl.BlockSpec((None, tm, N), row), pl.BlockSpec((None, tm, N), row),
                  pl.BlockSpec((None, tm, N), lambda b, i: (b, i, O_Z // N)),
                  pl.BlockSpec((1, GDN_DV), lambda b, i: (0, 0))],
        out_specs=pl.BlockSpec((None, tm, N), row),
        out_shape=jax.ShapeDtypeStruct((B, Lc, N), bf16),
        compiler_params=_cp("parallel", "parallel"),
        name="gdn_finish",
    )(of, ob, p, norm_w.reshape(1, GDN_DV))


def _layer_norm(y, g, b, eps=1e-5):
    mu = jnp.mean(y, axis=1, keepdims=True)
    d = y - mu
    var = jnp.mean(d * d, axis=1, keepdims=True)
    return d * lax.rsqrt(var + eps) * g + b


def _merge_kernel(x_ref, a1_ref, a2_ref, w1_ref, w2_ref, ml_ref, mc_ref, g_ref, b_ref, rh_ref, rl_ref,
                  xo_ref, u_ref, lg_ref, *, C, tm, D):
    i = pl.program_id(1)
    half = tm // 2
    spans = [(0, half), (half, tm)]

    def project(r0, r1):
        return (jnp.dot(a1_ref[r0:r1, :], w1_ref[...], preferred_element_type=f32)
                + jnp.dot(a2_ref[r0:r1, :], w2_ref[...], preferred_element_type=f32))

    def finish(o, r0, r1):
        gate, shift, scale = _select_mod(ml_ref[...], mc_ref[...], i * tm + r0, r1 - r0, C, D, (2, 3, 4))
        xn = _layer_norm(DEEPNORM_ALPHA * x_ref[r0:r1, :] + gate * o, g_ref[...], b_ref[...])
        xo_ref[r0:r1, :] = xn
        u = xn * (1.0 + scale) + shift
        u_ref[r0:r1, :] = u
        uh = u.astype(bf16)
        ul = (u - uh.astype(f32)).astype(bf16)
        rh = rh_ref[...]
        lg_ref[r0:r1, :] = (jnp.dot(uh, rh, preferred_element_type=f32) + jnp.dot(ul, rh, preferred_element_type=f32)
                            + jnp.dot(uh, rl_ref[...], preferred_element_type=f32))

    o_next = project(*spans[0])
    for n, (r0, r1) in enumerate(spans):
        o = o_next
        if n + 1 < len(spans):
            o_next = project(*spans[n + 1])
        finish(o, r0, r1)


def _merge(x, a1, a2, w1, w2, mods, ln_g, ln_b, r_hi, r_lo, *, C):
    B, Lc, D = x.shape
    K1, K2 = a1.shape[2], a2.shape[2]
    tm = _tile(Lc, 544, 32)
    nb = mods.shape[0] - 1
    row = lambda b, i: (b, i, 0)
    const = lambda b, i: (0, 0)
    kern = functools.partial(_merge_kernel, C=C, tm=tm, D=D)
    return pl.pallas_call(
        kern,
        grid=(B, Lc // tm),
        in_specs=[pl.BlockSpec((None, tm, D), row),
                  pl.BlockSpec((None, tm, K1), row),
                  pl.BlockSpec((None, tm, K2), row),
                  pl.BlockSpec((K1, D), const, pipeline_mode=pl.Buffered(1)),
                  pl.BlockSpec((K2, D), const, pipeline_mode=pl.Buffered(1)),
                  pl.BlockSpec((None, 1, 6 * D), lambda b, i: (b, 0, 0)),
                  pl.BlockSpec((None, 1, 6 * D), lambda b, i: (nb, 0, 0)),
                  pl.BlockSpec((1, D), const),
                  pl.BlockSpec((1, D), const),
                  pl.BlockSpec((D, LANE), const),
                  pl.BlockSpec((D, LANE), const)],
        out_specs=[pl.BlockSpec((None, tm, D), row),
                   pl.BlockSpec((None, tm, D), row),
                   pl.BlockSpec((None, tm, LANE), row)],
        out_shape=[jax.ShapeDtypeStruct((B, Lc, D), f32),
                   jax.ShapeDtypeStruct((B, Lc, D), f32),
                   jax.ShapeDtypeStruct((B, Lc, LANE), f32)],
        compiler_params=_cp("parallel", "parallel"),
        name="merge",
    )(x, a1, a2, w1, w2, mods, mods, ln_g.reshape(1, D), ln_b.reshape(1, D), r_hi, r_lo)


def _router_kernel(lg_ref, bias_ref, o_ref):
    lg = lg_ref[...]
    shape = lg.shape
    lane = lax.broadcasted_iota(jnp.int32, shape, 1)
    valid = lane < N_EXPERTS
    neg = -jnp.inf
    scores = _sigmoid(lg)
    biased = jnp.where(valid, scores + bias_ref[...], neg)

    def first_argmax(v):
        m = jnp.max(v, axis=1, keepdims=True)
        idx = jnp.min(jnp.where(v == m, lane, LANE), axis=1, keepdims=True)
        return m, idx

    best_score = None
    best_group = None
    for g in range(N_GROUPS):
        in_g = jnp.logical_and(lane >= g * EXPERTS_PER_GROUP, lane < (g + 1) * EXPERTS_PER_GROUP)
        vals = jnp.where(in_g, biased, neg)
        m1, i1 = first_argmax(vals)
        m2, _ = first_argmax(jnp.where(lane == i1, neg, vals))
        gs = m1 + m2
        if g == 0:
            best_score, best_group = gs, jnp.zeros_like(i1)
        else:
            better = gs > best_score
            best_score = jnp.where(better, gs, best_score)
            best_group = jnp.where(better, g, best_group)
    in_best = jnp.logical_and(lane >= best_group * EXPERTS_PER_GROUP, lane < (best_group + 1) * EXPERTS_PER_GROUP)
    vals = jnp.where(in_best, biased, neg)
    _, e0 = first_argmax(vals)
    _, e1 = first_argmax(jnp.where(lane == e0, neg, vals))
    w0 = jnp.sum(jnp.where(lane == e0, scores, 0.0), axis=1, keepdims=True)
    w1 = jnp.sum(jnp.where(lane == e1, scores, 0.0), axis=1, keepdims=True)
    tot = w0 + w1
    out = jnp.where(lane == 0, e0.astype(f32), jnp.where(lane == 1, e1.astype(f32),
                    jnp.where(lane == 2, w0 / tot, jnp.where(lane == 3, w1 / tot, 0.0))))
    o_ref[...] = out


def _router(logits, bias_row):
    T = logits.shape[0]
    tm = _tile(T, 1088, 8)
    return pl.pallas_call(
        _router_kernel,
        grid=(T // tm,),
        in_specs=[pl.BlockSpec((tm, LANE), lambda i: (i, 0)), pl.BlockSpec((1, LANE), lambda i: (0, 0))],
        out_specs=pl.BlockSpec((tm, LANE), lambda i: (i, 0)),
        out_shape=jax.ShapeDtypeStruct((T, LANE), f32),
        compiler_params=_cp("parallel"),
        name="router",
    )(logits, bias_row)


MOE_CODE_BITS = 16


def _moe_kernel(te_ref, na_ref, code_ref, u_hbm, wg_ref, wu_ref, wd_ref, y_hbm, xbuf, ybuf, gsem, ssem, *, tm):
    i = pl.program_id(0)
    na = na_ref[0]
    slot = i % 2

    def gather_row(tile, sl, r):
        tok = code_ref[tile * tm + r] >> MOE_CODE_BITS
        pltpu.make_async_copy(u_hbm.at[pl.ds(tok, 1), :], xbuf.at[sl, pl.ds(r, 1), :], gsem.at[sl]).start()

    def wait_gather(sl):
        pltpu.make_async_copy(u_hbm.at[pl.ds(0, tm), :], xbuf.at[sl], gsem.at[sl]).wait()

    def wait_scatter(sl):
        pltpu.make_async_copy(ybuf.at[sl], y_hbm.at[pl.ds(0, tm), :], ssem.at[sl]).wait()

    @pl.when(i < na)
    def _():
        @pl.when(i == 0)
        def _():
            lax.fori_loop(0, tm, lambda r, c: (gather_row(0, 0, r), c)[1], 0)
            ybuf[1] = jnp.zeros(ybuf.shape[1:], f32)
            spare = pltpu.make_async_copy(ybuf.at[1], y_hbm.at[pl.ds(y_hbm.shape[0] - tm, tm), :], ssem.at[1])
            spare.start()
            spare.wait()

        wait_gather(slot)

        @pl.when(i >= 2)
        def _():
            wait_scatter(slot)

        nxt = jnp.minimum(i + 1, na - 1)
        for r in range(tm):
            gather_row(nxt, 1 - slot, r)
        x = xbuf[slot].astype(bf16)
        g = jnp.dot(x, wg_ref[...], preferred_element_type=f32)
        u = jnp.dot(x, wu_ref[...], preferred_element_type=f32)
        h = (g * _sigmoid(g) * u).astype(bf16)
        ybuf[slot] = jnp.dot(h, wd_ref[...], preferred_element_type=f32)
        for r in range(tm):
            dst = code_ref[i * tm + r] & ((1 << MOE_CODE_BITS) - 1)
            pltpu.make_async_copy(ybuf.at[slot, pl.ds(r, 1), :], y_hbm.at[pl.ds(dst, 1), :], ssem.at[slot]).start()

        @pl.when(i == na - 1)
        def _():
            wait_gather(1 - slot)
            wait_scatter(slot)

            @pl.when(na >= 2)
            def _():
                wait_scatter(1 - slot)


def _moe_experts(u, code, tile_expert, n_active, wg, wu, wd, layer, n_out_rows):
    T, D = u.shape
    F = wg.shape[3]
    tm = MOE_TM
    n_tiles = code.shape[0] // tm
    grid_spec = pltpu.PrefetchScalarGridSpec(
        num_scalar_prefetch=3,
        grid=(n_tiles,),
        in_specs=[pl.BlockSpec(memory_space=pl.ANY),
                  pl.BlockSpec((None, None, D, F), lambda i, te, na, cd: (layer, te[i], 0, 0)),
                  pl.BlockSpec((None, None, D, F), lambda i, te, na, cd: (layer, te[i], 0, 0)),
                  pl.BlockSpec((None, None, F, D), lambda i, te, na, cd: (layer, te[i], 0, 0))],
        out_specs=pl.BlockSpec(memory_space=pl.ANY),
        scratch_shapes=[pltpu.VMEM((2, tm, D), f32), pltpu.VMEM((2, tm, D), f32),
                        pltpu.SemaphoreType.DMA((2,)), pltpu.SemaphoreType.DMA((2,))],
    )
    return pl.pallas_call(
        functools.partial(_moe_kernel, tm=tm),
        grid_spec=grid_spec,
        out_shape=jax.ShapeDtypeStruct((n_out_rows, D), f32),
        compiler_params=_cp("arbitrary"),
        name="moe_experts",
    )(tile_expert, n_active, code, u, wg, wu, wd)


def _final_ln_kernel(x_ref, y0_ref, y1_ref, r_ref, ml_ref, mc_ref, g_ref, b_ref, o_ref, *, C, tm, D):
    i = pl.program_id(1)
    (gate,) = _select_mod(ml_ref[...], mc_ref[...], i * tm, tm, C, D, (5,))
    r = r_ref[...]
    f = r[:, 2:3] * y0_ref[...] + r[:, 3:4] * y1_ref[...]
    o_ref[...] = _layer_norm(DEEPNORM_ALPHA * x_ref[...] + gate * f, g_ref[...], b_ref[...])


def _final_ln(x, y2, r, mods, ln_g, ln_b, *, C):
    B, Lc, D = x.shape
    tm = _tile(Lc, 544)
    nI = Lc // tm
    nb = mods.shape[0] - 1
    row = lambda b, i: (b, i, 0)
    kern = functools.partial(_final_ln_kernel, C=C, tm=tm, D=D)
    return pl.pallas_call(
        kern,
        grid=(B, nI),
        in_specs=[pl.BlockSpec((None, tm, D), row),
                  pl.BlockSpec((tm, D), lambda b, i: (b * nI + i, 0)),
                  pl.BlockSpec((tm, D), lambda b, i: (B * nI + b * nI + i, 0)),
                  pl.BlockSpec((tm, LANE), lambda b, i: (b * nI + i, 0)),
                  pl.BlockSpec((None, 1, 6 * D), lambda b, i: (b, 0, 0)),
                  pl.BlockSpec((None, 1, 6 * D), lambda b, i: (nb, 0, 0)),
                  pl.BlockSpec((1, D), lambda b, i: (0, 0)), pl.BlockSpec((1, D), lambda b, i: (0, 0))],
        out_specs=pl.BlockSpec((None, tm, D), row),
        out_shape=jax.ShapeDtypeStruct((B, Lc, D), f32),
        compiler_params=_cp("parallel", "parallel"),
        name="final_ln",
    )(x, y2, y2, r, mods, mods, ln_g.reshape(1, D), ln_b.reshape(1, D))


S5_PAIRS = S5_GROUPS // 2
S5_TILE = 8


def _s5_pair_tables(lam_re, lam_im, b_re, b_im, c_re, c_im, log_dt):
    T1, G, P, J = S5_BLOCK, S5_GROUPS, S5_STATE, S5_GROUP
    n = T1 * J
    t_idx = jnp.arange(T1)
    lag = t_idx[None, :] - t_idx[:, None]
    toep = 0.0
    bc_secs, cc_secs, a_secs = [], [], []
    for d in range(2):
        tb = _s5_tables_dir(lam_re[d], lam_im[d], b_re[d], b_im[d], c_re[d], c_im[d], log_dt[d])
        kern, pw, b_bar, c, lam_dt = tb
        use = (lag >= 0) if d == 0 else (lag <= 0)
        kk = jnp.transpose(kern[jnp.clip(jnp.abs(lag), 0, T1 - 1)], (2, 0, 4, 1, 3))
        toep = toep + jnp.where(use[None, :, None, :, None], kk, 0.0).reshape(G, n, n)
        p_in = pw[T1 - 1 - t_idx] if d == 0 else pw[t_idx]
        bc = jnp.transpose(p_in[:, :, :, None] * b_bar[None], (1, 0, 3, 2)).reshape(G, n, P)
        p_out = pw[t_idx + 1] if d == 0 else pw[T1 - t_idx]
        cc = jnp.transpose(c[None] * p_out[:, :, None, :], (1, 3, 0, 2)).reshape(G, P, n)
        bc_secs += [bc.real, bc.imag]
        cc_secs += [cc.real, -cc.imag]
        for k in range(1, S5_TILE + 1):
            ak = jnp.exp(lam_dt * float(T1 * k))
            a_secs += [ak.real, ak.imag]
    place = lambda outer, inner: np.stack([
        np.kron(np.eye(outer), np.kron(np.eye(2)[:, g2:g2 + 1], np.eye(inner))) for g2 in range(2)]).astype(np.float32)
    tok, sec = place(T1, J), place(4, P)
    put = lambda rows, t, cols: jnp.einsum('xra,pxab,xcb->prc', rows, t.reshape(S5_PAIRS, 2, n, n), cols)
    tp = put(tok, toep, tok)
    bc = put(tok, jnp.stack(bc_secs, axis=2), sec)
    cc = put(sec, jnp.stack(cc_secs, axis=1), tok)
    apw = jnp.stack(a_secs, axis=0).reshape(2, S5_TILE, 2, S5_PAIRS, 2 * P)
    apw = jnp.transpose(apw, (3, 0, 1, 2, 4)).reshape(S5_PAIRS, 4 * S5_TILE, 2 * P)
    order = (jnp.arange(S5_TILE), jnp.arange(S5_TILE - 1, -1, -1))
    atile = jnp.stack([apw[:, d * 2 * S5_TILE + 2 * order[d] + ri] for d in range(2) for ri in range(2)], axis=1)
    return tp.astype(bf16), bc.astype(bf16), cc.astype(bf16), apw, atile.reshape(S5_PAIRS, 4 * S5_TILE, 2 * P)


def _s5_tables_dir(lam_re, lam_im, b_re, b_im, c_re, c_im, log_dt):
    T1 = S5_BLOCK
    lam = lax.complex(lam_re.astype(f32), lam_im.astype(f32))
    lam_dt = lam * jnp.exp(log_dt.astype(f32))[:, None]
    lam_bar = jnp.exp(lam_dt)
    b_bar = ((lam_bar - 1.0) / lam)[..., None] * lax.complex(b_re.astype(f32), b_im.astype(f32))
    c = lax.complex(c_re.astype(f32), c_im.astype(f32))
    pw = jnp.exp(lam_dt[None] * jnp.arange(T1 + 1, dtype=f32)[:, None, None])
    kern = jnp.einsum('gip,tgp,gpj->tgij', c, pw[:T1], b_bar, precision=lax.Precision.HIGHEST).real
    return kern, pw, b_bar, c, lam_dt


def _s5_kernel(s_ref, toep_ref, bc_ref, cc_ref, apw_ref, atile_ref, y_ref, *, nb, nb_ctx):
    T1, W, NT = S5_BLOCK, 2 * S5_GROUP, S5_TILE
    Q = LANE // W
    nt, nt_ctx = nb // NT, nb_ctx // NT
    slabs = [s_ref[pl.ds(t, nb, stride=T1), :] for t in range(T1)]
    grp = lax.broadcasted_iota(jnp.int32, (nb, LANE), 1) // W
    row = lax.broadcasted_iota(jnp.int32, (nb, LANE), 0)
    row8 = row % NT
    roll_rows = lambda x, k: pltpu.roll(x, k % nb, axis=0)
    out = [None] * T1
    for q in range(Q):
        cols = []
        for v in range(T1 // Q):
            acc = None
            for r in range(Q):
                shift = ((r - q) * W) % LANE
                piece = slabs[Q * v + r] if shift == 0 else pltpu.roll(slabs[Q * v + r], shift, axis=1)
                acc = piece if acc is None else jnp.where(grp == r, piece, acc)
            cols.append(acc)
        u = jnp.concatenate(cols, axis=1).astype(bf16)
        gin = jnp.dot(u, bc_ref[q], preferred_element_type=f32)
        states = []
        for d in range(2):
            apow = lambda k, ri: apw_ref[q, d * 2 * NT + 2 * (k - 1) + ri:d * 2 * NT + 2 * (k - 1) + ri + 1, :]
            hr, hi = gin[:, 2 * d * LANE:(2 * d + 1) * LANE], gin[:, (2 * d + 1) * LANE:(2 * d + 2) * LANE]
            for k in (1, 2, 4):
                sr, si = (roll_rows(hr, k), roll_rows(hi, k)) if d == 0 else (roll_rows(hr, -k), roll_rows(hi, -k))
                keep = (row8 >= k) if d == 0 else (row8 < NT - k)
                sr, si = jnp.where(keep, sr, 0.0), jnp.where(keep, si, 0.0)
                ar, ai = apow(k, 0), apow(k, 1)
                hr, hi = hr + ar * sr - ai * si, hi + ar * si + ai * sr
            a8r, a8i = apow(NT, 0), apow(NT, 1)
            order = range(nt) if d == 0 else list(range(nt_ctx - 1, -1, -1)) + list(range(nt - 1, nt_ctx - 1, -1))
            end = NT - 1 if d == 0 else 0
            cr = ci = jnp.zeros((1, LANE), f32)
            enter_r, enter_i = [None] * nt, [None] * nt
            for t in order:
                enter_r[t], enter_i[t] = cr, ci
                er, ei = hr[t * NT + end:t * NT + end + 1, :], hi[t * NT + end:t * NT + end + 1, :]
                cr, ci = a8r * cr - a8i * ci + er, a8r * ci + a8i * cr + ei
            cfr = jnp.concatenate([jnp.broadcast_to(c, (NT, LANE)) for c in enter_r], axis=0)
            cfi = jnp.concatenate([jnp.broadcast_to(c, (NT, LANE)) for c in enter_i], axis=0)
            tr = jnp.concatenate([atile_ref[q, 2 * d * NT:(2 * d + 1) * NT, :]] * nt, axis=0)
            ti = jnp.concatenate([atile_ref[q, (2 * d + 1) * NT:(2 * d + 2) * NT, :]] * nt, axis=0)
            hr, hi = hr + tr * cfr - ti * cfi, hi + tr * cfi + ti * cfr
            pr, pi = (roll_rows(hr, 1), roll_rows(hi, 1)) if d == 0 else (roll_rows(hr, -1), roll_rows(hi, -1))
            first = (row == 0) if d == 0 else (row == nb_ctx - 1)
            states += [jnp.where(first, 0.0, pr), jnp.where(first, 0.0, pi)]
        hp = jnp.concatenate(states, axis=1).astype(bf16)
        y = (jnp.dot(u, toep_ref[q], preferred_element_type=f32)
             + jnp.dot(hp, cc_ref[q], preferred_element_type=f32))
        for t in range(T1):
            v, r = divmod(t, Q)
            shift = ((q - r) * W) % LANE
            col = y[:, v * LANE:(v + 1) * LANE]
            piece = col if shift == 0 else pltpu.roll(col, shift, axis=1)
            out[t] = piece if out[t] is None else jnp.where(grp == q, piece, out[t])
    for t in range(T1):
        y_ref[pl.ds(t, nb, stride=T1), :] = out[t]


def _s5_mixer(s, C, lam_re, lam_im, b_re, b_im, c_re, c_im, log_dt):
    B, Lc, N = s.shape
    T1 = S5_BLOCK
    assert C % (T1 * S5_TILE) == 0 and Lc % (T1 * S5_TILE) == 0
    nb = Lc // T1
    n2 = 2 * T1 * S5_GROUP
    Q = LANE // (2 * S5_GROUP)
    toep, bc, cc, apw, atile = _s5_pair_tables(lam_re, lam_im, b_re, b_im, c_re, c_im, log_dt)
    kern = functools.partial(_s5_kernel, nb=nb, nb_ctx=C // T1)
    tab = lambda rows, cols: pl.BlockSpec((Q, rows, cols), lambda cb, b: (cb, 0, 0))
    return pl.pallas_call(
        kern,
        grid=(N // LANE, B),
        in_specs=[pl.BlockSpec((None, Lc, LANE), lambda cb, b: (b, 0, cb)),
                  tab(n2, n2), tab(n2, 4 * LANE), tab(4 * LANE, n2), tab(4 * S5_TILE, LANE), tab(4 * S5_TILE, LANE)],
        out_specs=pl.BlockSpec((None, Lc, LANE), lambda cb, b: (b, 0, cb)),
        out_shape=jax.ShapeDtypeStruct((B, Lc, N), f32),
        compiler_params=_cp("parallel", "parallel"),
        name="s5_scan",
    )(s, toep, bc, cc, apw, atile)


def _dotT(a, b):
    return lax.dot_general(a.astype(bf16), b.astype(bf16), (((1,), (1,)), ((), ())), preferred_element_type=f32)


def _dot(a, b):
    return jnp.dot(a.astype(bf16), b.astype(bf16), preferred_element_type=f32)


def _split(a):
    hi = a.astype(bf16)
    return hi, (a - hi.astype(f32)).astype(bf16)


def _dot_split(a, b):
    n = a.shape[1]
    ah, al = _split(a)
    bh, bl = _split(b)
    lhs = jnp.concatenate([ah, al], axis=1)
    rhs = jnp.concatenate([jnp.concatenate([bh, bl], axis=1),
                           jnp.concatenate([bh, jnp.zeros_like(bl)], axis=1)], axis=0)
    out = jnp.dot(lhs, rhs, preferred_element_type=f32)
    return out[:, :n] + out[:, n:]


def _gdn_chunk_index(n, n_ctx, n_all, reverse):
    if not reverse:
        return n
    return jnp.where(n < n_ctx, n_ctx - 1 - n, n_all - 1 - (n - n_ctx))


def _gdn_kernel(x_ref, xp_ref, xn_ref, gt_ref, cw_ref, alog_ref, dtb_ref, o_ref, s_scr, *, C, Lc, Cn, reverse):
    n = pl.program_id(1)
    n_ctx, n_all = C // Cn, Lc // Cn
    ci = _gdn_chunk_index(n, n_ctx, n_all, reverse)
    H, dk, dv = GDN_HEADS, GDN_DK, GDN_DV
    d = 1 if reverse else 0

    @pl.when(n == 0)
    def _():
        s_scr[...] = jnp.zeros_like(s_scr)

    x = x_ref[...]
    loc = lax.broadcasted_iota(jnp.int32, (Cn, 1), 0)
    row = ci * Cn + loc
    prev = jnp.where(loc == 0, xp_ref[7:8, :], pltpu.roll(x, 1, axis=0))
    prev = jnp.where(jnp.logical_or(row == 0, row == C), 0.0, prev)
    nxt = jnp.where(loc == Cn - 1, xn_ref[0:1, :], pltpu.roll(x, Cn - 1, axis=0))
    nxt = jnp.where(jnp.logical_or(row == C - 1, row == Lc - 1), 0.0, nxt)
    y = prev * cw_ref[0:1, :] + x * cw_ref[1:2, :] + nxt * cw_ref[2:3, :]
    y = y * _sigmoid(y)

    graw = gt_ref[...]
    z = graw + dtb_ref[...]
    g_all = -jnp.exp(alog_ref[...]) * (jnp.maximum(z, 0.0) + jnp.log(1.0 + jnp.exp(-jnp.abs(z))))
    beta_all = _sigmoid(graw)

    ii = lax.broadcasted_iota(jnp.int32, (Cn, Cn), 0)
    jj = lax.broadcasted_iota(jnp.int32, (Cn, Cn), 1)
    incl = (ii <= jj) if reverse else (ii >= jj)
    strict = (ii < jj) if reverse else (ii > jj)
    eye = (ii == jj).astype(f32)
    tri = incl.astype(bf16)
    g_hi = g_all.astype(bf16)
    g_lo = (g_all - g_hi.astype(f32)).astype(bf16)
    gc_all = jnp.dot(tri, g_hi, preferred_element_type=f32) + jnp.dot(tri, g_lo, preferred_element_type=f32)
    gc_all_t = gc_all.T
    last = 0 if reverse else Cn - 1

    heads = range(H)
    gc = [gc_all[:, 8 * d + h:8 * d + h + 1] for h in heads]
    gl = [gc_all[last:last + 1, 8 * d + h:8 * d + h + 1] for h in heads]
    beta = [beta_all[:, 16 + 8 * d + h:17 + 8 * d + h] for h in heads]
    decay = [jnp.where(incl, jnp.exp(jnp.where(incl, gc[h] - gc_all_t[8 * d + h:8 * d + h + 1, :], 0.0)), 0.0)
             for h in heads]
    l2 = lambda t: t * lax.rsqrt(jnp.sum(t * t, axis=1, keepdims=True) + 1e-6)
    q = [l2(y[:, h * dk:(h + 1) * dk]) * (dk ** -0.5) for h in heads]
    k = [l2(y[:, H * dk + h * dk:H * dk + (h + 1) * dk]) for h in heads]
    v = [y[:, 2 * H * dk + h * dv:2 * H * dk + (h + 1) * dv] for h in heads]
    kb = [k[h] * beta[h] for h in heads]
    kq = [_dotT(jnp.concatenate([kb[h], q[h]], axis=0), k[h]) for h in heads]
    m = [jnp.where(strict, kq[h][:Cn] * decay[h], 0.0) for h in heads]
    a = [jnp.where(incl, kq[h][Cn:] * decay[h], 0.0) for h in heads]
    t_inv = [eye - m[h] for h in heads]
    pw = m
    for _ in range(int(math.log2(Cn)) - 1):
        pw = [_dot_split(pw[h], pw[h]) for h in heads]
        t_inv = [t_inv[h] + _dot_split(t_inv[h], pw[h]) for h in heads]
    e_gc = [jnp.exp(gc[h]) for h in heads]
    uw = [_dot(t_inv[h], jnp.concatenate([v[h] * beta[h], kb[h] * e_gc[h]], axis=1)) for h in heads]
    s_old = [s_scr[h] for h in heads]
    r = [_dot(jnp.concatenate([uw[h][:, dv:], q[h] * e_gc[h]], axis=0), s_old[h]) for h in heads]
    v_new = [uw[h][:, :dv] - r[h][:Cn] for h in heads]
    for h in heads:
        o_ref[:, h * dv:(h + 1) * dv] = r[h][Cn:] + _dot(a[h], v_new[h])
    for h in heads:
        kd = k[h] * jnp.exp(gl[h] - gc[h])
        s_scr[h] = s_old[h] * jnp.exp(gl[h]) + lax.dot_general(
            kd.astype(bf16), v_new[h].astype(bf16), (((0,), (0,)), ((), ())), preferred_element_type=f32)


def _gdn_direction(p, conv_w, alog_row, dtb_row, *, C, reverse):
    B, Lc, _ = p.shape
    Cn = LANE
    n_ctx, n_all = C // Cn, Lc // Cn
    nq = GDN_QKV
    ci = lambda n: _gdn_chunk_index(n, n_ctx, n_all, reverse)
    sub = Cn // 8
    kern = functools.partial(_gdn_kernel, C=C, Lc=Lc, Cn=Cn, reverse=reverse)
    return pl.pallas_call(
        kern,
        grid=(B, n_all),
        in_specs=[pl.BlockSpec((None, Cn, nq), lambda b, n: (b, ci(n), 0)),
                  pl.BlockSpec((None, 8, nq), lambda b, n: (b, jnp.maximum(ci(n) * sub - 1, 0), 0)),
                  pl.BlockSpec((None, 8, nq), lambda b, n: (b, jnp.minimum((ci(n) + 1) * sub, Lc // 8 - 1), 0)),
                  pl.BlockSpec((None, Cn, LANE), lambda b, n: (b, ci(n), O_GATE // LANE)),
                  pl.BlockSpec((3, nq), lambda b, n: (0, 0)),
                  pl.BlockSpec((1, LANE), lambda b, n: (0, 0)),
                  pl.BlockSpec((1, LANE), lambda b, n: (0, 0))],
        out_specs=pl.BlockSpec((None, Cn, GDN_HEADS * GDN_DV), lambda b, n: (b, ci(n), 0)),
        out_shape=jax.ShapeDtypeStruct((B, Lc, GDN_HEADS * GDN_DV), f32),
        scratch_shapes=[pltpu.VMEM((GDN_HEADS, GDN_DK, GDN_DV), f32)],
        compiler_params=_cp("parallel", "arbitrary"),
        name="gdn_rev" if reverse else "gdn_fwd",
    )(p, p, p, p, conv_w, alog_row, dtb_row)


def _gdn_pallas(p, C, conv_w, a_log, dt_bias):
    outs = []
    for d in range(2):
        pad = lambda t: jnp.zeros((1, LANE), f32).at[0, 8 * d:8 * d + GDN_HEADS].set(t[d].astype(f32))
        outs.append(_gdn_direction(p, conv_w, pad(a_log), pad(dt_bias), C=C, reverse=(d == 1)))
    return outs


def _dispatch_plan(e_idx, n_tiles):
    T = e_idx.shape[0]
    tm = MOE_TM
    assert 2 * T + tm <= (1 << MOE_CODE_BITS) and T < (1 << (31 - MOE_CODE_BITS))
    flat = e_idx.reshape(-1)
    onehot = (flat[:, None] == jnp.arange(N_EXPERTS)[None, :]).astype(jnp.int32)
    rank = jnp.sum((jnp.cumsum(onehot, axis=0) - onehot) * onehot, axis=1)
    counts = jnp.sum(onehot, axis=0)
    padded = ((counts + tm - 1) // tm) * tm
    ends = jnp.cumsum(padded)
    offs = ends - padded
    pos = offs[flat] + rank
    slot_id = jnp.arange(2 * T, dtype=jnp.int32)
    tok, k = slot_id // 2, slot_id % 2
    spare = 2 * T + jnp.arange(n_tiles * tm, dtype=jnp.int32) % tm
    code = spare.at[pos].set((tok << MOE_CODE_BITS) | (k * T + tok))
    tile_start = jnp.arange(n_tiles, dtype=jnp.int32) * tm
    tile_expert = jnp.minimum(jnp.sum((tile_start[:, None] >= ends[None, :]).astype(jnp.int32), axis=1), N_EXPERTS - 1)
    n_active = (ends[-1] // tm).astype(jnp.int32).reshape(1)
    last_e = tile_expert[jnp.maximum(n_active[0] - 1, 0)]
    tile_expert = jnp.where(tile_start < ends[-1], tile_expert, last_e).astype(jnp.int32)
    return code, tile_expert, n_active


def _moe(u, logits, bias_row, wg, wu, wd, layer):
    B, Lc, D = u.shape
    T = B * Lc
    r = _router(logits.reshape(T, LANE), bias_row)
    n_tiles = (2 * T + MOE_TM - 1) // MOE_TM + N_EXPERTS
    code, tile_expert, n_active = _dispatch_plan(r[:, 0:2].astype(jnp.int32), n_tiles)
    return _moe_experts(u.reshape(T, D), code, tile_expert, n_active, wg, wu, wd, layer, 2 * T + MOE_TM), r


def _rope_tables(L, C):
    rows = L // GRID_W
    t_row = jnp.repeat(jnp.arange(rows), GRID_W).astype(f32)
    t_col = jnp.tile(jnp.arange(GRID_W), rows).astype(f32)
    n_freq = ROPE_DIM // 4
    inv_freq = ROPE_BASE ** (-jnp.arange(n_freq, dtype=f32) / n_freq)
    ang = jnp.concatenate([t_row[:, None] * inv_freq, t_col[:, None] * inv_freq], axis=-1)
    cos = jnp.repeat(jnp.cos(ang), 2, axis=1)
    sin = jnp.repeat(jnp.sin(ang), 2, axis=1) * jnp.tile(jnp.array([-1.0, 1.0], f32), ROPE_DIM // 2)
    one = jnp.ones((L, ROPE_DIM), f32)
    zero = jnp.zeros((L, ROPE_DIM), f32)
    ctx = lambda t, fill: jnp.concatenate([jnp.full((C, LANE), fill, f32), t], axis=0)
    return (ctx(jnp.concatenate([cos, cos], 1), 1.0), ctx(jnp.concatenate([sin, sin], 1), 0.0),
            ctx(jnp.concatenate([cos, one], 1), 1.0), ctx(jnp.concatenate([sin, zero], 1), 0.0))


def _prep_even_w_in(w):
    D = w.shape[0]
    nqk = 2 * DA_HEADS * DA_HEAD_DIM
    heads_first = lambda t: t.reshape(D, 2, DA_HEADS, DA_HEAD_DIM).transpose(0, 2, 1, 3).reshape(D, nqk)
    q = heads_first(w[:, :nqk]) * (DA_HEAD_DIM ** -0.5 * LOG2E)
    k = heads_first(w[:, nqk:2 * nqk])
    return jnp.concatenate([q, k, w[:, 2 * nqk:]], axis=1).astype(bf16)


def _prep_odd_w_in(w):
    D = w.shape[0]
    z = lambda n: jnp.zeros((D, n), w.dtype)
    c0 = MLA_Q_RANK
    c1 = c0 + MLA_KV_RANK
    c2 = c1 + MLA_ROPE
    c3 = c2 + GDN_QKV
    c4 = c3 + GDN_HEADS * GDN_DV
    parts = [w[:, c2:c3], w[:, c3:c4], w[:, :c0], w[:, c0:c1], w[:, c1:c2], z(LANE - MLA_ROPE),
             w[:, c4:], z(LANE - 4 * GDN_HEADS)]
    out = jnp.concatenate(parts, axis=1)
    assert out.shape[1] == O_IN
    return out.astype(bf16)


def _prep_mla_w(w_uq, w_ukv):
    H = MLA_HEADS
    rq = w_uq.shape[0]
    wq = w_uq.reshape(rq, H, MLA_NOPE + MLA_ROPE)
    nope = wq[:, :, :MLA_NOPE].reshape(rq, H * MLA_NOPE)
    rope = jnp.concatenate([wq[:, :, MLA_NOPE:], jnp.zeros((rq, H, LANE - MLA_ROPE), w_uq.dtype)], axis=2)
    wq_p = jnp.concatenate([nope, rope.reshape(rq, H * LANE)], axis=1).astype(bf16)
    rkv = w_ukv.shape[0]
    wkv = w_ukv.reshape(rkv, H, MLA_NOPE + MLA_V)
    wkv_p = jnp.concatenate([wkv[:, :, :MLA_NOPE].reshape(rkv, H * MLA_NOPE),
                             wkv[:, :, MLA_NOPE:].reshape(rkv, H * MLA_V)], axis=1).astype(bf16)
    return wq_p, wkv_p


def kernel(x, c, ctx, c_ctx, mod_w, mod_b, ln_g, ln_b, e_w_in, e_w_out, da_lam, da_subln, s5_lam_re, s5_lam_im, s5_b_re, s5_b_im, s5_c_re, s5_c_im, s5_log_dt, s5_d, s5_glu_w, s5_glu_b, o_w_in, o_w_out, mla_q_norm, mla_kv_norm, mla_w_uq, mla_w_ukv, gdn_conv, gdn_a_log, gdn_dt_bias, gdn_norm, router_w, router_bias, moe_w_gate, moe_w_up, moe_w_down):
    B, L, D = x.shape
    C = ctx.shape[1]
    depth = mod_w.shape[0]
    assert depth == DEPTH

    R = -(-(B + 1) // 8) * 8
    cc = jnp.concatenate([c, c_ctx[None, :], jnp.zeros((R - B - 1, D), f32)], axis=0)
    mods_all = _modulation(cc, mod_w, mod_b)[:, :B + 1].reshape(depth, B + 1, 1, 6 * D)

    cos_e, sin_e, cos_o, sin_o = _rope_tables(L, C)
    wg_all, wu_all, wd_all = moe_w_gate.astype(bf16), moe_w_up.astype(bf16), moe_w_down.astype(bf16)
    r_pad = jnp.concatenate([router_w.astype(f32), jnp.zeros((D, LANE - N_EXPERTS), f32)], axis=1)
    r_hi = r_pad.astype(bf16)
    r_lo = (r_pad - r_hi.astype(f32)).astype(bf16)
    bias_row = jnp.concatenate([router_bias.astype(f32), jnp.zeros((LANE - N_EXPERTS,), f32)]).reshape(1, LANE)

    xs = jnp.concatenate([ctx, x], axis=1)
    n_att = DA_HEADS * DA_V_DIM
    for l in range(depth):
        mods = mods_all[l]
        i = l // 2
        if l % 2 == 0:
            qkv, s = _inproj(xs, mods, _prep_even_w_in(e_w_in[i]), cos_e, sin_e, C=C,
                             n_rope_cols=4 * DA_HEADS * DA_HEAD_DIM, n_a_cols=3072, a_dtype=bf16, b_dtype=f32)
            a1 = _diff_attention(qkv, da_lam[i].astype(f32), da_subln[i].astype(f32), C=C,
                                 lam_init=0.8 - 0.6 * math.exp(-0.3 * l))
            y = _s5_mixer(s, C, s5_lam_re[i], s5_lam_im[i], s5_b_re[i], s5_b_im[i], s5_c_re[i], s5_c_im[i],
                          s5_log_dt[i])
            a2 = _s5_finish(y, s, s5_d[i], s5_glu_w[i], s5_glu_b[i])
            w_out = e_w_out[i].astype(bf16)
        else:
            (p,) = _inproj(xs, mods, _prep_odd_w_in(o_w_in[i]), cos_e, sin_e, C=C,
                           n_rope_cols=0, n_a_cols=O_IN, a_dtype=f32, b_dtype=f32)
            wq_p, wkv_p = _prep_mla_w(mla_w_uq[i], mla_w_ukv[i])
            q, kv, kr = _mla_up(p, mla_q_norm[i], mla_kv_norm[i], wq_p, wkv_p, cos_o, sin_o)
            a1 = _mla_attention(q, kv, kr, C=C)
            of, ob = _gdn_pallas(p, C, gdn_conv[i].astype(f32), gdn_a_log[i], gdn_dt_bias[i])
            a2 = _gdn_finish(of, ob, p, gdn_norm[i])
            w_out = o_w_out[i].astype(bf16)
        xs, u, logits = _merge(xs, a1, a2, w_out[:n_att], w_out[n_att:], mods, ln_g[l, 0], ln_b[l, 0],
                               r_hi, r_lo, C=C)
        y2, r = _moe(u, logits, bias_row, wg_all, wu_all, wd_all, l)
        xs = _final_ln(xs, y2, r, mods, ln_g[l, 1], ln_b[l, 1], C=C)
    return xs[:, C:, :]
```

```python
import functools
import math

import jax
import jax.numpy as jnp
import numpy as np
from jax import lax
from jax.experimental import pallas as pl
from jax.experimental.pallas import tpu as pltpu

f32 = jnp.float32
bf16 = jnp.bfloat16

GRID_W = 64
ROPE_DIM = 64
ROPE_BASE = 10000.0
DA_HEADS = 8
DA_HEAD_DIM = 64
DA_V_DIM = 128
S5_CHANNELS = 512
S5_GROUP = 16
S5_GROUPS = 32
S5_STATE = 64
S5_BLOCK = 16
MLA_HEADS = 8
MLA_Q_RANK = 512
MLA_KV_RANK = 256
MLA_NOPE = 128
MLA_ROPE = 64
MLA_V = 128
GDN_HEADS = 8
GDN_DK = 128
GDN_DV = 128
GDN_QKV = 3072
GDN_CHUNK = 64
N_EXPERTS = 16
N_GROUPS = 4
EXPERTS_PER_GROUP = 4
DEPTH = 4
DEEPNORM_ALPHA = (2 * DEPTH) ** 0.25
LANE = 128
MOE_TM = 256
VMEM_LIMIT = 56 * 1024 * 1024

O_QKV, O_Z, O_CQ, O_CKV, O_KR, O_GATE, O_IN = 0, 3072, 4096, 4608, 4864, 4992, 5120


def _cp(*sem):
    return pltpu.CompilerParams(dimension_semantics=sem, vmem_limit_bytes=VMEM_LIMIT)


def _tile(n, cap, mult=16):
    best = None
    for t in range(mult, min(n, cap) + 1, mult):
        if n % t == 0:
            best = t
    assert best is not None, (n, cap, mult)
    return best


def _sigmoid(x):
    return 1.0 / (1.0 + jnp.exp(-x))


def _mod_kernel(a_ref, w_ref, b_ref, o_ref):
    a = a_ref[...]
    act = (a * _sigmoid(a)).astype(bf16)
    o_ref[...] = jnp.dot(act, w_ref[...].astype(bf16), preferred_element_type=f32) + b_ref[...]


def _modulation(cc, mod_w, mod_b):
    depth, D, N = mod_w.shape
    R = cc.shape[0]
    tn = _tile(N, 1024, LANE)
    return pl.pallas_call(
        _mod_kernel,
        grid=(depth, N // tn),
        in_specs=[pl.BlockSpec((R, D), lambda l, j: (0, 0)),
                  pl.BlockSpec((None, D, tn), lambda l, j: (l, 0, j)),
                  pl.BlockSpec((None, 1, tn), lambda l, j: (l, 0, j))],
        out_specs=pl.BlockSpec((None, R, tn), lambda l, j: (l, 0, j)),
        out_shape=jax.ShapeDtypeStruct((depth, R, N), f32),
        compiler_params=_cp("parallel", "parallel"),
        name="modulation",
    )(cc, mod_w, mod_b.reshape(depth, 1, N))


def _select_mod(ml, mc, row0, tm, C, D, chunks):
    is_ctx = (row0 + lax.broadcasted_iota(jnp.int32, (tm, 1), 0)) < C
    return [jnp.where(is_ctx, mc[:, k * D:(k + 1) * D], ml[:, k * D:(k + 1) * D]) for k in chunks]


def _rope(seg, cos, sin):
    nxt = pltpu.roll(seg, LANE - 1, axis=1)
    prv = pltpu.roll(seg, 1, axis=1)
    even = (lax.broadcasted_iota(jnp.int32, seg.shape, 1) % 2) == 0
    return seg * cos + jnp.where(even, nxt, prv) * sin


def _inproj_kernel(x_ref, ml_ref, mc_ref, w_ref, cos_ref, sin_ref, *rest, C, tm, tn, D, n_rope, n_a, has_b):
    if has_b:
        oa_ref, ob_ref, u_scr = rest
    else:
        oa_ref, u_scr = rest
        ob_ref = None
    i = pl.program_id(1)
    j = pl.program_id(2)

    @pl.when(j == 0)
    def _():
        shift, scale = _select_mod(ml_ref[...], mc_ref[...], i * tm, tm, C, D, (0, 1))
        u_scr[...] = (x_ref[...] * (1.0 + scale) + shift).astype(bf16)

    half = tm // 2
    spans = [(0, half), (half, tm)]

    def emit(write):
        acc = lambda r0, r1: jnp.dot(u_scr[r0:r1, :], w_ref[...], preferred_element_type=f32)
        a_next = acc(*spans[0])
        for n, (r0, r1) in enumerate(spans):
            a = a_next
            if n + 1 < len(spans):
                a_next = acc(*spans[n + 1])
            write(a, r0, r1)

    def write_rope(a, r0, r1):
        cos = cos_ref[r0:r1, :]
        sin = sin_ref[r0:r1, :]
        for c in range(tn // LANE):
            oa_ref[r0:r1, c * LANE:(c + 1) * LANE] = _rope(a[:, c * LANE:(c + 1) * LANE], cos, sin).astype(oa_ref.dtype)

    def write_a(a, r0, r1):
        oa_ref[r0:r1, :] = a.astype(oa_ref.dtype)

    def write_b(a, r0, r1):
        ob_ref[r0:r1, :] = a.astype(ob_ref.dtype)

    if n_rope > 0:
        @pl.when(j < n_rope)
        def _():
            emit(write_rope)

    @pl.when(jnp.logical_and(j >= n_rope, j < n_a))
    def _():
        emit(write_a)

    if has_b:
        @pl.when(j >= n_a)
        def _():
            emit(write_b)


def _inproj(x, mods, w, cos, sin, *, C, n_rope_cols, n_a_cols, a_dtype, b_dtype):
    B, Lc, D = x.shape
    N = w.shape[1]
    tm = _tile(Lc, 1088)
    tn = 512
    assert N % tn == 0 and n_rope_cols % tn == 0 and n_a_cols % tn == 0
    n_a = n_a_cols // tn
    has_b = n_a_cols < N
    kern = functools.partial(_inproj_kernel, C=C, tm=tm, tn=tn, D=D, n_rope=n_rope_cols // tn, n_a=n_a, has_b=has_b)
    out_shape = [jax.ShapeDtypeStruct((B, Lc, n_a_cols), a_dtype)]
    out_specs = [pl.BlockSpec((None, tm, tn), lambda b, i, j: (b, i, jnp.minimum(j, n_a - 1)))]
    if has_b:
        out_shape.append(jax.ShapeDtypeStruct((B, Lc, N - n_a_cols), b_dtype))
        out_specs.append(pl.BlockSpec((None, tm, tn), lambda b, i, j: (b, i, jnp.maximum(j - n_a, 0))))
    nb = mods.shape[0] - 1
    return pl.pallas_call(
        kern,
        grid=(B, Lc // tm, N // tn),
        in_specs=[pl.BlockSpec((None, tm, D), lambda b, i, j: (b, i, 0)),
                  pl.BlockSpec((None, 1, 6 * D), lambda b, i, j: (b, 0, 0)),
                  pl.BlockSpec((None, 1, 6 * D), lambda b, i, j: (nb, 0, 0)),
                  pl.BlockSpec((D, tn), lambda b, i, j: (0, j)),
                  pl.BlockSpec((tm, LANE), lambda b, i, j: (i, 0)),
                  pl.BlockSpec((tm, LANE), lambda b, i, j: (i, 0))],
        out_specs=out_specs,
        out_shape=out_shape,
        scratch_shapes=[pltpu.VMEM((tm, D), bf16)],
        compiler_params=_cp("parallel", "parallel", "arbitrary"),
        name="inproj",
    )(x, mods, mods, w, cos, sin)


def _transpose_bf16(x):
    return x.astype(f32).T.astype(bf16)


LOG2E = math.log2(math.e)
ATTN_KEY_CHUNK = 1024


def _attend_t(ks, q_ts, vt_ref, nkeys, scale=None):
    n = len(ks)
    chunks = [(c0, min(c0 + ATTN_KEY_CHUNK, nkeys)) for c0 in range(0, nkeys, ATTN_KEY_CHUNK)]

    def scores(c):
        out = [jnp.dot(ks[i][c[0]:c[1]], q_ts[i], preferred_element_type=f32) for i in range(n)]
        return out if scale is None else [t * (scale * LOG2E) for t in out]

    m, l, acc = [None] * n, [None] * n, [None] * n
    s_next = scores(chunks[0])
    for j, (c0, c1) in enumerate(chunks):
        s_cur = s_next
        if j + 1 < len(chunks):
            s_next = scores(chunks[j + 1])
        v_c = vt_ref[:, c0:c1]
        for i in range(n):
            mc = jnp.max(s_cur[i], axis=0, keepdims=True)
            m_new = mc if m[i] is None else jnp.maximum(m[i], mc)
            p = jnp.exp2(s_cur[i] - m_new)
            pv = jnp.dot(v_c, p.astype(bf16), preferred_element_type=f32)
            if m[i] is None:
                l[i], acc[i] = jnp.sum(p, axis=0, keepdims=True), pv
            else:
                alpha = jnp.exp2(m[i] - m_new)
                l[i] = alpha * l[i] + jnp.sum(p, axis=0, keepdims=True)
                acc[i] = alpha * acc[i] + pv
            m[i] = m_new
    return [acc[i] / l[i] for i in range(n)]


def _diff_attn_kernel(q_ref, k_ref, v_ref, lam_ref, sub_ref, o_ref, vt_scr, *, C, tq, lam_init):
    qi = pl.program_id(2)

    @pl.when(qi == 0)
    def _():
        vt_scr[...] = _transpose_bf16(v_ref[...])

    lv = lam_ref[...]
    lam = (jnp.exp(jnp.sum(lv[0:1] * lv[1:2], axis=1, keepdims=True))
           - jnp.exp(jnp.sum(lv[2:3] * lv[3:4], axis=1, keepdims=True)) + lam_init)

    def run(nkeys):
        q_t = _transpose_bf16(q_ref[...])
        k = k_ref[0:nkeys, :]
        hd = DA_HEAD_DIM
        o1, o2 = _attend_t([k[:, 0:hd], k[:, hd:2 * hd]], [q_t[0:hd, :], q_t[hd:2 * hd, :]], vt_scr, nkeys)
        o = (o1 - lam * o2).T
        o = o * lax.rsqrt(jnp.mean(o * o, axis=1, keepdims=True) + 1e-6) * sub_ref[...]
        o_ref[...] = (o * (1.0 - lam_init)).astype(o_ref.dtype)

    @pl.when(qi * tq < C)
    def _():
        run(C)

    @pl.when(qi * tq >= C)
    def _():
        run(k_ref.shape[0])


def _diff_attention(qkv, da_lam, da_subln, *, C, lam_init):
    B, Lc, _ = qkv.shape
    H = DA_HEADS
    tq = _tile(C, 256)
    kern = functools.partial(_diff_attn_kernel, C=C, tq=tq, lam_init=lam_init)
    return pl.pallas_call(
        kern,
        grid=(B, H, Lc // tq),
        in_specs=[pl.BlockSpec((None, tq, LANE), lambda b, h, i: (b, i, h)),
                  pl.BlockSpec((None, Lc, LANE), lambda b, h, i: (b, 0, H + h)),
                  pl.BlockSpec((None, Lc, LANE), lambda b, h, i: (b, 0, 2 * H + h)),
                  pl.BlockSpec((4, DA_HEAD_DIM), lambda b, h, i: (0, 0)),
                  pl.BlockSpec((1, DA_V_DIM), lambda b, h, i: (0, 0))],
        out_specs=pl.BlockSpec((None, tq, LANE), lambda b, h, i: (b, i, h)),
        out_shape=jax.ShapeDtypeStruct((B, Lc, H * DA_V_DIM), bf16),
        scratch_shapes=[pltpu.VMEM((DA_V_DIM, Lc), bf16)],
        compiler_params=_cp("parallel", "parallel", "arbitrary"),
        name="diff_attention",
    )(qkv, qkv, qkv, da_lam, da_subln.reshape(1, DA_V_DIM))


def _mla_attn_kernel(qn_ref, qr_ref, kn_ref, kr_ref, v_ref, o_ref, vt_scr, *, C, tq, scale):
    qi = pl.program_id(2)

    @pl.when(qi == 0)
    def _():
        vt_scr[...] = _transpose_bf16(v_ref[...])

    def run(nkeys):
        q_t = jnp.concatenate([_transpose_bf16(qn_ref[...]), _transpose_bf16(qr_ref[...])], axis=0)
        k = jnp.concatenate([kn_ref[0:nkeys, :], kr_ref[0:nkeys, :]], axis=1)
        o_ref[...] = _attend_t([k], [q_t], vt_scr, nkeys, scale)[0].T.astype(o_ref.dtype)

    @pl.when(qi * tq < C)
    def _():
        run(C)

    @pl.when(qi * tq >= C)
    def _():
        run(kn_ref.shape[0])


def _mla_attention(q, kv, kr, *, C):
    B, Lc, _ = q.shape
    H = MLA_HEADS
    tq = _tile(C, 256)
    kern = functools.partial(_mla_attn_kernel, C=C, tq=tq, scale=(MLA_NOPE + MLA_ROPE) ** -0.5)
    return pl.pallas_call(
        kern,
        grid=(B, H, Lc // tq),
        in_specs=[pl.BlockSpec((None, tq, LANE), lambda b, h, i: (b, i, h)),
                  pl.BlockSpec((None, tq, LANE), lambda b, h, i: (b, i, H + h)),
                  pl.BlockSpec((None, Lc, LANE), lambda b, h, i: (b, 0, h)),
                  pl.BlockSpec((None, Lc, LANE), lambda b, h, i: (b, 0, 0)),
                  pl.BlockSpec((None, Lc, LANE), lambda b, h, i: (b, 0, H + h))],
        out_specs=pl.BlockSpec((None, tq, LANE), lambda b, h, i: (b, i, h)),
        out_shape=jax.ShapeDtypeStruct((B, Lc, H * MLA_V), bf16),
        scratch_shapes=[pltpu.VMEM((MLA_V, Lc), bf16)],
        compiler_params=_cp("parallel", "parallel", "arbitrary"),
        name="mla_attention",
    )(q, q, kv, kr, kv)


def _rms(x, w, eps=1e-6):
    return x * lax.rsqrt(jnp.mean(x * x, axis=1, keepdims=True) + eps) * w


def _mla_up_kernel(cq_ref, ckv_ref, kr_ref, qn_ref, kvn_ref, wq_ref, wkv_ref, cos_ref, sin_ref,
                   q_ref, kv_ref, kro_ref):
    nq = MLA_HEADS * MLA_NOPE
    cos = cos_ref[...]
    sin = sin_ref[...]
    q = jnp.dot(_rms(cq_ref[...], qn_ref[...]).astype(bf16), wq_ref[...], preferred_element_type=f32)
    q_ref[:, 0:nq] = q[:, 0:nq].astype(q_ref.dtype)
    for h in range(MLA_HEADS):
        c0 = nq + h * LANE
        q_ref[:, c0:c0 + LANE] = _rope(q[:, c0:c0 + LANE], cos, sin).astype(q_ref.dtype)
    kv = jnp.dot(_rms(ckv_ref[...], kvn_ref[...]).astype(bf16), wkv_ref[...], preferred_element_type=f32)
    kv_ref[...] = kv.astype(kv_ref.dtype)
    kro_ref[...] = _rope(kr_ref[...], cos, sin).astype(kro_ref.dtype)


def _mla_up(p, q_norm, kv_norm, wq, wkv, cos, sin):
    B, Lc, _ = p.shape
    tm = _tile(Lc, 544)
    nq = wq.shape[1]
    nkv = wkv.shape[1]
    return pl.pallas_call(
        _mla_up_kernel,
        grid=(B, Lc // tm),
        in_specs=[pl.BlockSpec((None, tm, MLA_Q_RANK), lambda b, i: (b, i, O_CQ // MLA_Q_RANK)),
                  pl.BlockSpec((None, tm, MLA_KV_RANK), lambda b, i: (b, i, O_CKV // MLA_KV_RANK)),
                  pl.BlockSpec((None, tm, LANE), lambda b, i: (b, i, O_KR // LANE)),
                  pl.BlockSpec((1, MLA_Q_RANK), lambda b, i: (0, 0)),
                  pl.BlockSpec((1, MLA_KV_RANK), lambda b, i: (0, 0)),
                  pl.BlockSpec((MLA_Q_RANK, nq), lambda b, i: (0, 0)),
                  pl.BlockSpec((MLA_KV_RANK, nkv), lambda b, i: (0, 0)),
                  pl.BlockSpec((tm, LANE), lambda b, i: (i, 0)),
                  pl.BlockSpec((tm, LANE), lambda b, i: (i, 0))],
        out_specs=[pl.BlockSpec((None, tm, nq), lambda b, i: (b, i, 0)),
                   pl.BlockSpec((None, tm, nkv), lambda b, i: (b, i, 0)),
                   pl.BlockSpec((None, tm, LANE), lambda b, i: (b, i, 0))],
        out_shape=[jax.ShapeDtypeStruct((B, Lc, nq), bf16),
                   jax.ShapeDtypeStruct((B, Lc, nkv), bf16),
                   jax.ShapeDtypeStruct((B, Lc, LANE), bf16)],
        compiler_params=_cp("parallel", "parallel"),
        name="mla_up",
    )(p, p, p, q_norm.reshape(1, -1), kv_norm.reshape(1, -1), wq, wkv, cos, sin)


def _s5_finish_kernel(y_ref, u_ref, d_ref, w_ref, b_ref, o_ref):
    y = y_ref[...] + d_ref[...] * u_ref[...]
    y = 0.5 * y * (1.0 + jnp.tanh(math.sqrt(2.0 / math.pi) * (y + 0.044715 * (y * y * y))))
    z = jnp.dot(y.astype(bf16), w_ref[...], preferred_element_type=f32) + b_ref[...]
    o_ref[...] = (y * _sigmoid(z)).astype(o_ref.dtype)


def _s5_finish(y, u, d_skip, glu_w, glu_b):
    B, Lc, N = u.shape
    tm = _tile(Lc, 1088)
    row = lambda b, i: (b, i, 0)
    const = lambda b, i: (0, 0)
    return pl.pallas_call(
        _s5_finish_kernel,
        grid=(B, Lc // tm),
        in_specs=[pl.BlockSpec((None, tm, N), row), pl.BlockSpec((None, tm, N), row),
                  pl.BlockSpec((1, N), const), pl.BlockSpec((N, N), const), pl.BlockSpec((1, N), const)],
        out_specs=pl.BlockSpec((None, tm, N), row),
        out_shape=jax.ShapeDtypeStruct((B, Lc, N), bf16),
        compiler_params=_cp("parallel", "parallel"),
        name="s5_finish",
    )(y, u, d_skip.reshape(1, N), glu_w.astype(bf16), glu_b.reshape(1, N))


def _gdn_finish_kernel(of_ref, ob_ref, z_ref, w_ref, o_ref):
    o = of_ref[...] + ob_ref[...]
    z = z_ref[...]
    w = w_ref[...]
    for h in range(GDN_HEADS):
        sl = slice(h * GDN_DV, (h + 1) * GDN_DV)
        zz = z[:, sl]
        o_ref[:, sl] = (_rms(o[:, sl], w) * (zz * _sigmoid(zz))).astype(o_ref.dtype)


def _gdn_finish(of, ob, p, norm_w):
    B, Lc, N = of.shape
    tm = _tile(Lc, 1088)
    row = lambda b, i: (b, i, 0)
    return pl.pallas_call(
        _gdn_finish_kernel,
        grid=(B, Lc // tm),
        in_specs=[pl.BlockSpec((None, tm, N), row), pl.BlockSpec((None, tm, N), row),
                  pl.BlockSpec((None, tm, N), lambda b, i: (b, i, O_Z // N)),
                  pl.BlockSpec((1, GDN_DV), lambda b, i: (0, 0))],
        out_specs=pl.BlockSpec((None, tm, N), row),
        out_shape=jax.ShapeDtypeStruct((B, Lc, N), bf16),
        compiler_params=_cp("parallel", "parallel"),
        name="gdn_finish",
    )(of, ob, p, norm_w.reshape(1, GDN_DV))


def _layer_norm(y, g, b, eps=1e-5):
    mu = jnp.mean(y, axis=1, keepdims=True)
    d = y - mu
    var = jnp.mean(d * d, axis=1, keepdims=True)
    return d * lax.rsqrt(var + eps) * g + b


def _merge_kernel(x_ref, a1_ref, a2_ref, w1_ref, w2_ref, ml_ref, mc_ref, g_ref, b_ref, rh_ref, rl_ref,
                  xo_ref, u_ref, lg_ref, *, C, tm, D):
    i = pl.program_id(1)
    half = tm // 2
    spans = [(0, half), (half, tm)]

    def project(r0, r1):
        return (jnp.dot(a1_ref[r0:r1, :], w1_ref[...], preferred_element_type=f32)
                + jnp.dot(a2_ref[r0:r1, :], w2_ref[...], preferred_element_type=f32))

    def finish(o, r0, r1):
        gate, shift, scale = _select_mod(ml_ref[...], mc_ref[...], i * tm + r0, r1 - r0, C, D, (2, 3, 4))
        xn = _layer_norm(DEEPNORM_ALPHA * x_ref[r0:r1, :] + gate * o, g_ref[...], b_ref[...])
        xo_ref[r0:r1, :] = xn
        u = xn * (1.0 + scale) + shift
        u_ref[r0:r1, :] = u
        uh = u.astype(bf16)
        ul = (u - uh.astype(f32)).astype(bf16)
        rh = rh_ref[...]
        lg_ref[r0:r1, :] = (jnp.dot(uh, rh, preferred_element_type=f32) + jnp.dot(ul, rh, preferred_element_type=f32)
                            + jnp.dot(uh, rl_ref[...], preferred_element_type=f32))

    o_next = project(*spans[0])
    for n, (r0, r1) in enumerate(spans):
        o = o_next
        if n + 1 < len(spans):
            o_next = project(*spans[n + 1])
        finish(o, r0, r1)


def _merge(x, a1, a2, w1, w2, mods, ln_g, ln_b, r_hi, r_lo, *, C):
    B, Lc, D = x.shape
    K1, K2 = a1.shape[2], a2.shape[2]
    tm = _tile(Lc, 544, 32)
    nb = mods.shape[0] - 1
    row = lambda b, i: (b, i, 0)
    const = lambda b, i: (0, 0)
    kern = functools.partial(_merge_kernel, C=C, tm=tm, D=D)
    return pl.pallas_call(
        kern,
        grid=(B, Lc // tm),
        in_specs=[pl.BlockSpec((None, tm, D), row),
                  pl.BlockSpec((None, tm, K1), row),
                  pl.BlockSpec((None, tm, K2), row),
                  pl.BlockSpec((K1, D), const, pipeline_mode=pl.Buffered(1)),
                  pl.BlockSpec((K2, D), const, pipeline_mode=pl.Buffered(1)),
                  pl.BlockSpec((None, 1, 6 * D), lambda b, i: (b, 0, 0)),
                  pl.BlockSpec((None, 1, 6 * D), lambda b, i: (nb, 0, 0)),
                  pl.BlockSpec((1, D), const),
                  pl.BlockSpec((1, D), const),
                  pl.BlockSpec((D, LANE), const),
                  pl.BlockSpec((D, LANE), const)],
        out_specs=[pl.BlockSpec((None, tm, D), row),
                   pl.BlockSpec((None, tm, D), row),
                   pl.BlockSpec((None, tm, LANE), row)],
        out_shape=[jax.ShapeDtypeStruct((B, Lc, D), f32),
                   jax.ShapeDtypeStruct((B, Lc, D), f32),
                   jax.ShapeDtypeStruct((B, Lc, LANE), f32)],
        compiler_params=_cp("parallel", "parallel"),
        name="merge",
    )(x, a1, a2, w1, w2, mods, mods, ln_g.reshape(1, D), ln_b.reshape(1, D), r_hi, r_lo)


def _router_kernel(lg_ref, bias_ref, o_ref):
    lg = lg_ref[...]
    shape = lg.shape
    lane = lax.broadcasted_iota(jnp.int32, shape, 1)
    valid = lane < N_EXPERTS
    neg = -jnp.inf
    scores = _sigmoid(lg)
    biased = jnp.where(valid, scores + bias_ref[...], neg)

    def first_argmax(v):
        m = jnp.max(v, axis=1, keepdims=True)
        idx = jnp.min(jnp.where(v == m, lane, LANE), axis=1, keepdims=True)
        return m, idx

    best_score = None
    best_group = None
    for g in range(N_GROUPS):
        in_g = jnp.logical_and(lane >= g * EXPERTS_PER_GROUP, lane < (g + 1) * EXPERTS_PER_GROUP)
        vals = jnp.where(in_g, biased, neg)
        m1, i1 = first_argmax(vals)
        m2, _ = first_argmax(jnp.where(lane == i1, neg, vals))
        gs = m1 + m2
        if g == 0:
            best_score, best_group = gs, jnp.zeros_like(i1)
        else:
            better = gs > best_score
            best_score = jnp.where(better, gs, best_score)
            best_group = jnp.where(better, g, best_group)
    in_best = jnp.logical_and(lane >= best_group * EXPERTS_PER_GROUP, lane < (best_group + 1) * EXPERTS_PER_GROUP)
    vals = jnp.where(in_best, biased, neg)
    _, e0 = first_argmax(vals)
    _, e1 = first_argmax(jnp.where(lane == e0, neg, vals))
    w0 = jnp.sum(jnp.where(lane == e0, scores, 0.0), axis=1, keepdims=True)
    w1 = jnp.sum(jnp.where(lane == e1, scores, 0.0), axis=1, keepdims=True)
    tot = w0 + w1
    out = jnp.where(lane == 0, e0.astype(f32), jnp.where(lane == 1, e1.astype(f32),
                    jnp.where(lane == 2, w0 / tot, jnp.where(lane == 3, w1 / tot, 0.0))))
    o_ref[...] = out


def _router(logits, bias_row):
    T = logits.shape[0]
    tm = _tile(T, 1088, 8)
    return pl.pallas_call(
        _router_kernel,
        grid=(T // tm,),
        in_specs=[pl.BlockSpec((tm, LANE), lambda i: (i, 0)), pl.BlockSpec((1, LANE), lambda i: (0, 0))],
        out_specs=pl.BlockSpec((tm, LANE), lambda i: (i, 0)),
        out_shape=jax.ShapeDtypeStruct((T, LANE), f32),
        compiler_params=_cp("parallel"),
        name="router",
    )(logits, bias_row)


MOE_CODE_BITS = 16


def _moe_kernel(te_ref, na_ref, code_ref, u_hbm, wg_ref, wu_ref, wd_ref, y_hbm, xbuf, ybuf, gsem, ssem, *, tm):
    i = pl.program_id(0)
    na = na_ref[0]
    slot = i % 2

    def gather_row(tile, sl, r):
        tok = code_ref[tile * tm + r] >> MOE_CODE_BITS
        pltpu.make_async_copy(u_hbm.at[pl.ds(tok, 1), :], xbuf.at[sl, pl.ds(r, 1), :], gsem.at[sl]).start()

    def wait_gather(sl):
        pltpu.make_async_copy(u_hbm.at[pl.ds(0, tm), :], xbuf.at[sl], gsem.at[sl]).wait()

    def wait_scatter(sl):
        pltpu.make_async_copy(ybuf.at[sl], y_hbm.at[pl.ds(0, tm), :], ssem.at[sl]).wait()

    @pl.when(i < na)
    def _():
        @pl.when(i == 0)
        def _():
            lax.fori_loop(0, tm, lambda r, c: (gather_row(0, 0, r), c)[1], 0)
            ybuf[1] = jnp.zeros(ybuf.shape[1:], f32)
            spare = pltpu.make_async_copy(ybuf.at[1], y_hbm.at[pl.ds(y_hbm.shape[0] - tm, tm), :], ssem.at[1])
            spare.start()
            spare.wait()

        wait_gather(slot)

        @pl.when(i >= 2)
        def _():
            wait_scatter(slot)

        nxt = jnp.minimum(i + 1, na - 1)
        for r in range(tm):
            gather_row(nxt, 1 - slot, r)
        x = xbuf[slot].astype(bf16)
        g = jnp.dot(x, wg_ref[...], preferred_element_type=f32)
        u = jnp.dot(x, wu_ref[...], preferred_element_type=f32)
        h = (g * _sigmoid(g) * u).astype(bf16)
        ybuf[slot] = jnp.dot(h, wd_ref[...], preferred_element_type=f32)
        for r in range(tm):
            dst = code_ref[i * tm + r] & ((1 << MOE_CODE_BITS) - 1)
            pltpu.make_async_copy(ybuf.at[slot, pl.ds(r, 1), :], y_hbm.at[pl.ds(dst, 1), :],
                                  ssem.at[slot]).start(priority=r % 2)

        @pl.when(i == na - 1)
        def _():
            wait_gather(1 - slot)
            wait_scatter(slot)

            @pl.when(na >= 2)
            def _():
                wait_scatter(1 - slot)


def _moe_experts(u, code, tile_expert, n_active, wg, wu, wd, layer, n_out_rows):
    T, D = u.shape
    F = wg.shape[3]
    tm = MOE_TM
    n_tiles = code.shape[0] // tm
    grid_spec = pltpu.PrefetchScalarGridSpec(
        num_scalar_prefetch=3,
        grid=(n_tiles,),
        in_specs=[pl.BlockSpec(memory_space=pl.ANY),
                  pl.BlockSpec((None, None, D, F), lambda i, te, na, cd: (layer, te[i], 0, 0)),
                  pl.BlockSpec((None, None, D, F), lambda i, te, na, cd: (layer, te[i], 0, 0)),
                  pl.BlockSpec((None, None, F, D), lambda i, te, na, cd: (layer, te[i], 0, 0))],
        out_specs=pl.BlockSpec(memory_space=pl.ANY),
        scratch_shapes=[pltpu.VMEM((2, tm, D), f32), pltpu.VMEM((2, tm, D), f32),
                        pltpu.SemaphoreType.DMA((2,)), pltpu.SemaphoreType.DMA((2,))],
    )
    return pl.pallas_call(
        functools.partial(_moe_kernel, tm=tm),
        grid_spec=grid_spec,
        out_shape=jax.ShapeDtypeStruct((n_out_rows, D), f32),
        compiler_params=_cp("arbitrary"),
        name="moe_experts",
    )(tile_expert, n_active, code, u, wg, wu, wd)


def _final_ln_kernel(x_ref, y0_ref, y1_ref, r_ref, ml_ref, mc_ref, g_ref, b_ref, o_ref, *, C, tm, D):
    i = pl.program_id(1)
    (gate,) = _select_mod(ml_ref[...], mc_ref[...], i * tm, tm, C, D, (5,))
    r = r_ref[...]
    f = r[:, 2:3] * y0_ref[...] + r[:, 3:4] * y1_ref[...]
    o_ref[...] = _layer_norm(DEEPNORM_ALPHA * x_ref[...] + gate * f, g_ref[...], b_ref[...])


def _final_ln(x, y2, r, mods, ln_g, ln_b, *, C):
    B, Lc, D = x.shape
    tm = _tile(Lc, 544)
    nI = Lc // tm
    nb = mods.shape[0] - 1
    row = lambda b, i: (b, i, 0)
    kern = functools.partial(_final_ln_kernel, C=C, tm=tm, D=D)
    return pl.pallas_call(
        kern,
        grid=(B, nI),
        in_specs=[pl.BlockSpec((None, tm, D), row),
                  pl.BlockSpec((tm, D), lambda b, i: (b * nI + i, 0)),
                  pl.BlockSpec((tm, D), lambda b, i: (B * nI + b * nI + i, 0)),
                  pl.BlockSpec((tm, LANE), lambda b, i: (b * nI + i, 0)),
                  pl.BlockSpec((None, 1, 6 * D), lambda b, i: (b, 0, 0)),
                  pl.BlockSpec((None, 1, 6 * D), lambda b, i: (nb, 0, 0)),
                  pl.BlockSpec((1, D), lambda b, i: (0, 0)), pl.BlockSpec((1, D), lambda b, i: (0, 0))],
        out_specs=pl.BlockSpec((None, tm, D), row),
        out_shape=jax.ShapeDtypeStruct((B, Lc, D), f32),
        compiler_params=_cp("parallel", "parallel"),
        name="final_ln",
    )(x, y2, y2, r, mods, mods, ln_g.reshape(1, D), ln_b.reshape(1, D))


S5_PAIRS = S5_GROUPS // 2
S5_TILE = 8


def _s5_pair_tables(lam_re, lam_im, b_re, b_im, c_re, c_im, log_dt):
    T1, G, P, J = S5_BLOCK, S5_GROUPS, S5_STATE, S5_GROUP
    n = T1 * J
    t_idx = jnp.arange(T1)
    lag = t_idx[None, :] - t_idx[:, None]
    toep = 0.0
    bc_secs, cc_secs, a_secs = [], [], []
    for d in range(2):
        tb = _s5_tables_dir(lam_re[d], lam_im[d], b_re[d], b_im[d], c_re[d], c_im[d], log_dt[d])
        kern, pw, b_bar, c, lam_dt = tb
        use = (lag >= 0) if d == 0 else (lag <= 0)
        kk = jnp.transpose(kern[jnp.clip(jnp.abs(lag), 0, T1 - 1)], (2, 0, 4, 1, 3))
        toep = toep + jnp.where(use[None, :, None, :, None], kk, 0.0).reshape(G, n, n)
        p_in = pw[T1 - 1 - t_idx] if d == 0 else pw[t_idx]
        bc = jnp.transpose(p_in[:, :, :, None] * b_bar[None], (1, 0, 3, 2)).reshape(G, n, P)
        p_out = pw[t_idx + 1] if d == 0 else pw[T1 - t_idx]
        cc = jnp.transpose(c[None] * p_out[:, :, None, :], (1, 3, 0, 2)).reshape(G, P, n)
        bc_secs += [bc.real, bc.imag]
        cc_secs += [cc.real, -cc.imag]
        for k in range(1, S5_TILE + 1):
            ak = jnp.exp(lam_dt * float(T1 * k))
            a_secs += [ak.real, ak.imag]
    place = lambda outer, inner: np.stack([
        np.kron(np.eye(outer), np.kron(np.eye(2)[:, g2:g2 + 1], np.eye(inner))) for g2 in range(2)]).astype(np.float32)
    tok, sec = place(T1, J), place(4, P)
    put = lambda rows, t, cols: jnp.einsum('xra,pxab,xcb->prc', rows, t.reshape(S5_PAIRS, 2, n, n), cols)
    tp = put(tok, toep, tok)
    bc = put(tok, jnp.stack(bc_secs, axis=2), sec)
    cc = put(sec, jnp.stack(cc_secs, axis=1), tok)
    apw = jnp.stack(a_secs, axis=0).reshape(2, S5_TILE, 2, S5_PAIRS, 2 * P)
    apw = jnp.transpose(apw, (3, 0, 1, 2, 4)).reshape(S5_PAIRS, 4 * S5_TILE, 2 * P)
    order = (jnp.arange(S5_TILE), jnp.arange(S5_TILE - 1, -1, -1))
    atile = jnp.stack([apw[:, d * 2 * S5_TILE + 2 * order[d] + ri] for d in range(2) for ri in range(2)], axis=1)
    return tp.astype(bf16), bc.astype(bf16), cc.astype(bf16), apw, atile.reshape(S5_PAIRS, 4 * S5_TILE, 2 * P)


def _s5_tables_dir(lam_re, lam_im, b_re, b_im, c_re, c_im, log_dt):
    T1 = S5_BLOCK
    lam = lax.complex(lam_re.astype(f32), lam_im.astype(f32))
    lam_dt = lam * jnp.exp(log_dt.astype(f32))[:, None]
    lam_bar = jnp.exp(lam_dt)
    b_bar = ((lam_bar - 1.0) / lam)[..., None] * lax.complex(b_re.astype(f32), b_im.astype(f32))
    c = lax.complex(c_re.astype(f32), c_im.astype(f32))
    pw = jnp.exp(lam_dt[None] * jnp.arange(T1 + 1, dtype=f32)[:, None, None])
    kern = jnp.einsum('gip,tgp,gpj->tgij', c, pw[:T1], b_bar, precision=lax.Precision.HIGHEST).real
    return kern, pw, b_bar, c, lam_dt


def _s5_kernel(s_ref, toep_ref, bc_ref, cc_ref, apw_ref, atile_ref, y_ref, *, nb, nb_ctx):
    T1, W, NT = S5_BLOCK, 2 * S5_GROUP, S5_TILE
    Q = LANE // W
    nt, nt_ctx = nb // NT, nb_ctx // NT
    slabs = [s_ref[pl.ds(t, nb, stride=T1), :] for t in range(T1)]
    grp = lax.broadcasted_iota(jnp.int32, (nb, LANE), 1) // W
    row = lax.broadcasted_iota(jnp.int32, (nb, LANE), 0)
    row8 = row % NT
    roll_rows = lambda x, k: pltpu.roll(x, k % nb, axis=0)
    out = [None] * T1
    for q in range(Q):
        cols = []
        for v in range(T1 // Q):
            acc = None
            for r in range(Q):
                shift = ((r - q) * W) % LANE
                piece = slabs[Q * v + r] if shift == 0 else pltpu.roll(slabs[Q * v + r], shift, axis=1)
                acc = piece if acc is None else jnp.where(grp == r, piece, acc)
            cols.append(acc)
        u = jnp.concatenate(cols, axis=1).astype(bf16)
        gin = jnp.dot(u, bc_ref[q], preferred_element_type=f32)
        states = []
        for d in range(2):
            apow = lambda k, ri: apw_ref[q, d * 2 * NT + 2 * (k - 1) + ri:d * 2 * NT + 2 * (k - 1) + ri + 1, :]
            hr, hi = gin[:, 2 * d * LANE:(2 * d + 1) * LANE], gin[:, (2 * d + 1) * LANE:(2 * d + 2) * LANE]
            for k in (1, 2, 4):
                sr, si = (roll_rows(hr, k), roll_rows(hi, k)) if d == 0 else (roll_rows(hr, -k), roll_rows(hi, -k))
                keep = (row8 >= k) if d == 0 else (row8 < NT - k)
                sr, si = jnp.where(keep, sr, 0.0), jnp.where(keep, si, 0.0)
                ar, ai = apow(k, 0), apow(k, 1)
                hr, hi = hr + ar * sr - ai * si, hi + ar * si + ai * sr
            a8r, a8i = apow(NT, 0), apow(NT, 1)
            order = range(nt) if d == 0 else list(range(nt_ctx - 1, -1, -1)) + list(range(nt - 1, nt_ctx - 1, -1))
            end = NT - 1 if d == 0 else 0
            cr = ci = jnp.zeros((1, LANE), f32)
            enter_r, enter_i = [None] * nt, [None] * nt
            for t in order:
                enter_r[t], enter_i[t] = cr, ci
                er, ei = hr[t * NT + end:t * NT + end + 1, :], hi[t * NT + end:t * NT + end + 1, :]
                cr, ci = a8r * cr - a8i * ci + er, a8r * ci + a8i * cr + ei
            cfr = jnp.concatenate([jnp.broadcast_to(c, (NT, LANE)) for c in enter_r], axis=0)
            cfi = jnp.concatenate([jnp.broadcast_to(c, (NT, LANE)) for c in enter_i], axis=0)
            tr = jnp.concatenate([atile_ref[q, 2 * d * NT:(2 * d + 1) * NT, :]] * nt, axis=0)
            ti = jnp.concatenate([atile_ref[q, (2 * d + 1) * NT:(2 * d + 2) * NT, :]] * nt, axis=0)
            hr, hi = hr + tr * cfr - ti * cfi, hi + tr * cfi + ti * cfr
            pr, pi = (roll_rows(hr, 1), roll_rows(hi, 1)) if d == 0 else (roll_rows(hr, -1), roll_rows(hi, -1))
            first = (row == 0) if d == 0 else (row == nb_ctx - 1)
            states += [jnp.where(first, 0.0, pr), jnp.where(first, 0.0, pi)]
        hp = jnp.concatenate(states, axis=1).astype(bf16)
        y = (jnp.dot(u, toep_ref[q], preferred_element_type=f32)
             + jnp.dot(hp, cc_ref[q], preferred_element_type=f32))
        for t in range(T1):
            v, r = divmod(t, Q)
            shift = ((q - r) * W) % LANE
            col = y[:, v * LANE:(v + 1) * LANE]
            piece = col if shift == 0 else pltpu.roll(col, shift, axis=1)
            out[t] = piece if out[t] is None else jnp.where(grp == q, piece, out[t])
    for t in range(T1):
        y_ref[pl.ds(t, nb, stride=T1), :] = out[t]


def _s5_mixer(s, C, lam_re, lam_im, b_re, b_im, c_re, c_im, log_dt):
    B, Lc, N = s.shape
    T1 = S5_BLOCK
    assert C % (T1 * S5_TILE) == 0 and Lc % (T1 * S5_TILE) == 0
    nb = Lc // T1
    n2 = 2 * T1 * S5_GROUP
    Q = LANE // (2 * S5_GROUP)
    toep, bc, cc, apw, atile = _s5_pair_tables(lam_re, lam_im, b_re, b_im, c_re, c_im, log_dt)
    kern = functools.partial(_s5_kernel, nb=nb, nb_ctx=C // T1)
    tab = lambda rows, cols: pl.BlockSpec((Q, rows, cols), lambda cb, b: (cb, 0, 0))
    return pl.pallas_call(
        kern,
        grid=(N // LANE, B),
        in_specs=[pl.BlockSpec((None, Lc, LANE), lambda cb, b: (b, 0, cb)),
                  tab(n2, n2), tab(n2, 4 * LANE), tab(4 * LANE, n2), tab(4 * S5_TILE, LANE), tab(4 * S5_TILE, LANE)],
        out_specs=pl.BlockSpec((None, Lc, LANE), lambda cb, b: (b, 0, cb)),
        out_shape=jax.ShapeDtypeStruct((B, Lc, N), f32),
        compiler_params=_cp("parallel", "parallel"),
        name="s5_scan",
    )(s, toep, bc, cc, apw, atile)


def _dotT(a, b):
    return lax.dot_general(a.astype(bf16), b.astype(bf16), (((1,), (1,)), ((), ())), preferred_element_type=f32)


def _dot(a, b):
    return jnp.dot(a.astype(bf16), b.astype(bf16), preferred_element_type=f32)


def _split(a):
    hi = a.astype(bf16)
    return hi, (a - hi.astype(f32)).astype(bf16)


def _dot_split(a, b):
    n = a.shape[1]
    ah, al = _split(a)
    bh, bl = _split(b)
    lhs = jnp.concatenate([ah, al], axis=1)
    rhs = jnp.concatenate([jnp.concatenate([bh, bl], axis=1),
                           jnp.concatenate([bh, jnp.zeros_like(bl)], axis=1)], axis=0)
    out = jnp.dot(lhs, rhs, preferred_element_type=f32)
    return out[:, :n] + out[:, n:]


def _gdn_chunk_index(n, n_ctx, n_all, reverse):
    if not reverse:
        return n
    return jnp.where(n < n_ctx, n_ctx - 1 - n, n_all - 1 - (n - n_ctx))


def _gdn_kernel(x_ref, xp_ref, xn_ref, gt_ref, cw_ref, alog_ref, dtb_ref, o_ref, s_scr, *, C, Lc, Cn, reverse):
    n = pl.program_id(1)
    n_ctx, n_all = C // Cn, Lc // Cn
    ci = _gdn_chunk_index(n, n_ctx, n_all, reverse)
    H, dk, dv = GDN_HEADS, GDN_DK, GDN_DV
    d = 1 if reverse else 0

    @pl.when(n == 0)
    def _():
        s_scr[...] = jnp.zeros_like(s_scr)

    x = x_ref[...]
    loc = lax.broadcasted_iota(jnp.int32, (Cn, 1), 0)
    row = ci * Cn + loc
    prev = jnp.where(loc == 0, xp_ref[7:8, :], pltpu.roll(x, 1, axis=0))
    prev = jnp.where(jnp.logical_or(row == 0, row == C), 0.0, prev)
    nxt = jnp.where(loc == Cn - 1, xn_ref[0:1, :], pltpu.roll(x, Cn - 1, axis=0))
    nxt = jnp.where(jnp.logical_or(row == C - 1, row == Lc - 1), 0.0, nxt)
    y = prev * cw_ref[0:1, :] + x * cw_ref[1:2, :] + nxt * cw_ref[2:3, :]
    y = y * _sigmoid(y)

    graw = gt_ref[...]
    z = graw + dtb_ref[...]
    g_all = -jnp.exp(alog_ref[...]) * (jnp.maximum(z, 0.0) + jnp.log(1.0 + jnp.exp(-jnp.abs(z))))
    beta_all = _sigmoid(graw)

    ii = lax.broadcasted_iota(jnp.int32, (Cn, Cn), 0)
    jj = lax.broadcasted_iota(jnp.int32, (Cn, Cn), 1)
    incl = (ii <= jj) if reverse else (ii >= jj)
    strict = (ii < jj) if reverse else (ii > jj)
    eye = (ii == jj).astype(f32)
    tri = incl.astype(bf16)
    g_hi = g_all.astype(bf16)
    g_lo = (g_all - g_hi.astype(f32)).astype(bf16)
    gc_all = jnp.dot(tri, g_hi, preferred_element_type=f32) + jnp.dot(tri, g_lo, preferred_element_type=f32)
    gc_all_t = gc_all.T
    last = 0 if reverse else Cn - 1

    heads = range(H)
    gc = [gc_all[:, 8 * d + h:8 * d + h + 1] for h in heads]
    gl = [gc_all[last:last + 1, 8 * d + h:8 * d + h + 1] for h in heads]
    beta = [beta_all[:, 16 + 8 * d + h:17 + 8 * d + h] for h in heads]
    decay = [jnp.where(incl, jnp.exp(jnp.where(incl, gc[h] - gc_all_t[8 * d + h:8 * d + h + 1, :], 0.0)), 0.0)
             for h in heads]
    l2 = lambda t: t * lax.rsqrt(jnp.sum(t * t, axis=1, keepdims=True) + 1e-6)
    q = [l2(y[:, h * dk:(h + 1) * dk]) * (dk ** -0.5) for h in heads]
    k = [l2(y[:, H * dk + h * dk:H * dk + (h + 1) * dk]) for h in heads]
    v = [y[:, 2 * H * dk + h * dv:2 * H * dk + (h + 1) * dv] for h in heads]
    kb = [k[h] * beta[h] for h in heads]
    kq = [_dotT(jnp.concatenate([kb[h], q[h]], axis=0), k[h]) for h in heads]
    m = [jnp.where(strict, kq[h][:Cn] * decay[h], 0.0) for h in heads]
    a = [jnp.where(incl, kq[h][Cn:] * decay[h], 0.0) for h in heads]
    t_inv = [eye - m[h] for h in heads]
    pw = m
    for _ in range(int(math.log2(Cn)) - 1):
        pw = [_dot_split(pw[h], pw[h]) for h in heads]
        t_inv = [t_inv[h] + _dot_split(t_inv[h], pw[h]) for h in heads]
    e_gc = [jnp.exp(gc[h]) for h in heads]
    uw = [_dot(t_inv[h], jnp.concatenate([v[h] * beta[h], kb[h] * e_gc[h]], axis=1)) for h in heads]
    s_old = [s_scr[h] for h in heads]
    r = [_dot(jnp.concatenate([uw[h][:, dv:], q[h] * e_gc[h]], axis=0), s_old[h]) for h in heads]
    v_new = [uw[h][:, :dv] - r[h][:Cn] for h in heads]
    for h in heads:
        o_ref[:, h * dv:(h + 1) * dv] = r[h][Cn:] + _dot(a[h], v_new[h])
    for h in heads:
        kd = k[h] * jnp.exp(gl[h] - gc[h])
        s_scr[h] = s_old[h] * jnp.exp(gl[h]) + lax.dot_general(
            kd.astype(bf16), v_new[h].astype(bf16), (((0,), (0,)), ((), ())), preferred_element_type=f32)


def _gdn_direction(p, conv_w, alog_row, dtb_row, *, C, reverse):
    B, Lc, _ = p.shape
    Cn = LANE
    n_ctx, n_all = C // Cn, Lc // Cn
    nq = GDN_QKV
    ci = lambda n: _gdn_chunk_index(n, n_ctx, n_all, reverse)
    sub = Cn // 8
    kern = functools.partial(_gdn_kernel, C=C, Lc=Lc, Cn=Cn, reverse=reverse)
    return pl.pallas_call(
        kern,
        grid=(B, n_all),
        in_specs=[pl.BlockSpec((None, Cn, nq), lambda b, n: (b, ci(n), 0)),
                  pl.BlockSpec((None, 8, nq), lambda b, n: (b, jnp.maximum(ci(n) * sub - 1, 0), 0)),
                  pl.BlockSpec((None, 8, nq), lambda b, n: (b, jnp.minimum((ci(n) + 1) * sub, Lc // 8 - 1), 0)),
                  pl.BlockSpec((None, Cn, LANE), lambda b, n: (b, ci(n), O_GATE // LANE)),
                  pl.BlockSpec((3, nq), lambda b, n: (0, 0)),
                  pl.BlockSpec((1, LANE), lambda b, n: (0, 0)),
                  pl.BlockSpec((1, LANE), lambda b, n: (0, 0))],
        out_specs=pl.BlockSpec((None, Cn, GDN_HEADS * GDN_DV), lambda b, n: (b, ci(n), 0)),
        out_shape=jax.ShapeDtypeStruct((B, Lc, GDN_HEADS * GDN_DV), f32),
        scratch_shapes=[pltpu.VMEM((GDN_HEADS, GDN_DK, GDN_DV), f32)],
        compiler_params=_cp("parallel", "arbitrary"),
        name="gdn_rev" if reverse else "gdn_fwd",
    )(p, p, p, p, conv_w, alog_row, dtb_row)


def _gdn_pallas(p, C, conv_w, a_log, dt_bias):
    outs = []
    for d in range(2):
        pad = lambda t: jnp.zeros((1, LANE), f32).at[0, 8 * d:8 * d + GDN_HEADS].set(t[d].astype(f32))
        outs.append(_gdn_direction(p, conv_w, pad(a_log), pad(dt_bias), C=C, reverse=(d == 1)))
    return outs


def _dispatch_plan(e_idx, n_tiles):
    T = e_idx.shape[0]
    tm = MOE_TM
    assert 2 * T + tm <= (1 << MOE_CODE_BITS) and T < (1 << (31 - MOE_CODE_BITS))
    flat = e_idx.reshape(-1)
    onehot = (flat[:, None] == jnp.arange(N_EXPERTS)[None, :]).astype(jnp.int32)
    rank = jnp.sum((jnp.cumsum(onehot, axis=0) - onehot) * onehot, axis=1)
    counts = jnp.sum(onehot, axis=0)
    padded = ((counts + tm - 1) // tm) * tm
    ends = jnp.cumsum(padded)
    offs = ends - padded
    pos = offs[flat] + rank
    slot_id = jnp.arange(2 * T, dtype=jnp.int32)
    tok, k = slot_id // 2, slot_id % 2
    spare = 2 * T + jnp.arange(n_tiles * tm, dtype=jnp.int32) % tm
    code = spare.at[pos].set((tok << MOE_CODE_BITS) | (k * T + tok))
    tile_start = jnp.arange(n_tiles, dtype=jnp.int32) * tm
    tile_expert = jnp.minimum(jnp.sum((tile_start[:, None] >= ends[None, :]).astype(jnp.int32), axis=1), N_EXPERTS - 1)
    n_active = (ends[-1] // tm).astype(jnp.int32).reshape(1)
    last_e = tile_expert[jnp.maximum(n_active[0] - 1, 0)]
    tile_expert = jnp.where(tile_start < ends[-1], tile_expert, last_e).astype(jnp.int32)
    return code, tile_expert, n_active


def _moe(u, logits, bias_row, wg, wu, wd, layer):
    B, Lc, D = u.shape
    T = B * Lc
    r = _router(logits.reshape(T, LANE), bias_row)
    n_tiles = (2 * T + MOE_TM - 1) // MOE_TM + N_EXPERTS
    code, tile_expert, n_active = _dispatch_plan(r[:, 0:2].astype(jnp.int32), n_tiles)
    return _moe_experts(u.reshape(T, D), code, tile_expert, n_active, wg, wu, wd, layer, 2 * T + MOE_TM), r


def _rope_tables(L, C):
    rows = L // GRID_W
    t_row = jnp.repeat(jnp.arange(rows), GRID_W).astype(f32)
    t_col = jnp.tile(jnp.arange(GRID_W), rows).astype(f32)
    n_freq = ROPE_DIM // 4
    inv_freq = ROPE_BASE ** (-jnp.arange(n_freq, dtype=f32) / n_freq)
    ang = jnp.concatenate([t_row[:, None] * inv_freq, t_col[:, None] * inv_freq], axis=-1)
    cos = jnp.repeat(jnp.cos(ang), 2, axis=1)
    sin = jnp.repeat(jnp.sin(ang), 2, axis=1) * jnp.tile(jnp.array([-1.0, 1.0], f32), ROPE_DIM // 2)
    one = jnp.ones((L, ROPE_DIM), f32)
    zero = jnp.zeros((L, ROPE_DIM), f32)
    ctx = lambda t, fill: jnp.concatenate([jnp.full((C, LANE), fill, f32), t], axis=0)
    return (ctx(jnp.concatenate([cos, cos], 1), 1.0), ctx(jnp.concatenate([sin, sin], 1), 0.0),
            ctx(jnp.concatenate([cos, one], 1), 1.0), ctx(jnp.concatenate([sin, zero], 1), 0.0))


def _prep_even_w_in(w):
    D = w.shape[0]
    nqk = 2 * DA_HEADS * DA_HEAD_DIM
    heads_first = lambda t: t.reshape(D, 2, DA_HEADS, DA_HEAD_DIM).transpose(0, 2, 1, 3).reshape(D, nqk)
    q = heads_first(w[:, :nqk]) * (DA_HEAD_DIM ** -0.5 * LOG2E)
    k = heads_first(w[:, nqk:2 * nqk])
    return jnp.concatenate([q, k, w[:, 2 * nqk:]], axis=1).astype(bf16)


def _prep_odd_w_in(w):
    D = w.shape[0]
    z = lambda n: jnp.zeros((D, n), w.dtype)
    c0 = MLA_Q_RANK
    c1 = c0 + MLA_KV_RANK
    c2 = c1 + MLA_ROPE
    c3 = c2 + GDN_QKV
    c4 = c3 + GDN_HEADS * GDN_DV
    parts = [w[:, c2:c3], w[:, c3:c4], w[:, :c0], w[:, c0:c1], w[:, c1:c2], z(LANE - MLA_ROPE),
             w[:, c4:], z(LANE - 4 * GDN_HEADS)]
    out = jnp.concatenate(parts, axis=1)
    assert out.shape[1] == O_IN
    return out.astype(bf16)


def _prep_mla_w(w_uq, w_ukv):
    H = MLA_HEADS
    rq = w_uq.shape[0]
    wq = w_uq.reshape(rq, H, MLA_NOPE + MLA_ROPE)
    nope = wq[:, :, :MLA_NOPE].reshape(rq, H * MLA_NOPE)
    rope = jnp.concatenate([wq[:, :, MLA_NOPE:], jnp.zeros((rq, H, LANE - MLA_ROPE), w_uq.dtype)], axis=2)
    wq_p = jnp.concatenate([nope, rope.reshape(rq, H * LANE)], axis=1).astype(bf16)
    rkv = w_ukv.shape[0]
    wkv = w_ukv.reshape(rkv, H, MLA_NOPE + MLA_V)
    wkv_p = jnp.concatenate([wkv[:, :, :MLA_NOPE].reshape(rkv, H * MLA_NOPE),
                             wkv[:, :, MLA_NOPE:].reshape(rkv, H * MLA_V)], axis=1).astype(bf16)
    return wq_p, wkv_p


def kernel(x, c, ctx, c_ctx, mod_w, mod_b, ln_g, ln_b, e_w_in, e_w_out, da_lam, da_subln, s5_lam_re, s5_lam_im, s5_b_re, s5_b_im, s5_c_re, s5_c_im, s5_log_dt, s5_d, s5_glu_w, s5_glu_b, o_w_in, o_w_out, mla_q_norm, mla_kv_norm, mla_w_uq, mla_w_ukv, gdn_conv, gdn_a_log, gdn_dt_bias, gdn_norm, router_w, router_bias, moe_w_gate, moe_w_up, moe_w_down):
    B, L, D = x.shape
    C = ctx.shape[1]
    depth = mod_w.shape[0]
    assert depth == DEPTH

    R = -(-(B + 1) // 8) * 8
    cc = jnp.concatenate([c, c_ctx[None, :], jnp.zeros((R - B - 1, D), f32)], axis=0)
    mods_all = _modulation(cc, mod_w, mod_b)[:, :B + 1].reshape(depth, B + 1, 1, 6 * D)

    cos_e, sin_e, cos_o, sin_o = _rope_tables(L, C)
    wg_all, wu_all, wd_all = moe_w_gate.astype(bf16), moe_w_up.astype(bf16), moe_w_down.astype(bf16)
    r_pad = jnp.concatenate([router_w.astype(f32), jnp.zeros((D, LANE - N_EXPERTS), f32)], axis=1)
    r_hi = r_pad.astype(bf16)
    r_lo = (r_pad - r_hi.astype(f32)).astype(bf16)
    bias_row = jnp.concatenate([router_bias.astype(f32), jnp.zeros((LANE - N_EXPERTS,), f32)]).reshape(1, LANE)

    xs = jnp.concatenate([ctx, x], axis=1)
    n_att = DA_HEADS * DA_V_DIM
    for l in range(depth):
        mods = mods_all[l]
        i = l // 2
        if l % 2 == 0:
            qkv, s = _inproj(xs, mods, _prep_even_w_in(e_w_in[i]), cos_e, sin_e, C=C,
                             n_rope_cols=4 * DA_HEADS * DA_HEAD_DIM, n_a_cols=3072, a_dtype=bf16, b_dtype=f32)
            a1 = _diff_attention(qkv, da_lam[i].astype(f32), da_subln[i].astype(f32), C=C,
                                 lam_init=0.8 - 0.6 * math.exp(-0.3 * l))
            y = _s5_mixer(s, C, s5_lam_re[i], s5_lam_im[i], s5_b_re[i], s5_b_im[i], s5_c_re[i], s5_c_im[i],
                          s5_log_dt[i])
            a2 = _s5_finish(y, s, s5_d[i], s5_glu_w[i], s5_glu_b[i])
            w_out = e_w_out[i].astype(bf16)
        else:
            (p,) = _inproj(xs, mods, _prep_odd_w_in(o_w_in[i]), cos_e, sin_e, C=C,
                           n_rope_cols=0, n_a_cols=O_IN, a_dtype=f32, b_dtype=f32)
            wq_p, wkv_p = _prep_mla_w(mla_w_uq[i], mla_w_ukv[i])
            q, kv, kr = _mla_up(p, mla_q_norm[i], mla_kv_norm[i], wq_p, wkv_p, cos_o, sin_o)
            a1 = _mla_attention(q, kv, kr, C=C)
            of, ob = _gdn_pallas(p, C, gdn_conv[i].astype(f32), gdn_a_log[i], gdn_dt_bias[i])
            a2 = _gdn_finish(of, ob, p, gdn_norm[i])
            w_out = o_w_out[i].astype(bf16)
        xs, u, logits = _merge(xs, a1, a2, w_out[:n_att], w_out[n_att:], mods, ln_g[l, 0], ln_b[l, 0],
                               r_hi, r_lo, C=C)
        y2, r = _moe(u, logits, bias_row, wg_all, wu_all, wd_all, l)
        xs = _final_ln(xs, y2, r, mods, ln_g[l, 1], ln_b[l, 1], C=C)
    return xs[:, C:, :]
```
